```python
import math
import jax, jax.numpy as jnp
from jax import lax
import numpy as np


D_MODEL = 1024
BATCH = 8
SEQ = 2048
DEPTH = 2
DEC_BATCH = 32
DEC_SEQ = 4
PAST_LEN = 16384
PAGE_SIZE = 128

N_EVEN = (DEPTH + 1) // 2
N_ODD = DEPTH // 2
D_CONV = D_MODEL // 2
CONV_K = 31
N_HEADS = 8
NOPE_DIM = 64
ROPE_DIM = 32
V_DIM = 64
Q_RANK = 512
KV_RANK = 256
ROPE_THETA = 10000.0
Q_BLOCK = 128
ATTN_SCALE = (NOPE_DIM + ROPE_DIM) ** -0.5
IN_COLS = 2 * D_CONV + Q_RANK + KV_RANK + ROPE_DIM
MIX_COLS = D_CONV + N_HEADS * V_DIM
POOL_WINDOWS = (2, 4, 8, 16)
POOL_GROUPS = len(POOL_WINDOWS)
POOL_GC = D_MODEL // POOL_GROUPS
POOL_CTX = max(POOL_WINDOWS) - 1
D_FF = 2816
N_EXPERTS = 8
TOP_K = 2
D_FF_EXPERT = 3584
EPS = 1e-6

kernel_name = "hybrid_conv_mla_pool_moe_decoder_step"


def rmsnorm(x, g):
    xf = x.astype(jnp.float32)
    y = xf * lax.rsqrt(jnp.mean(xf * xf, axis=-1, keepdims=True) + EPS)
    return (y * g.astype(jnp.float32)).astype(x.dtype)


def layernorm(x, g, b):
    xf = x.astype(jnp.float32)
    mu = jnp.mean(xf, axis=-1, keepdims=True)
    xc = xf - mu
    y = xc * lax.rsqrt(jnp.mean(xc * xc, axis=-1, keepdims=True) + EPS)
    return (y * g.astype(jnp.float32) + b.astype(jnp.float32)).astype(x.dtype)


def rope(x, pos):
    half = ROPE_DIM // 2
    inv_freq = jnp.power(ROPE_THETA, -jnp.arange(half, dtype=jnp.float32) / half)
    ang = pos.astype(jnp.float32)[:, None] * inv_freq[None, :]
    shape = (pos.shape[0],) + (1,) * (x.ndim - 3) + (half,)
    cos = jnp.cos(ang).reshape(shape)
    sin = jnp.sin(ang).reshape(shape)
    xf = x.astype(jnp.float32)
    x1, x2 = xf[..., :half], xf[..., half:]
    return jnp.concatenate([x1 * cos - x2 * sin, x2 * cos + x1 * sin], axis=-1).astype(x.dtype)


def swiglu(h, wg, wu, wd):
    return (jax.nn.silu(h @ wg) * (h @ wu)) @ wd


def causal_depthwise(ext, w, b):
    y = lax.conv_general_dilated(ext, w[:, None, :].astype(ext.dtype), window_strides=(1,), padding='VALID',
                                 dimension_numbers=('NWC', 'WIO', 'NWC'), feature_group_count=ext.shape[-1])
    return y + b.astype(y.dtype)


def mla_prompt(q_nope, q_pe, ckv, kpe, w_ukv):
    B, S, H, _ = q_nope.shape
    kv = jnp.einsum('bsr,rhd->bshd', ckv, w_ukv)
    k_nope, v = kv[..., :NOPE_DIM], kv[..., NOPE_DIM:]
    nb = S // Q_BLOCK
    qn = q_nope.reshape(B, nb, Q_BLOCK, H, NOPE_DIM).transpose(1, 0, 2, 3, 4)
    qp = q_pe.reshape(B, nb, Q_BLOCK, H, ROPE_DIM).transpose(1, 0, 2, 3, 4)
    key_pos = jnp.arange(S)

    def block(args):
        qn_b, qp_b, b_idx = args
        q_pos = b_idx * Q_BLOCK + jnp.arange(Q_BLOCK)
        s = (jnp.einsum('bqhd,bkhd->bhqk', qn_b, k_nope)
             + jnp.einsum('bqhp,bkp->bhqk', qp_b, kpe)).astype(jnp.float32) * ATTN_SCALE
        mask = key_pos[None, :] <= q_pos[:, None]
        p = jax.nn.softmax(jnp.where(mask[None, None], s, -jnp.inf), axis=-1).astype(v.dtype)
        return jnp.einsum('bhqk,bkhd->bqhd', p, v)

    out = lax.map(block, (qn, qp, jnp.arange(nb)))
    return out.transpose(1, 0, 2, 3, 4).reshape(B, S, H, V_DIM)


def mla_sample(q_nope, q_pe, ckv_new, kpe_new, cache_ckv, cache_kpe, i, page_table, w_ukv):
    DB, T, H, _ = q_nope.shape
    past = page_table.shape[1] * cache_ckv.shape[2]
    ckv_past = cache_ckv[i, page_table].reshape(DB, past, KV_RANK)
    kpe_past = cache_kpe[i, page_table].reshape(DB, past, ROPE_DIM)
    w_uk, w_uv = w_ukv[..., :NOPE_DIM], w_ukv[..., NOPE_DIM:]
    q_lat = jnp.einsum('bthn,rhn->bthr', q_nope, w_uk)
    s_past = (jnp.einsum('bthr,bsr->bhts', q_lat, ckv_past)
              + jnp.einsum('bthp,bsp->bhts', q_pe, kpe_past)).astype(jnp.float32) * ATTN_SCALE
    s_new = (jnp.einsum('bthr,bsr->bhts', q_lat, ckv_new)
             + jnp.einsum('bthp,bsp->bhts', q_pe, kpe_new)).astype(jnp.float32) * ATTN_SCALE
    causal = jnp.tril(jnp.ones((T, T), dtype=bool))
    s_new = jnp.where(causal[None, None], s_new, -jnp.inf)
    p = jax.nn.softmax(jnp.concatenate([s_past, s_new], axis=-1), axis=-1).astype(ckv_new.dtype)
    p_past, p_new = p[..., :past], p[..., past:]
    o_lat = (jnp.einsum('bhts,bsr->bthr', p_past, ckv_past)
             + jnp.einsum('bhts,bsr->bthr', p_new, ckv_new))
    return jnp.einsum('bthr,rhv->bthv', o_lat, w_uv)


def mixer_conv_mla(h, pos, conv_prev, attend, i, P):
    B, T, _ = h.shape
    proj = h @ P['w_in'][i]
    a = proj[..., :D_CONV]
    gate = proj[..., D_CONV:2 * D_CONV]
    q_dn = proj[..., 2 * D_CONV:2 * D_CONV + Q_RANK]
    kv_dn = proj[..., 2 * D_CONV + Q_RANK:]
    u = a * jax.nn.sigmoid(gate)
    ext = jnp.concatenate([conv_prev.astype(u.dtype), u], axis=1)
    c = causal_depthwise(ext, P['conv_w'][i], P['conv_b'][i])
    c = jax.nn.silu(layernorm(c, P['conv_ln_g'][i], P['conv_ln_b'][i]))
    q = (rmsnorm(q_dn, P['q_norm'][i]) @ P['w_uq'][i]).reshape(B, T, N_HEADS, NOPE_DIM + ROPE_DIM)
    q_nope, q_pe = q[..., :NOPE_DIM], rope(q[..., NOPE_DIM:], pos)
    ckv = rmsnorm(kv_dn[..., :KV_RANK], P['kv_norm'][i])
    kpe = rope(kv_dn[..., KV_RANK:], pos)
    w_ukv = P['w_ukv'][i].reshape(KV_RANK, N_HEADS, NOPE_DIM + V_DIM)
    o = attend(i, q_nope, q_pe, ckv, kpe, w_ukv)
    out = jnp.concatenate([c, o.reshape(B, T, N_HEADS * V_DIM).astype(c.dtype)], axis=-1) @ P['w_out'][i]
    return out, ext[:, -(CONV_K - 1):], ckv, kpe


def pool_mixer(h, prev, pos, w_pool, scale):
    B, T, D = h.shape
    ext = jnp.concatenate([prev.astype(h.dtype), h], axis=1)
    cs = jnp.cumsum(ext.astype(jnp.float32), axis=1)
    cs = jnp.concatenate([jnp.zeros((B, 1, D), jnp.float32), cs], axis=1)
    start = POOL_CTX + 1
    means = []
    for g, w in enumerate(POOL_WINDOWS):
        sl = slice(g * POOL_GC, (g + 1) * POOL_GC)
        s = cs[:, start:start + T, sl] - cs[:, start - w:start - w + T, sl]
        cnt = jnp.minimum(pos + 1, w).astype(jnp.float32)[None, :, None]
        means.append(s / cnt)
    diff = (jnp.concatenate(means, axis=-1) - h.astype(jnp.float32)).astype(h.dtype)
    y = jnp.einsum('btgc,gcd->btgd', diff.reshape(B, T, POOL_GROUPS, POOL_GC), w_pool).reshape(B, T, D)
    return y * scale.astype(y.dtype), ext[:, -POOL_CTX:]


def moe_swiglu(h, router_w, wg, wu, wd):
    B, T, D = h.shape
    t = h.reshape(B * T, D)
    logits = (t @ router_w).astype(jnp.float32)
    vals, idx = lax.top_k(logits, TOP_K)
    gk = jax.nn.softmax(vals, axis=-1)
    gate = jnp.sum(jax.nn.one_hot(idx, N_EXPERTS, dtype=jnp.float32) * gk[..., None], axis=1)
    out = jnp.zeros((B * T, D), jnp.float32)
    for e in range(N_EXPERTS):
        out = out + gate[:, e:e + 1] * swiglu(t, wg[e], wu[e], wd[e]).astype(jnp.float32)
    return out.astype(h.dtype).reshape(B, T, D)


def forward(x, pos, conv_prev, pool_prev, attend, P):
    ckvs, kpes, convs, pools = [], [], [], []
    for layer in range(DEPTH):
        i = layer // 2
        h = rmsnorm(x, P['norm_mix'][layer])
        if layer % 2 == 0:
            m, cst, ckv, kpe = mixer_conv_mla(h, pos, conv_prev[i], attend, i, P)
            ckvs.append(ckv); kpes.append(kpe); convs.append(cst)
        else:
            m, pst = pool_mixer(h, pool_prev[i], pos, P['pool_w'][i], P['pool_scale'][i])
            pools.append(pst)
        x = x + m
        h = rmsnorm(x, P['norm_ffn'][layer])
        if layer % 2 == 0:
            f = swiglu(h, P['ffn_w_gate'][i], P['ffn_w_up'][i], P['ffn_w_down'][i])
        else:
            f = moe_swiglu(h, P['router_w'][i], P['moe_w_gate'][i], P['moe_w_up'][i], P['moe_w_down'][i])
        x = x + f
    y = rmsnorm(x, P['norm_final'])
    return y, jnp.stack(ckvs), jnp.stack(kpes), jnp.stack(convs), jnp.stack(pools)


def setup_inputs(seed: int = 0) -> dict:
    key = jax.random.key(seed)
    ks = iter(jax.random.split(key, 40))
    f32 = jnp.float32

    def nrm(shape, fan_in):
        return jax.random.normal(next(ks), shape, f32) * (fan_in ** -0.5)

    def gain(shape):
        return 1.0 + 0.02 * jax.random.normal(next(ks), shape, f32)

    def small(shape):
        return 0.02 * jax.random.normal(next(ks), shape, f32)

    n_pages = PAST_LEN // PAGE_SIZE
    n_used = DEC_BATCH * n_pages
    n_pool = n_used + n_used // 4
    x_prompt = jax.random.normal(next(ks), (BATCH, SEQ, D_MODEL), f32)
    x_sample = jax.random.normal(next(ks), (DEC_BATCH, DEC_SEQ, D_MODEL), f32)
    cache_ckv = jax.random.normal(next(ks), (N_EVEN, n_pool, PAGE_SIZE, KV_RANK), f32)
    cache_kpe = jax.random.normal(next(ks), (N_EVEN, n_pool, PAGE_SIZE, ROPE_DIM), f32)
    page_table = jax.random.permutation(next(ks), n_pool)[:n_used].reshape(DEC_BATCH, n_pages).astype(jnp.int32)
    state_conv = 0.5 * jax.random.normal(next(ks), (N_EVEN, DEC_BATCH, CONV_K - 1, D_CONV), f32)
    state_pool = jax.random.normal(next(ks), (N_ODD, DEC_BATCH, POOL_CTX, D_MODEL), f32)
    return {
        'x_prompt': x_prompt,
        'x_sample': x_sample,
        'cache_ckv': cache_ckv,
        'cache_kpe': cache_kpe,
        'page_table': page_table,
        'state_conv': state_conv,
        'state_pool': state_pool,
        'norm_mix': gain((DEPTH, D_MODEL)),
        'norm_ffn': gain((DEPTH, D_MODEL)),
        'norm_final': gain((D_MODEL,)),
        'w_in': nrm((N_EVEN, D_MODEL, IN_COLS), D_MODEL),
        'conv_w': nrm((N_EVEN, CONV_K, D_CONV), CONV_K),
        'conv_b': small((N_EVEN, D_CONV)),
        'conv_ln_g': gain((N_EVEN, D_CONV)),
        'conv_ln_b': small((N_EVEN, D_CONV)),
        'q_norm': gain((N_EVEN, Q_RANK)),
        'w_uq': nrm((N_EVEN, Q_RANK, N_HEADS * (NOPE_DIM + ROPE_DIM)), Q_RANK),
        'kv_norm': gain((N_EVEN, KV_RANK)),
        'w_ukv': nrm((N_EVEN, KV_RANK, N_HEADS * (NOPE_DIM + V_DIM)), KV_RANK),
        'w_out': nrm((N_EVEN, MIX_COLS, D_MODEL), MIX_COLS),
        'ffn_w_gate': nrm((N_EVEN, D_MODEL, D_FF), D_MODEL),
        'ffn_w_up': nrm((N_EVEN, D_MODEL, D_FF), D_MODEL),
        'ffn_w_down': nrm((N_EVEN, D_FF, D_MODEL), D_FF),
        'pool_w': nrm((N_ODD, POOL_GROUPS, POOL_GC, POOL_GC), POOL_GC),
        'pool_scale': 0.5 + 0.1 * jax.random.normal(next(ks), (N_ODD, D_MODEL), f32),
        'router_w': nrm((N_ODD, D_MODEL, N_EXPERTS), D_MODEL),
        'moe_w_gate': nrm((N_ODD, N_EXPERTS, D_MODEL, D_FF_EXPERT), D_MODEL),
        'moe_w_up': nrm((N_ODD, N_EXPERTS, D_MODEL, D_FF_EXPERT), D_MODEL),
        'moe_w_down': nrm((N_ODD, N_EXPERTS, D_FF_EXPERT, D_MODEL), D_FF_EXPERT),
    }


def reference(x_prompt, x_sample, cache_ckv, cache_kpe, page_table, state_conv, state_pool,
              norm_mix, norm_ffn, norm_final, w_in, conv_w, conv_b, conv_ln_g, conv_ln_b,
              q_norm, w_uq, kv_norm, w_ukv, w_out, ffn_w_gate, ffn_w_up, ffn_w_down,
              pool_w, pool_scale, router_w, moe_w_gate, moe_w_up, moe_w_down):
    P = {'norm_mix': norm_mix, 'norm_ffn': norm_ffn, 'norm_final': norm_final, 'w_in': w_in,
         'conv_w': conv_w, 'conv_b': conv_b, 'conv_ln_g': conv_ln_g, 'conv_ln_b': conv_ln_b,
         'q_norm': q_norm, 'w_uq': w_uq, 'kv_norm': kv_norm, 'w_ukv': w_ukv, 'w_out': w_out,
         'ffn_w_gate': ffn_w_gate, 'ffn_w_up': ffn_w_up, 'ffn_w_down': ffn_w_down,
         'pool_w': pool_w, 'pool_scale': pool_scale, 'router_w': router_w,
         'moe_w_gate': moe_w_gate, 'moe_w_up': moe_w_up, 'moe_w_down': moe_w_down}

    Bp, Sp, _ = x_prompt.shape
    pos_p = jnp.arange(Sp, dtype=jnp.int32)
    conv0 = jnp.zeros((N_EVEN, Bp, CONV_K - 1, D_CONV), x_prompt.dtype)
    pool0 = jnp.zeros((N_ODD, Bp, POOL_CTX, D_MODEL), x_prompt.dtype)

    def attend_prompt(i, qn, qp, ckv, kpe, wkv):
        return mla_prompt(qn, qp, ckv, kpe, wkv)

    y_prompt, ckv_p, kpe_p, conv_p, pool_p = forward(x_prompt, pos_p, conv0, pool0, attend_prompt, P)

    Ts = x_sample.shape[1]
    pos_s = PAST_LEN + jnp.arange(Ts, dtype=jnp.int32)

    def attend_sample(i, qn, qp, ckv, kpe, wkv):
        return mla_sample(qn, qp, ckv, kpe, cache_ckv, cache_kpe, i, page_table, wkv)

    y_sample, ckv_s, kpe_s, conv_s, pool_s = forward(x_sample, pos_s, state_conv, state_pool, attend_sample, P)

    return (y_prompt, y_sample, ckv_p, kpe_p, conv_p, pool_p, ckv_s, kpe_s, conv_s, pool_s)
```

```python
import functools

import jax
import jax.numpy as jnp
from jax import lax
from jax.experimental import pallas as pl
from jax.experimental.pallas import tpu as pltpu

F32 = jnp.float32
BF16 = jnp.bfloat16

N_HEADS = 8
NOPE_DIM = 64
ROPE_DIM = 32
V_DIM = 64
ROPE_THETA = 10000.0
ATTN_SCALE = (NOPE_DIM + ROPE_DIM) ** -0.5
POOL_WINDOWS = (2, 4, 8, 16)
TOP_K = 2
EPS = 1e-6

HEAD_SLOT = 128
LANES = 128
NEG_BIG = -1e30
VMEM_LIMIT = 56 * 1024 * 1024

CONV_HALO = 32
POOL_HALO = 16


def _cparams(*semantics):
    return pltpu.CompilerParams(dimension_semantics=semantics, vmem_limit_bytes=VMEM_LIMIT)


def _rms(x, g):
    return x * lax.rsqrt(jnp.mean(x * x, axis=-1, keepdims=True) + EPS) * g


def _row_tile(n, want):
    if n <= want:
        return n
    t = want
    while t >= 8:
        if n % t == 0 and t % 8 == 0:
            return t
        t -= 8
    return n


def _in_proj_kernel(x_ref, g_ref, w_ref, qg_ref, kvg_ref, cos_ref, sin_ref,
                    u_ref, qn_ref, ckv_ref, kpe_ref, *, dc, qr, kvr):
    h = _rms(x_ref[...], g_ref[...])
    proj = jnp.dot(h.astype(BF16), w_ref[...], preferred_element_type=F32)
    a = proj[:, :dc]
    gate = proj[:, dc:2 * dc]
    u_ref[...] = a * jax.nn.sigmoid(gate)
    o = 2 * dc
    qn_ref[...] = _rms(proj[:, o:o + qr], qg_ref[...]).astype(BF16)
    o += qr
    ckv_ref[...] = _rms(proj[:, o:o + kvr], kvg_ref[...])
    o += kvr
    kpe_ref[...] = (proj[:, o:o + HEAD_SLOT] * cos_ref[...]
                    + proj[:, o + HEAD_SLOT:o + 2 * HEAD_SLOT] * sin_ref[...])


def _in_proj(x, g, w_ext, qg, kvg, cos_k, sin_k, *, dc, qr, kvr, tm):
    n, d = x.shape
    n_tab = cos_k.shape[0] // tm
    row = lambda i: (i, 0)
    full = lambda i: (0, 0)
    tab = lambda i: (i % n_tab, 0)
    return pl.pallas_call(
        functools.partial(_in_proj_kernel, dc=dc, qr=qr, kvr=kvr),
        grid=(n // tm,),
        in_specs=[
            pl.BlockSpec((tm, d), row),
            pl.BlockSpec((1, d), full),
            pl.BlockSpec(w_ext.shape, full),
            pl.BlockSpec((1, qr), full),
            pl.BlockSpec((1, kvr), full),
            pl.BlockSpec((tm, HEAD_SLOT), tab),
            pl.BlockSpec((tm, HEAD_SLOT), tab),
        ],
        out_specs=[
            pl.BlockSpec((tm, dc), row),
            pl.BlockSpec((tm, qr), row),
            pl.BlockSpec((tm, kvr), row),
            pl.BlockSpec((tm, HEAD_SLOT), row),
        ],
        out_shape=[
            jax.ShapeDtypeStruct((n, dc), F32),
            jax.ShapeDtypeStruct((n, qr), BF16),
            jax.ShapeDtypeStruct((n, kvr), F32),
            jax.ShapeDtypeStruct((n, HEAD_SLOT), F32),
        ],
        compiler_params=_cparams("parallel"),
        name="in_proj",
    )(x, g, w_ext, qg, kvg, cos_k, sin_k)


def _conv_kernel(*refs, tc, k_taps, rows_per_chunk, has_halo):
    if has_halo:
        prev_ref, halo_ref, u_ref, w_ref, b_ref, lg_ref, lb_ref, c_ref, ext_ref = refs
    else:
        prev_ref, u_ref, w_ref, b_ref, lg_ref, lb_ref, c_ref, ext_ref = refs
    if has_halo:
        t = pl.program_id(1)

        @pl.when(t == 0)
        def _():
            ext_ref[0:CONV_HALO, :] = prev_ref[0]

        @pl.when(t > 0)
        def _():
            ext_ref[0:CONV_HALO, :] = halo_ref[0]
    else:
        ext_ref[0:CONV_HALO, :] = prev_ref[0]
    ext_ref[CONV_HALO:CONV_HALO + tc, :] = u_ref[0]

    first = CONV_HALO - (k_taps - 1)
    for c0 in range(0, tc, rows_per_chunk):
        rc = min(rows_per_chunk, tc - c0)
        acc = w_ref[0:1, :] * ext_ref[first + c0:first + c0 + rc, :]
        for k in range(1, k_taps):
            acc = acc + w_ref[k:k + 1, :] * ext_ref[first + c0 + k:first + c0 + k + rc, :]
        acc = acc + b_ref[...]
        mu = jnp.mean(acc, axis=-1, keepdims=True)
        xc = acc - mu
        y = xc * lax.rsqrt(jnp.mean(xc * xc, axis=-1, keepdims=True) + EPS)
        y = y * lg_ref[...] + lb_ref[...]
        c_ref[0, c0:c0 + rc, :] = (y * jax.nn.sigmoid(y)).astype(BF16)


def _conv_branch(prev_pad, u, w, b, lg, lb, *, tc):
    bsz, t_len, c = u.shape
    k_taps = w.shape[0]
    n_t = t_len // tc
    has_halo = n_t > 1
    hb = tc // CONV_HALO if has_halo else 1
    in_specs = [pl.BlockSpec((1, CONV_HALO, c), lambda bi, ti: (bi, 0, 0))]
    args = [prev_pad]
    if has_halo:
        in_specs.append(pl.BlockSpec((1, CONV_HALO, c),
                                     lambda bi, ti: (bi, jnp.maximum(ti * hb - 1, 0), 0)))
        args.append(u)
    in_specs += [
        pl.BlockSpec((1, tc, c), lambda bi, ti: (bi, ti, 0)),
        pl.BlockSpec((k_taps, c), lambda bi, ti: (0, 0)),
        pl.BlockSpec((1, c), lambda bi, ti: (0, 0)),
        pl.BlockSpec((1, c), lambda bi, ti: (0, 0)),
        pl.BlockSpec((1, c), lambda bi, ti: (0, 0)),
    ]
    args += [u, w, b, lg, lb]
    return pl.pallas_call(
        functools.partial(_conv_kernel, tc=tc, k_taps=k_taps, rows_per_chunk=64, has_halo=has_halo),
        grid=(bsz, n_t),
        in_specs=in_specs,
        out_specs=pl.BlockSpec((1, tc, c), lambda bi, ti: (bi, ti, 0)),
        out_shape=jax.ShapeDtypeStruct((bsz, t_len, c), BF16),
        scratch_shapes=[pltpu.VMEM((CONV_HALO + tc, c), F32)],
        compiler_params=_cparams("parallel", "parallel"),
        name="conv_branch",
    )(*args)


def _q_proj_kernel(qn_ref, w_ref, cos_ref, sin_ref, q_ref, *, n_heads):
    proj = jnp.dot(qn_ref[...], w_ref[...], preferred_element_type=F32)
    half = n_heads * HEAD_SLOT
    cos = cos_ref[...]
    sin = sin_ref[...]
    for h in range(n_heads):
        a = proj[:, h * HEAD_SLOT:(h + 1) * HEAD_SLOT]
        b = proj[:, half + h * HEAD_SLOT:half + (h + 1) * HEAD_SLOT]
        q_ref[:, h * HEAD_SLOT:(h + 1) * HEAD_SLOT] = (a * cos + b * sin).astype(BF16)


def _q_proj(qn, w_ext, cos_q, sin_q, *, tm):
    n, qr = qn.shape
    n_tab = cos_q.shape[0] // tm
    width = N_HEADS * HEAD_SLOT
    return pl.pallas_call(
        functools.partial(_q_proj_kernel, n_heads=N_HEADS),
        grid=(n // tm,),
        in_specs=[
            pl.BlockSpec((tm, qr), lambda i: (i, 0)),
            pl.BlockSpec(w_ext.shape, lambda i: (0, 0)),
            pl.BlockSpec((tm, HEAD_SLOT), lambda i: (i % n_tab, 0)),
            pl.BlockSpec((tm, HEAD_SLOT), lambda i: (i % n_tab, 0)),
        ],
        out_specs=pl.BlockSpec((tm, width), lambda i: (i, 0)),
        out_shape=jax.ShapeDtypeStruct((n, width), BF16),
        compiler_params=_cparams("parallel"),
        name="q_proj",
    )(qn, w_ext, cos_q, sin_q)


def _kv_proj_kernel(ckv_ref, kpe_ref, w_ref, k_ref, v_ref, *, n_heads):
    proj = jnp.dot(ckv_ref[...].astype(BF16), w_ref[...], preferred_element_type=F32)
    half = n_heads * HEAD_SLOT
    kpe = kpe_ref[...]
    for h in range(n_heads):
        k_ref[:, h * HEAD_SLOT:(h + 1) * HEAD_SLOT] = (
            proj[:, h * HEAD_SLOT:(h + 1) * HEAD_SLOT] + kpe).astype(BF16)
    v_ref[...] = proj[:, half:].astype(BF16)


def _kv_proj(ckv, kpe_slot, w_ext, *, tm):
    n, kvr = ckv.shape
    width = N_HEADS * HEAD_SLOT
    return pl.pallas_call(
        functools.partial(_kv_proj_kernel, n_heads=N_HEADS),
        grid=(n // tm,),
        in_specs=[
            pl.BlockSpec((tm, kvr), lambda i: (i, 0)),
            pl.BlockSpec((tm, HEAD_SLOT), lambda i: (i, 0)),
            pl.BlockSpec(w_ext.shape, lambda i: (0, 0)),
        ],
        out_specs=[pl.BlockSpec((tm, width), lambda i: (i, 0)),
                   pl.BlockSpec((tm, width), lambda i: (i, 0))],
        out_shape=[jax.ShapeDtypeStruct((n, width), BF16),
                   jax.ShapeDtypeStruct((n, width), BF16)],
        compiler_params=_cparams("parallel"),
        name="kv_proj",
    )(ckv, kpe_slot, w_ext)


def _attn_prompt_kernel(q_ref, k_ref, v_ref, o_ref, *, tq, n_heads):
    qi = pl.program_id(1)
    row = lax.broadcasted_iota(jnp.int32, (tq, tq), 0)
    col = lax.broadcasted_iota(jnp.int32, (tq, tq), 1)
    causal = col <= row

    for h in range(n_heads):
        lanes = slice(h * HEAD_SLOT, (h + 1) * HEAD_SLOT)
        q = q_ref[:, lanes]

        def step(kt, carry, masked):
            m, l, acc = carry
            start = pl.multiple_of(kt * tq, tq)
            k = k_ref[pl.ds(start, tq), lanes]
            v = v_ref[pl.ds(start, tq), lanes]
            s = lax.dot_general(q, k, (((1,), (1,)), ((), ())), preferred_element_type=F32)
            if masked:
                s = jnp.where(causal, s, NEG_BIG)
            m_new = jnp.maximum(m, jnp.max(s, axis=-1, keepdims=True))
            alpha = jnp.exp(m - m_new)
            p = jnp.exp(s - m_new)
            l = alpha * l + jnp.sum(p, axis=-1, keepdims=True)
            acc = alpha * acc + jnp.dot(p.astype(BF16), v, preferred_element_type=F32)
            return m_new, l, acc

        init = (jnp.full((tq, 1), NEG_BIG, F32), jnp.zeros((tq, 1), F32),
                jnp.zeros((tq, HEAD_SLOT), F32))
        carry = lax.fori_loop(0, qi, functools.partial(step, masked=False), init)
        m, l, acc = step(qi, carry, True)
        o_ref[:, lanes] = (acc / l).astype(BF16)


def _attn_prompt(q, k, v, *, bsz, t_len, tq):
    width = q.shape[1]
    nq = t_len // tq
    return pl.pallas_call(
        functools.partial(_attn_prompt_kernel, tq=tq, n_heads=N_HEADS),
        grid=(bsz, nq),
        in_specs=[
            pl.BlockSpec((tq, width), lambda b, i: (b * nq + i, 0)),
            pl.BlockSpec((t_len, width), lambda b, i: (b, 0)),
            pl.BlockSpec((t_len, width), lambda b, i: (b, 0)),
        ],
        out_specs=pl.BlockSpec((tq, width), lambda b, i: (b * nq + i, 0)),
        out_shape=jax.ShapeDtypeStruct(q.shape, BF16),
        compiler_params=_cparams("parallel", "arbitrary"),
        name="attn_prompt",
    )(q, k, v)


def _qlat_kernel(q_ref, w_ref, o_ref, *, n_heads):
    for h in range(n_heads):
        o_ref[h] = jnp.dot(q_ref[:, h * HEAD_SLOT:(h + 1) * HEAD_SLOT], w_ref[h],
                           preferred_element_type=F32).astype(BF16)


def _qlat(q_slots, w_q2lat):
    n = q_slots.shape[0]
    n_heads, _, kvr = w_q2lat.shape
    return pl.pallas_call(
        functools.partial(_qlat_kernel, n_heads=n_heads),
        out_shape=jax.ShapeDtypeStruct((n_heads, n, kvr), BF16),
        compiler_params=pltpu.CompilerParams(vmem_limit_bytes=VMEM_LIMIT),
        name="q_latent",
    )(q_slots, w_q2lat)


def _attn_sample_kernel(pt_ref, ql_ref, qp_ref, cn_ref, kn_ref, *rest, pages_per_step, page, t_new, t_pad):
    g = pages_per_step
    ckv_pages = rest[:g]
    kpe_pages = rest[g:2 * g]
    o_ref, kbf_ref, pbf_ref, m_ref, l_ref, acc_ref = rest[2 * g:]
    s_idx = pl.program_id(1)
    ql = ql_ref[0]
    qp = qp_ref[0]
    rows = ql.shape[0]
    contract_last = (((1,), (1,)), ((), ()))

    @pl.when(s_idx == 0)
    def _():
        cn = cn_ref[0].astype(BF16)
        kn = kn_ref[0].astype(BF16)
        s = (lax.dot_general(ql, cn, contract_last, preferred_element_type=F32)
             + lax.dot_general(qp, kn, contract_last, preferred_element_type=F32))
        t_of_row = lax.broadcasted_iota(jnp.int32, (rows, t_pad), 0) % t_pad
        key = lax.broadcasted_iota(jnp.int32, (rows, t_pad), 1)
        ok = (key <= t_of_row) & (key < t_new)
        s = jnp.where(ok, s, NEG_BIG)
        m = jnp.max(s, axis=-1, keepdims=True)
        p = jnp.where(ok, jnp.exp(s - m), 0.0)
        m_ref[...] = m
        l_ref[...] = jnp.sum(p, axis=-1, keepdims=True)
        acc_ref[...] = jnp.dot(p.astype(BF16), cn, preferred_element_type=F32)

    for j in range(g):
        kbf_ref[j * page:(j + 1) * page, :] = ckv_pages[j][...].astype(BF16)
        pbf_ref[j * page:(j + 1) * page, :] = kpe_pages[j][...].astype(BF16)
    kb = kbf_ref[...]
    s = (lax.dot_general(ql, kb, contract_last, preferred_element_type=F32)
         + lax.dot_general(qp, pbf_ref[...], contract_last, preferred_element_type=F32))
    m_old = m_ref[...]
    m_new = jnp.maximum(m_old, jnp.max(s, axis=-1, keepdims=True))
    alpha = jnp.exp(m_old - m_new)
    p = jnp.exp(s - m_new)
    m_ref[...] = m_new
    l_ref[...] = alpha * l_ref[...] + jnp.sum(p, axis=-1, keepdims=True)
    acc_ref[...] = alpha * acc_ref[...] + jnp.dot(p.astype(BF16), kb, preferred_element_type=F32)

    @pl.when(s_idx == pl.num_programs(1) - 1)
    def _():
        o_ref[0] = acc_ref[...] / l_ref[...]


def _attn_sample(page_table, q_lat, q_pe, ckv_new, kpe_new, cache_ckv, cache_kpe, layer, *,
                 t_new, t_pad, pages_per_step):
    bsz, rows, kvr = q_lat.shape
    rope = q_pe.shape[2]
    n_pages = page_table.shape[1]
    page = cache_ckv.shape[2]
    g = pages_per_step
    n_steps = n_pages // g

    def page_spec(width, j):
        return pl.BlockSpec((None, None, page, width),
                            lambda b, s, pt: (layer, pt[b, s * g + j], 0, 0))

    in_specs = [
        pl.BlockSpec((1, rows, kvr), lambda b, s, pt: (b, 0, 0)),
        pl.BlockSpec((1, rows, rope), lambda b, s, pt: (b, 0, 0)),
        pl.BlockSpec((1, t_pad, kvr), lambda b, s, pt: (b, 0, 0)),
        pl.BlockSpec((1, t_pad, rope), lambda b, s, pt: (b, 0, 0)),
    ]
    in_specs += [page_spec(kvr, j) for j in range(g)]
    in_specs += [page_spec(rope, j) for j in range(g)]
    return pl.pallas_call(
        functools.partial(_attn_sample_kernel, pages_per_step=g, page=page, t_new=t_new, t_pad=t_pad),
        grid_spec=pltpu.PrefetchScalarGridSpec(
            num_scalar_prefetch=1,
            grid=(bsz, n_steps),
            in_specs=in_specs,
            out_specs=pl.BlockSpec((1, rows, kvr), lambda b, s, pt: (b, 0, 0)),
            scratch_shapes=[
                pltpu.VMEM((g * page, kvr), BF16),
                pltpu.VMEM((g * page, rope), BF16),
                pltpu.VMEM((rows, 1), F32),
                pltpu.VMEM((rows, 1), F32),
                pltpu.VMEM((rows, kvr), F32),
            ],
        ),
        out_shape=jax.ShapeDtypeStruct((bsz, rows, kvr), F32),
        compiler_params=_cparams("parallel", "arbitrary"),
        name="attn_sample",
    )(page_table, q_lat, q_pe, ckv_new, kpe_new, *([cache_ckv] * g), *([cache_kpe] * g))


def _olat_kernel(o_ref, w_ref, out_ref, *, n_heads):
    for h in range(n_heads):
        out_ref[:, h * HEAD_SLOT:(h + 1) * HEAD_SLOT] = jnp.dot(
            o_ref[h].astype(BF16), w_ref[h], preferred_element_type=F32).astype(BF16)


def _olat(o_lat_heads, w_uv_pad):
    n_heads, n, _ = o_lat_heads.shape
    return pl.pallas_call(
        functools.partial(_olat_kernel, n_heads=n_heads),
        out_shape=jax.ShapeDtypeStruct((n, n_heads * HEAD_SLOT), BF16),
        compiler_params=pltpu.CompilerParams(vmem_limit_bytes=VMEM_LIMIT),
        name="o_latent",
    )(o_lat_heads, w_uv_pad)


def _out_proj_kernel(x_ref, c_ref, o_ref, wc_ref, wo_ref, g_ref, x1_ref, h_ref):
    mix = (jnp.dot(c_ref[...], wc_ref[...], preferred_element_type=F32)
           + jnp.dot(o_ref[...], wo_ref[...], preferred_element_type=F32))
    x1 = x_ref[...] + mix
    x1_ref[...] = x1
    h_ref[...] = _rms(x1, g_ref[...]).astype(BF16)


def _out_proj(x, c, o_slots, w_c, w_o, g, *, tm):
    n, d = x.shape
    row = lambda i: (i, 0)
    full = lambda i: (0, 0)
    return pl.pallas_call(
        _out_proj_kernel,
        grid=(n // tm,),
        in_specs=[
            pl.BlockSpec((tm, d), row),
            pl.BlockSpec((tm, c.shape[1]), row),
            pl.BlockSpec((tm, o_slots.shape[1]), row),
            pl.BlockSpec(w_c.shape, full),
            pl.BlockSpec(w_o.shape, full),
            pl.BlockSpec((1, d), full),
        ],
        out_specs=[pl.BlockSpec((tm, d), row), pl.BlockSpec((tm, d), row)],
        out_shape=[jax.ShapeDtypeStruct((n, d), F32), jax.ShapeDtypeStruct((n, d), BF16)],
        compiler_params=_cparams("parallel"),
        name="out_proj",
    )(x, c, o_slots, w_c, w_o, g)


def _swiglu_kernel(te_ref, nv_ref, x_ref, wg_ref, wu_ref, wd_ref, side_ref, o_ref, acc_ref, *, residual):
    m = pl.program_id(0)
    f = pl.program_id(1)
    n_f = pl.num_programs(1)

    @pl.when(m < nv_ref[0])
    def _():
        x = x_ref[...]
        a = jnp.dot(x, wg_ref[...].astype(BF16), preferred_element_type=F32)
        b = jnp.dot(x, wu_ref[...].astype(BF16), preferred_element_type=F32)
        mid = (a * jax.nn.sigmoid(a) * b).astype(BF16)
        part = jnp.dot(mid, wd_ref[...].astype(BF16), preferred_element_type=F32)

        @pl.when(f == 0)
        def _():
            acc_ref[...] = part

        @pl.when(f > 0)
        def _():
            acc_ref[...] += part

        @pl.when(f == n_f - 1)
        def _():
            if residual:
                o_ref[...] = side_ref[...] + acc_ref[...]
            else:
                o_ref[...] = side_ref[...] * acc_ref[...]


def _swiglu(tile_expert, n_valid, x, wg, wu, wd, side, *, tm, tf, residual):
    p, d = x.shape
    d_ff = wg.shape[2]
    n_f = d_ff // tf
    n_tiles = p // tm

    def mm(m, nv):
        return jnp.minimum(m, nv[0] - 1)

    def ff(m, f, nv):
        return jnp.where(m < nv[0], f, n_f - 1)

    side_w = side.shape[1]
    return pl.pallas_call(
        functools.partial(_swiglu_kernel, residual=residual),
        grid_spec=pltpu.PrefetchScalarGridSpec(
            num_scalar_prefetch=2,
            grid=(n_tiles, n_f),
            in_specs=[
                pl.BlockSpec((tm, d), lambda m, f, te, nv: (mm(m, nv), 0)),
                pl.BlockSpec((None, d, tf), lambda m, f, te, nv: (te[mm(m, nv)], 0, ff(m, f, nv))),
                pl.BlockSpec((None, d, tf), lambda m, f, te, nv: (te[mm(m, nv)], 0, ff(m, f, nv))),
                pl.BlockSpec((None, tf, d), lambda m, f, te, nv: (te[mm(m, nv)], ff(m, f, nv), 0)),
                pl.BlockSpec((tm, side_w), lambda m, f, te, nv: (mm(m, nv), 0)),
            ],
            out_specs=pl.BlockSpec((tm, d), lambda m, f, te, nv: (mm(m, nv), 0)),
            scratch_shapes=[pltpu.VMEM((tm, d), F32)],
        ),
        out_shape=jax.ShapeDtypeStruct((p, d), F32),
        compiler_params=_cparams("arbitrary", "arbitrary"),
        name="swiglu_residual" if residual else "swiglu_gated",
    )(tile_expert, n_valid, x, wg, wu, wd, side)


def _pool_kernel(*refs, tm, pos0, windows, has_halo, n_experts):
    if has_halo:
        (prev_ref, halo_ref, x_ref, g1_ref, wp_ref, sc_ref, g2_ref, wr_ref,
         x3_ref, h_ref, hn_ref, gate_ref, idx_ref, ext_ref) = refs
    else:
        (prev_ref, x_ref, g1_ref, wp_ref, sc_ref, g2_ref, wr_ref,
         x3_ref, h_ref, hn_ref, gate_ref, idx_ref, ext_ref) = refs
    t = pl.program_id(1)
    x = x_ref[0]
    hn = _rms(x, g1_ref[...])
    hn_ref[0] = hn
    if has_halo:
        @pl.when(t == 0)
        def _():
            ext_ref[0:POOL_HALO, :] = prev_ref[0]

        @pl.when(t > 0)
        def _():
            ext_ref[0:POOL_HALO, :] = _rms(halo_ref[0], g1_ref[...])
    else:
        ext_ref[0:POOL_HALO, :] = prev_ref[0]
    ext_ref[POOL_HALO:POOL_HALO + tm, :] = hn

    d = x.shape[1]
    gc = d // len(windows)
    pos = pos0 + t * tm + lax.broadcasted_iota(jnp.int32, (tm, 1), 0)
    for gi, w in enumerate(windows):
        lanes = slice(gi * gc, (gi + 1) * gc)
        s = ext_ref[POOL_HALO:POOL_HALO + tm, lanes]
        for j in range(1, w):
            s = s + ext_ref[POOL_HALO - j:POOL_HALO - j + tm, lanes]
        cnt = jnp.minimum(pos + 1, w).astype(F32)
        diff = s / cnt - hn[:, lanes]
        y = jnp.dot(diff.astype(BF16), wp_ref[gi], preferred_element_type=F32)
        x3_ref[0, :, lanes] = x[:, lanes] + y * sc_ref[:, lanes]

    x3 = x3_ref[0]
    h2 = _rms(x3, g2_ref[...])
    h_ref[0] = h2.astype(BF16)
    logits = jnp.dot(h2, wr_ref[...], preferred_element_type=F32, precision=lax.Precision.HIGHEST)
    lane = lax.broadcasted_iota(jnp.int32, logits.shape, 1)
    logits = jnp.where(lane < n_experts, logits, -jnp.inf)
    m1 = jnp.max(logits, axis=-1, keepdims=True)
    i1 = jnp.min(jnp.where(logits == m1, lane, LANES), axis=-1, keepdims=True)
    rest = jnp.where(lane == i1, -jnp.inf, logits)
    m2 = jnp.max(rest, axis=-1, keepdims=True)
    i2 = jnp.min(jnp.where(rest == m2, lane, LANES), axis=-1, keepdims=True)
    e = jnp.exp(m2 - m1)
    g1 = 1.0 / (1.0 + e)
    g2 = e / (1.0 + e)
    gate_ref[0] = jnp.where(lane == 0, g1, jnp.where(lane == 1, g2, 0.0))
    idx_ref[0] = jnp.where(lane == 0, i1, jnp.where(lane == 1, i2, 0))


def _pool_layer(prev_pad, x, g1, w_pool, scale, g2, w_router_pad, *, tm, pos0, n_experts):
    bsz, t_len, d = x.shape
    n_t = t_len // tm
    has_halo = n_t > 1
    hb = tm // POOL_HALO if has_halo else 1
    cur = lambda b, t: (b, t, 0)
    full2 = lambda b, t: (0, 0)
    in_specs = [pl.BlockSpec((1, POOL_HALO, d), lambda b, t: (b, 0, 0))]
    args = [prev_pad]
    if has_halo:
        in_specs.append(pl.BlockSpec((1, POOL_HALO, d), lambda b, t: (b, jnp.maximum(t * hb - 1, 0), 0)))
        args.append(x)
    in_specs += [
        pl.BlockSpec((1, tm, d), cur),
        pl.BlockSpec((1, d), full2),
        pl.BlockSpec(w_pool.shape, lambda b, t: (0, 0, 0)),
        pl.BlockSpec((1, d), full2),
        pl.BlockSpec((1, d), full2),
        pl.BlockSpec(w_router_pad.shape, full2),
    ]
    args += [x, g1, w_pool, scale, g2, w_router_pad]
    return pl.pallas_call(
        functools.partial(_pool_kernel, tm=tm, pos0=pos0, windows=POOL_WINDOWS, has_halo=has_halo,
                          n_experts=n_experts),
        grid=(bsz, n_t),
        in_specs=in_specs,
        out_specs=[
            pl.BlockSpec((1, tm, d), cur),
            pl.BlockSpec((1, tm, d), cur),
            pl.BlockSpec((1, tm, d), cur),
            pl.BlockSpec((1, tm, LANES), cur),
            pl.BlockSpec((1, tm, LANES), cur),
        ],
        out_shape=[
            jax.ShapeDtypeStruct((bsz, t_len, d), F32),
            jax.ShapeDtypeStruct((bsz, t_len, d), BF16),
            jax.ShapeDtypeStruct((bsz, t_len, d), F32),
            jax.ShapeDtypeStruct((bsz, t_len, LANES), F32),
            jax.ShapeDtypeStruct((bsz, t_len, LANES), jnp.int32),
        ],
        scratch_shapes=[pltpu.VMEM((POOL_HALO + tm, d), F32)],
        compiler_params=_cparams("parallel", "parallel"),
        name="pool_layer",
    )(*args)


def _final_kernel(x_ref, ya_ref, yb_ref, g_ref, y_ref):
    y_ref[...] = _rms(x_ref[...] + (ya_ref[...] + yb_ref[...]), g_ref[...])


def _final(x, ya, yb, g, *, tm):
    n, d = x.shape
    row = lambda i: (i, 0)
    return pl.pallas_call(
        _final_kernel,
        grid=(n // tm,),
        in_specs=[pl.BlockSpec((tm, d), row)] * 3 + [pl.BlockSpec((1, d), lambda i: (0, 0))],
        out_specs=pl.BlockSpec((tm, d), row),
        out_shape=jax.ShapeDtypeStruct((n, d), F32),
        compiler_params=_cparams("parallel"),
        name="final_norm",
    )(x, ya, yb, g)


def _swap_halves(w):
    half = w.shape[-1] // 2
    return jnp.concatenate([w[..., half:], w[..., :half]], axis=-1)


def _slot(parts, total=HEAD_SLOT):
    used = sum(p.shape[-1] for p in parts)
    pad = jnp.zeros(parts[0].shape[:-1] + (total - used,), parts[0].dtype)
    return jnp.concatenate(list(parts) + [pad], axis=-1)


def _prep_even_weights(w_in, w_uq, w_ukv, w_out, *, dc, qr, kvr):
    d = w_in.shape[0]
    body = w_in[:, :2 * dc + qr + kvr]
    w_kpe = w_in[:, 2 * dc + qr + kvr:]
    z_nope = jnp.zeros((d, NOPE_DIM), w_in.dtype)
    w_in_ext = jnp.concatenate(
        [body, _slot([z_nope, w_kpe]), _slot([z_nope, _swap_halves(w_kpe)])], axis=-1).astype(BF16)

    uq = w_uq.reshape(qr, N_HEADS, NOPE_DIM + ROPE_DIM)
    uq_nope, uq_pe = uq[..., :NOPE_DIM], uq[..., NOPE_DIM:]
    qa = _slot([uq_nope, uq_pe]).reshape(qr, N_HEADS * HEAD_SLOT)
    qb = _slot([jnp.zeros_like(uq_nope), _swap_halves(uq_pe)]).reshape(qr, N_HEADS * HEAD_SLOT)
    w_uq_ext = jnp.concatenate([qa, qb], axis=-1).astype(BF16)

    ukv = w_ukv.reshape(kvr, N_HEADS, NOPE_DIM + V_DIM)
    uk, uv = ukv[..., :NOPE_DIM], ukv[..., NOPE_DIM:]
    w_kv_ext = jnp.concatenate([_slot([uk]).reshape(kvr, N_HEADS * HEAD_SLOT),
                                _slot([uv]).reshape(kvr, N_HEADS * HEAD_SLOT)], axis=-1).astype(BF16)
    w_q2lat = jnp.concatenate(
        [jnp.transpose(uk, (1, 2, 0)),
         jnp.zeros((N_HEADS, HEAD_SLOT - NOPE_DIM, kvr), w_ukv.dtype)], axis=1).astype(BF16)
    w_uv_pad = _slot([jnp.transpose(uv, (1, 0, 2))]).astype(BF16)

    w_c = w_out[:dc].astype(BF16)
    wo = w_out[dc:].reshape(N_HEADS, V_DIM, d)
    w_o = jnp.concatenate([wo, jnp.zeros((N_HEADS, HEAD_SLOT - V_DIM, d), w_out.dtype)],
                          axis=1).reshape(N_HEADS * HEAD_SLOT, d).astype(BF16)
    return w_in_ext, w_uq_ext, w_kv_ext, w_q2lat, w_uv_pad, w_c, w_o


def _rope_tables(pos):
    half = ROPE_DIM // 2
    inv_freq = jnp.power(ROPE_THETA, -jnp.arange(half, dtype=F32) / half)
    ang = pos.astype(F32)[:, None] * inv_freq[None, :]
    cos, sin = jnp.cos(ang), jnp.sin(ang)
    n = pos.shape[0]
    z_nope = jnp.zeros((n, NOPE_DIM), F32)
    cos_k = _slot([z_nope, cos, cos])
    sin_k = _slot([z_nope, -sin, sin])
    cos_q = _slot([jnp.ones((n, NOPE_DIM), F32), cos, cos]) * ATTN_SCALE
    sin_q = sin_k * ATTN_SCALE
    return cos_k, sin_k, cos_q, sin_q


def _even_layer(x, conv_prev, pos, attend, P, layer, i):
    bsz, t_len, d = x.shape
    n = bsz * t_len
    dc = P['conv_w'].shape[2]
    qr = P['q_norm'].shape[1]
    kvr = P['kv_norm'].shape[1]
    k_taps = P['conv_w'].shape[1]
    tm = _row_tile(n, 512) if t_len >= 512 else n
    w_in_ext, w_uq_ext, w_kv_ext, w_q2lat, w_uv_pad, w_c, w_o = _prep_even_weights(
        P['w_in'][i], P['w_uq'][i], P['w_ukv'][i], P['w_out'][i], dc=dc, qr=qr, kvr=kvr)
    cos_k, sin_k, cos_q, sin_q = _rope_tables(pos)
    if t_len < tm:
        reps = tm // t_len
        cos_k, sin_k, cos_q, sin_q = (jnp.tile(a, (reps, 1)) for a in (cos_k, sin_k, cos_q, sin_q))

    xf = x.reshape(n, d)
    u, qn, ckv, kpe_slot = _in_proj(
        xf, P['norm_mix'][layer][None], w_in_ext, P['q_norm'][i][None], P['kv_norm'][i][None],
        cos_k, sin_k, dc=dc, qr=qr, kvr=kvr, tm=tm)

    u3 = u.reshape(bsz, t_len, dc)
    prev_pad = jnp.pad(conv_prev.astype(F32), ((0, 0), (CONV_HALO - (k_taps - 1), 0), (0, 0)))
    c = _conv_branch(prev_pad, u3, P['conv_w'][i], P['conv_b'][i][None], P['conv_ln_g'][i][None],
                     P['conv_ln_b'][i][None], tc=min(t_len, 512))
    conv_state = jnp.concatenate([conv_prev.astype(F32), u3], axis=1)[:, -(k_taps - 1):]

    q_slots = _q_proj(qn, w_uq_ext, cos_q, sin_q, tm=tm)
    o_slots = attend(q_slots, ckv, kpe_slot, w_kv_ext, w_q2lat, w_uv_pad, tm)

    x1, h_ffn = _out_proj(xf, c.reshape(n, dc), o_slots, w_c, w_o, P['norm_ffn'][layer][None], tm=tm)

    tm_f = _row_tile(n, 1024)
    n_tiles = n // tm_f
    x2 = _swiglu(jnp.full((n_tiles,), i, jnp.int32), jnp.full((1,), n_tiles, jnp.int32), h_ffn,
                 P['ffn_w_gate'], P['ffn_w_up'], P['ffn_w_down'], x1, tm=tm_f, tf=256, residual=True)
    kpe = kpe_slot[:, NOPE_DIM:NOPE_DIM + ROPE_DIM]
    return (x2.reshape(bsz, t_len, d), conv_state, ckv.reshape(bsz, t_len, kvr),
            kpe.reshape(bsz, t_len, ROPE_DIM))


def _make_attend_prompt(bsz, t_len):
    def attend(q_slots, ckv, kpe_slot, w_kv_ext, w_q2lat, w_uv_pad, tm):
        k_slots, v_slots = _kv_proj(ckv, kpe_slot, w_kv_ext, tm=tm)
        return _attn_prompt(q_slots, k_slots, v_slots, bsz=bsz, t_len=t_len, tq=min(t_len, 512))
    return attend


def _make_attend_sample(bsz, t_len, cache_ckv, cache_kpe, page_table, i):
    t_pad = 8

    def attend(q_slots, ckv, kpe_slot, w_kv_ext, w_q2lat, w_uv_pad, tm):
        n = bsz * t_len
        kvr = ckv.shape[1]
        q_lat = _qlat(q_slots, w_q2lat)
        q_lat = q_lat.reshape(N_HEADS, bsz, t_len, kvr)
        q_lat = jnp.pad(q_lat, ((0, 0), (0, 0), (0, t_pad - t_len), (0, 0)))
        q_lat = jnp.transpose(q_lat, (1, 0, 2, 3)).reshape(bsz, N_HEADS * t_pad, kvr)
        q_pe = q_slots.reshape(bsz, t_len, N_HEADS, HEAD_SLOT)[..., NOPE_DIM:NOPE_DIM + ROPE_DIM]
        q_pe = jnp.pad(jnp.transpose(q_pe, (0, 2, 1, 3)), ((0, 0), (0, 0), (0, t_pad - t_len), (0, 0)))
        q_pe = q_pe.reshape(bsz, N_HEADS * t_pad, ROPE_DIM)
        pad_t = ((0, 0), (0, t_pad - t_len), (0, 0))
        ckv_new = jnp.pad(ckv.reshape(bsz, t_len, kvr), pad_t)
        kpe_new = jnp.pad(kpe_slot[:, NOPE_DIM:NOPE_DIM + ROPE_DIM].reshape(bsz, t_len, ROPE_DIM), pad_t)
        o_lat = _attn_sample(page_table, q_lat, q_pe, ckv_new, kpe_new, cache_ckv, cache_kpe, i,
                             t_new=t_len, t_pad=t_pad, pages_per_step=16)
        o_lat = o_lat.reshape(bsz, N_HEADS, t_pad, kvr)[:, :, :t_len]
        o_lat = jnp.transpose(o_lat, (1, 0, 2, 3)).reshape(N_HEADS, n, kvr)
        return _olat(o_lat, w_uv_pad)
    return attend


def _odd_layer_mixer(x, pool_prev, pos0, P, layer, i):
    bsz, t_len, d = x.shape
    n_experts = P['router_w'].shape[2]
    ctx = max(POOL_WINDOWS) - 1
    prev_pad = jnp.pad(pool_prev.astype(F32), ((0, 0), (POOL_HALO - ctx, 0), (0, 0)))
    w_router_pad = jnp.pad(P['router_w'][i], ((0, 0), (0, LANES - n_experts)))
    x3, h_moe, hn, gates, ids = _pool_layer(
        prev_pad, x, P['norm_mix'][layer][None], P['pool_w'][i].astype(BF16), P['pool_scale'][i][None],
        P['norm_ffn'][layer][None], w_router_pad, tm=min(t_len, 512), pos0=pos0, n_experts=n_experts)
    pool_state = jnp.concatenate([pool_prev.astype(F32), hn], axis=1)[:, -ctx:]
    return x3, h_moe, pool_state, gates[..., :TOP_K], ids[..., :TOP_K]


def _moe_dispatch(ids, n_experts, tm):
    n = ids.shape[0]
    flat = ids.reshape(-1)
    onehot = (flat[:, None] == jnp.arange(n_experts, dtype=jnp.int32)[None, :]).astype(jnp.int32)
    csum = jnp.cumsum(onehot, axis=0)
    counts = csum[-1]
    rank = jnp.take_along_axis(csum, flat[:, None], axis=1)[:, 0] - 1
    padded = ((counts + tm - 1) // tm) * tm
    ends = jnp.cumsum(padded)
    starts = ends - padded
    pos = starts[flat] + rank
    n_tiles = (n * TOP_K + tm - 1) // tm + n_experts
    n_valid = (ends[-1] // tm).astype(jnp.int32)
    tile_start = jnp.arange(n_tiles, dtype=jnp.int32) * tm
    tile_expert = jnp.minimum(jnp.searchsorted(ends, tile_start, side='right'), n_experts - 1)
    return pos.reshape(n, TOP_K), tile_expert.astype(jnp.int32), n_valid.reshape(1), n_tiles


def _moe(h_all, gates, ids, P, i, *, tm):
    n, d = h_all.shape
    n_experts = P['router_w'].shape[2]
    pos, tile_expert, n_valid, n_tiles = _moe_dispatch(ids, n_experts, tm)
    p_rows = n_tiles * tm
    token = jnp.repeat(jnp.arange(n, dtype=jnp.int32), TOP_K)
    src = jnp.zeros((p_rows,), jnp.int32).at[pos.reshape(-1)].set(token)
    gate_rows = jnp.zeros((p_rows,), F32).at[pos.reshape(-1)].set(gates.reshape(-1))
    xs = jnp.take(h_all, src, axis=0)
    ys = _swiglu(tile_expert, n_valid, xs, P['moe_w_gate'][i], P['moe_w_up'][i], P['moe_w_down'][i],
                 gate_rows[:, None], tm=tm, tf=512, residual=False)
    return ys, pos


def kernel(x_prompt, x_sample, cache_ckv, cache_kpe, page_table, state_conv, state_pool, norm_mix, norm_ffn,
           norm_final, w_in, conv_w, conv_b, conv_ln_g, conv_ln_b, q_norm, w_uq, kv_norm, w_ukv, w_out,
           ffn_w_gate, ffn_w_up, ffn_w_down, pool_w, pool_scale, router_w, moe_w_gate, moe_w_up, moe_w_down):
    P = {'norm_mix': norm_mix, 'norm_ffn': norm_ffn, 'norm_final': norm_final, 'w_in': w_in,
         'conv_w': conv_w, 'conv_b': conv_b, 'conv_ln_g': conv_ln_g, 'conv_ln_b': conv_ln_b,
         'q_norm': q_norm, 'w_uq': w_uq, 'kv_norm': kv_norm, 'w_ukv': w_ukv, 'w_out': w_out,
         'ffn_w_gate': ffn_w_gate, 'ffn_w_up': ffn_w_up, 'ffn_w_down': ffn_w_down,
         'pool_w': pool_w, 'pool_scale': pool_scale, 'router_w': router_w,
         'moe_w_gate': moe_w_gate, 'moe_w_up': moe_w_up, 'moe_w_down': moe_w_down}
    depth = norm_mix.shape[0]
    d = x_prompt.shape[2]
    bp, tp, _ = x_prompt.shape
    bs, ts, _ = x_sample.shape
    past = page_table.shape[1] * cache_ckv.shape[2]
    k_taps = conv_w.shape[1]
    ctx = max(POOL_WINDOWS) - 1
    pos_p = jnp.arange(tp, dtype=jnp.int32)
    pos_s = past + jnp.arange(ts, dtype=jnp.int32)

    xp, xs = x_prompt, x_sample
    outs_p = {'ckv': [], 'kpe': [], 'conv': [], 'pool': []}
    outs_s = {'ckv': [], 'kpe': [], 'conv': [], 'pool': []}
    for layer in range(depth):
        i = layer // 2
        if layer % 2 == 0:
            conv0 = jnp.zeros((bp, k_taps - 1, conv_w.shape[2]), F32)
            xp, cst, ckv, kpe = _even_layer(xp, conv0, pos_p, _make_attend_prompt(bp, tp), P, layer, i)
            outs_p['conv'].append(cst); outs_p['ckv'].append(ckv); outs_p['kpe'].append(kpe)
            xs, cst, ckv, kpe = _even_layer(
                xs, state_conv[i], pos_s, _make_attend_sample(bs, ts, cache_ckv, cache_kpe, page_table, i),
                P, layer, i)
            outs_s['conv'].append(cst); outs_s['ckv'].append(ckv); outs_s['kpe'].append(kpe)
        else:
            pool0 = jnp.zeros((bp, ctx, d), F32)
            x3p, hp, pst_p, gp, ip = _odd_layer_mixer(xp, pool0, 0, P, layer, i)
            x3s, hs, pst_s, gs, is_ = _odd_layer_mixer(xs, state_pool[i], past, P, layer, i)
            outs_p['pool'].append(pst_p); outs_s['pool'].append(pst_s)
            n_p, n_s = bp * tp, bs * ts
            h_all = jnp.concatenate([hp.reshape(n_p, d), hs.reshape(n_s, d)], axis=0)
            gates = jnp.concatenate([gp.reshape(n_p, TOP_K), gs.reshape(n_s, TOP_K)], axis=0)
            ids = jnp.concatenate([ip.reshape(n_p, TOP_K), is_.reshape(n_s, TOP_K)], axis=0)
            ys, pos = _moe(h_all, gates, ids, P, i, tm=1024)
            last = layer == depth - 1
            g_fin = norm_final[None] if last else None
            new = []
            for x3, lo, hi in ((x3p, 0, n_p), (x3s, n_p, n_p + n_s)):
                ya = jnp.take(ys, pos[lo:hi, 0], axis=0)
                yb = jnp.take(ys, pos[lo:hi, 1], axis=0)
                x3f = x3.reshape(hi - lo, d)
                if last:
                    new.append(_final(x3f, ya, yb, g_fin, tm=_row_tile(hi - lo, 512)).reshape(x3.shape))
                else:
                    new.append((x3f + (ya + yb)).reshape(x3.shape))
            xp, xs = new
    if depth % 2 == 1:
        zero = jnp.zeros_like
        xp = _final(xp.reshape(bp * tp, d), zero(xp).reshape(bp * tp, d), zero(xp).reshape(bp * tp, d),
                    norm_final[None], tm=_row_tile(bp * tp, 512)).reshape(xp.shape)
        xs = _final(xs.reshape(bs * ts, d), zero(xs).reshape(bs * ts, d), zero(xs).reshape(bs * ts, d),
                    norm_final[None], tm=_row_tile(bs * ts, 512)).reshape(xs.shape)

    def stack(lst, shape_if_empty):
        return jnp.stack(lst) if lst else jnp.zeros(shape_if_empty, F32)

    return (xp, xs,
            jnp.stack(outs_p['ckv']), jnp.stack(outs_p['kpe']), jnp.stack(outs_p['conv']),
            stack(outs_p['pool'], (0, bp, ctx, d)),
            jnp.stack(outs_s['ckv']), jnp.stack(outs_s['kpe']), jnp.stack(outs_s['conv']),
            stack(outs_s['pool'], (0, bs, ctx, d)))
```

```python
import functools

import jax
import jax.numpy as jnp
from jax import lax
from jax.experimental import pallas as pl
from jax.experimental.pallas import tpu as pltpu

F32 = jnp.float32
BF16 = jnp.bfloat16

N_HEADS = 8
NOPE_DIM = 64
ROPE_DIM = 32
V_DIM = 64
ROPE_THETA = 10000.0
ATTN_SCALE = (NOPE_DIM + ROPE_DIM) ** -0.5
POOL_WINDOWS = (2, 4, 8, 16)
TOP_K = 2
EPS = 1e-6

HEAD_SLOT = 128
LANES = 128
NEG_BIG = -1e30
VMEM_LIMIT = 56 * 1024 * 1024

CONV_HALO = 32
POOL_HALO = 16


def _cparams(*semantics):
    return pltpu.CompilerParams(dimension_semantics=semantics, vmem_limit_bytes=VMEM_LIMIT)


def _rms(x, g):
    return x * lax.rsqrt(jnp.mean(x * x, axis=-1, keepdims=True) + EPS) * g


def _row_tile(n, want):
    if n <= want:
        return n
    t = want
    while t >= 8:
        if n % t == 0 and t % 8 == 0:
            return t
        t -= 8
    return n


def _in_proj_kernel(x_ref, g_ref, w_ref, qg_ref, kvg_ref, cos_ref, sin_ref,
                    u_ref, qn_ref, ckv_ref, kpe_ref, *, dc, qr, kvr):
    h = _rms(x_ref[...], g_ref[...])
    proj = jnp.dot(h.astype(BF16), w_ref[...], preferred_element_type=F32)
    a = proj[:, :dc]
    gate = proj[:, dc:2 * dc]
    u_ref[...] = a * jax.nn.sigmoid(gate)
    o = 2 * dc
    qn_ref[...] = _rms(proj[:, o:o + qr], qg_ref[...]).astype(BF16)
    o += qr
    ckv_ref[...] = _rms(proj[:, o:o + kvr], kvg_ref[...])
    o += kvr
    kpe_ref[...] = (proj[:, o:o + HEAD_SLOT] * cos_ref[...]
                    + proj[:, o + HEAD_SLOT:o + 2 * HEAD_SLOT] * sin_ref[...])


def _in_proj(x, g, w_ext, qg, kvg, cos_k, sin_k, *, dc, qr, kvr, tm):
    n, d = x.shape
    n_tab = cos_k.shape[0] // tm
    row = lambda i: (i, 0)
    full = lambda i: (0, 0)
    tab = lambda i: (i % n_tab, 0)
    return pl.pallas_call(
        functools.partial(_in_proj_kernel, dc=dc, qr=qr, kvr=kvr),
        grid=(n // tm,),
        in_specs=[
            pl.BlockSpec((tm, d), row),
            pl.BlockSpec((1, d), full),
            pl.BlockSpec(w_ext.shape, full),
            pl.BlockSpec((1, qr), full),
            pl.BlockSpec((1, kvr), full),
            pl.BlockSpec((tm, HEAD_SLOT), tab),
            pl.BlockSpec((tm, HEAD_SLOT), tab),
        ],
        out_specs=[
            pl.BlockSpec((tm, dc), row),
            pl.BlockSpec((tm, qr), row),
            pl.BlockSpec((tm, kvr), row),
            pl.BlockSpec((tm, HEAD_SLOT), row),
        ],
        out_shape=[
            jax.ShapeDtypeStruct((n, dc), F32),
            jax.ShapeDtypeStruct((n, qr), BF16),
            jax.ShapeDtypeStruct((n, kvr), F32),
            jax.ShapeDtypeStruct((n, HEAD_SLOT), F32),
        ],
        compiler_params=_cparams("parallel"),
        name="in_proj",
    )(x, g, w_ext, qg, kvg, cos_k, sin_k)


def _conv_kernel(*refs, tc, k_taps, rows_per_chunk, has_halo):
    if has_halo:
        prev_ref, halo_ref, u_ref, w_ref, b_ref, lg_ref, lb_ref, c_ref, ext_ref = refs
    else:
        prev_ref, u_ref, w_ref, b_ref, lg_ref, lb_ref, c_ref, ext_ref = refs
    if has_halo:
        t = pl.program_id(1)

        @pl.when(t == 0)
        def _():
            ext_ref[0:CONV_HALO, :] = prev_ref[0]

        @pl.when(t > 0)
        def _():
            ext_ref[0:CONV_HALO, :] = halo_ref[0]
    else:
        ext_ref[0:CONV_HALO, :] = prev_ref[0]
    ext_ref[CONV_HALO:CONV_HALO + tc, :] = u_ref[0]

    first = CONV_HALO - (k_taps - 1)
    for c0 in range(0, tc, rows_per_chunk):
        rc = min(rows_per_chunk, tc - c0)
        acc = w_ref[0:1, :] * ext_ref[first + c0:first + c0 + rc, :]
        for k in range(1, k_taps):
            acc = acc + w_ref[k:k + 1, :] * ext_ref[first + c0 + k:first + c0 + k + rc, :]
        acc = acc + b_ref[...]
        mu = jnp.mean(acc, axis=-1, keepdims=True)
        xc = acc - mu
        y = xc * lax.rsqrt(jnp.mean(xc * xc, axis=-1, keepdims=True) + EPS)
        y = y * lg_ref[...] + lb_ref[...]
        c_ref[0, c0:c0 + rc, :] = (y * jax.nn.sigmoid(y)).astype(BF16)


def _conv_branch(prev_pad, u, w, b, lg, lb, *, tc):
    bsz, t_len, c = u.shape
    k_taps = w.shape[0]
    n_t = t_len // tc
    has_halo = n_t > 1
    hb = tc // CONV_HALO if has_halo else 1
    in_specs = [pl.BlockSpec((1, CONV_HALO, c), lambda bi, ti: (bi, 0, 0))]
    args = [prev_pad]
    if has_halo:
        in_specs.append(pl.BlockSpec((1, CONV_HALO, c),
                                     lambda bi, ti: (bi, jnp.maximum(ti * hb - 1, 0), 0)))
        args.append(u)
    in_specs += [
        pl.BlockSpec((1, tc, c), lambda bi, ti: (bi, ti, 0)),
        pl.BlockSpec((k_taps, c), lambda bi, ti: (0, 0)),
        pl.BlockSpec((1, c), lambda bi, ti: (0, 0)),
        pl.BlockSpec((1, c), lambda bi, ti: (0, 0)),
        pl.BlockSpec((1, c), lambda bi, ti: (0, 0)),
    ]
    args += [u, w, b, lg, lb]
    return pl.pallas_call(
        functools.partial(_conv_kernel, tc=tc, k_taps=k_taps, rows_per_chunk=64, has_halo=has_halo),
        grid=(bsz, n_t),
        in_specs=in_specs,
        out_specs=pl.BlockSpec((1, tc, c), lambda bi, ti: (bi, ti, 0)),
        out_shape=jax.ShapeDtypeStruct((bsz, t_len, c), BF16),
        scratch_shapes=[pltpu.VMEM((CONV_HALO + tc, c), F32)],
        compiler_params=_cparams("parallel", "parallel"),
        name="conv_branch",
    )(*args)


def _q_proj_kernel(qn_ref, w_ref, cos_ref, sin_ref, q_ref, *, n_heads):
    proj = jnp.dot(qn_ref[...], w_ref[...], preferred_element_type=F32)
    half = n_heads * HEAD_SLOT
    cos = cos_ref[...]
    sin = sin_ref[...]
    for h in range(n_heads):
        a = proj[:, h * HEAD_SLOT:(h + 1) * HEAD_SLOT]
        b = proj[:, half + h * HEAD_SLOT:half + (h + 1) * HEAD_SLOT]
        q_ref[:, h * HEAD_SLOT:(h + 1) * HEAD_SLOT] = (a * cos + b * sin).astype(BF16)


def _q_proj(qn, w_ext, cos_q, sin_q, *, tm):
    n, qr = qn.shape
    n_tab = cos_q.shape[0] // tm
    width = N_HEADS * HEAD_SLOT
    return pl.pallas_call(
        functools.partial(_q_proj_kernel, n_heads=N_HEADS),
        grid=(n // tm,),
        in_specs=[
            pl.BlockSpec((tm, qr), lambda i: (i, 0)),
            pl.BlockSpec(w_ext.shape, lambda i: (0, 0)),
            pl.BlockSpec((tm, HEAD_SLOT), lambda i: (i % n_tab, 0)),
            pl.BlockSpec((tm, HEAD_SLOT), lambda i: (i % n_tab, 0)),
        ],
        out_specs=pl.BlockSpec((tm, width), lambda i: (i, 0)),
        out_shape=jax.ShapeDtypeStruct((n, width), BF16),
        compiler_params=_cparams("parallel"),
        name="q_proj",
    )(qn, w_ext, cos_q, sin_q)


def _kv_proj_kernel(ckv_ref, kpe_ref, w_ref, k_ref, v_ref, *, n_heads):
    proj = jnp.dot(ckv_ref[...].astype(BF16), w_ref[...], preferred_element_type=F32)
    half = n_heads * HEAD_SLOT
    kpe = kpe_ref[...]
    for h in range(n_heads):
        k_ref[:, h * HEAD_SLOT:(h + 1) * HEAD_SLOT] = (
            proj[:, h * HEAD_SLOT:(h + 1) * HEAD_SLOT] + kpe).astype(BF16)
    v_ref[...] = proj[:, half:].astype(BF16)


def _kv_proj(ckv, kpe_slot, w_ext, *, tm):
    n, kvr = ckv.shape
    width = N_HEADS * HEAD_SLOT
    return pl.pallas_call(
        functools.partial(_kv_proj_kernel, n_heads=N_HEADS),
        grid=(n // tm,),
        in_specs=[
            pl.BlockSpec((tm, kvr), lambda i: (i, 0)),
            pl.BlockSpec((tm, HEAD_SLOT), lambda i: (i, 0)),
            pl.BlockSpec(w_ext.shape, lambda i: (0, 0)),
        ],
        out_specs=[pl.BlockSpec((tm, width), lambda i: (i, 0)),
                   pl.BlockSpec((tm, width), lambda i: (i, 0))],
        out_shape=[jax.ShapeDtypeStruct((n, width), BF16),
                   jax.ShapeDtypeStruct((n, width), BF16)],
        compiler_params=_cparams("parallel"),
        name="kv_proj",
    )(ckv, kpe_slot, w_ext)


def _attn_prompt_kernel(q_ref, k_ref, v_ref, o_ref, *, tq, n_heads):
    qi = pl.program_id(1)
    row = lax.broadcasted_iota(jnp.int32, (tq, tq), 0)
    col = lax.broadcasted_iota(jnp.int32, (tq, tq), 1)
    causal = col <= row

    for h in range(n_heads):
        lanes = slice(h * HEAD_SLOT, (h + 1) * HEAD_SLOT)
        q = q_ref[:, lanes]

        def step(kt, carry, masked):
            m, l, acc = carry
            start = pl.multiple_of(kt * tq, tq)
            k = k_ref[pl.ds(start, tq), lanes]
            v = v_ref[pl.ds(start, tq), lanes]
            s = lax.dot_general(q, k, (((1,), (1,)), ((), ())), preferred_element_type=F32)
            if masked:
                s = jnp.where(causal, s, NEG_BIG)
            m_new = jnp.maximum(m, jnp.max(s, axis=-1, keepdims=True))
            alpha = jnp.exp(m - m_new)
            p = jnp.exp(s - m_new)
            l = alpha * l + jnp.sum(p, axis=-1, keepdims=True)
            acc = alpha * acc + jnp.dot(p.astype(BF16), v, preferred_element_type=F32)
            return m_new, l, acc

        init = (jnp.full((tq, 1), NEG_BIG, F32), jnp.zeros((tq, 1), F32),
                jnp.zeros((tq, HEAD_SLOT), F32))
        carry = lax.fori_loop(0, qi, functools.partial(step, masked=False), init)
        m, l, acc = step(qi, carry, True)
        o_ref[:, lanes] = (acc / l).astype(BF16)


def _attn_prompt(q, k, v, *, bsz, t_len, tq):
    width = q.shape[1]
    nq = t_len // tq
    return pl.pallas_call(
        functools.partial(_attn_prompt_kernel, tq=tq, n_heads=N_HEADS),
        grid=(bsz, nq),
        in_specs=[
            pl.BlockSpec((tq, width), lambda b, i: (b * nq + i, 0)),
            pl.BlockSpec((t_len, width), lambda b, i: (b, 0)),
            pl.BlockSpec((t_len, width), lambda b, i: (b, 0)),
        ],
        out_specs=pl.BlockSpec((tq, width), lambda b, i: (b * nq + i, 0)),
        out_shape=jax.ShapeDtypeStruct(q.shape, BF16),
        compiler_params=_cparams("parallel", "arbitrary"),
        name="attn_prompt",
    )(q, k, v)


def _qlat_kernel(q_ref, w_ref, o_ref, *, n_heads):
    for h in range(n_heads):
        o_ref[h] = jnp.dot(q_ref[:, h * HEAD_SLOT:(h + 1) * HEAD_SLOT], w_ref[h],
                           preferred_element_type=F32).astype(BF16)


def _qlat(q_slots, w_q2lat):
    n = q_slots.shape[0]
    n_heads, _, kvr = w_q2lat.shape
    return pl.pallas_call(
        functools.partial(_qlat_kernel, n_heads=n_heads),
        out_shape=jax.ShapeDtypeStruct((n_heads, n, kvr), BF16),
        compiler_params=pltpu.CompilerParams(vmem_limit_bytes=VMEM_LIMIT),
        name="q_latent",
    )(q_slots, w_q2lat)


def _attn_sample_kernel(pt_ref, ql_ref, qp_ref, cn_ref, kn_ref, *rest, pages_per_step, pages_per_chunk,
                        page, t_new, t_pad):
    g = pages_per_step
    ckv_pages = rest[:g]
    kpe_pages = rest[g:2 * g]
    o_ref, kbf_ref, pbf_ref, m_ref, l_ref, acc_ref = rest[2 * g:]
    s_idx = pl.program_id(1)
    ql = ql_ref[0]
    qp = qp_ref[0]
    rows = ql.shape[0]
    contract_last = (((1,), (1,)), ((), ()))

    @pl.when(s_idx == 0)
    def _():
        cn = cn_ref[0].astype(BF16)
        kn = kn_ref[0].astype(BF16)
        s = (lax.dot_general(ql, cn, contract_last, preferred_element_type=F32)
             + lax.dot_general(qp, kn, contract_last, preferred_element_type=F32))
        t_of_row = lax.broadcasted_iota(jnp.int32, (rows, t_pad), 0) % t_pad
        key = lax.broadcasted_iota(jnp.int32, (rows, t_pad), 1)
        ok = (key <= t_of_row) & (key < t_new)
        s = jnp.where(ok, s, NEG_BIG)
        m = jnp.max(s, axis=-1, keepdims=True)
        p = jnp.where(ok, jnp.exp(s - m), 0.0)
        m_ref[...] = m
        l_ref[...] = jnp.sum(p, axis=-1, keepdims=True)
        acc_ref[...] = jnp.dot(p.astype(BF16), cn, preferred_element_type=F32)

    for j in range(g):
        kbf_ref[j * page:(j + 1) * page, :] = ckv_pages[j][...].astype(BF16)
        pbf_ref[:, j * page:(j + 1) * page] = kpe_pages[j][...].astype(BF16)

    ck = pages_per_chunk * page
    parts = []
    for c in range(g // pages_per_chunk):
        kb = kbf_ref[c * ck:(c + 1) * ck, :]
        s = (lax.dot_general(ql, kb, contract_last, preferred_element_type=F32)
             + jnp.dot(qp, pbf_ref[:, c * ck:(c + 1) * ck], preferred_element_type=F32))
        mc = jnp.max(s, axis=-1, keepdims=True)
        p = jnp.exp(s - mc)
        parts.append((mc, jnp.sum(p, axis=-1, keepdims=True),
                      jnp.dot(p.astype(BF16), kb, preferred_element_type=F32)))
    m_old = m_ref[...]
    m_new = m_old
    for mc, _, _ in parts:
        m_new = jnp.maximum(m_new, mc)
    alpha = jnp.exp(m_old - m_new)
    l = alpha * l_ref[...]
    acc = alpha * acc_ref[...]
    for mc, lc, ac in parts:
        w = jnp.exp(mc - m_new)
        l = l + w * lc
        acc = acc + w * ac
    m_ref[...] = m_new
    l_ref[...] = l
    acc_ref[...] = acc

    @pl.when(s_idx == pl.num_programs(1) - 1)
    def _():
        o_ref[0] = acc_ref[...] / l_ref[...]


def _attn_sample(page_table, q_lat, q_pe, ckv_new, kpe_new, cache_ckv, cache_kpe_t, layer, *,
                 t_new, t_pad, pages_per_step, pages_per_chunk):
    bsz, rows, kvr = q_lat.shape
    rope = q_pe.shape[2]
    n_pages = page_table.shape[1]
    page = cache_ckv.shape[2]
    g = pages_per_step
    n_steps = n_pages // g

    def page_spec(shape, j):
        return pl.BlockSpec((None, None) + shape, lambda b, s, pt: (layer, pt[b, s * g + j], 0, 0))

    in_specs = [
        pl.BlockSpec((1, rows, kvr), lambda b, s, pt: (b, 0, 0)),
        pl.BlockSpec((1, rows, rope), lambda b, s, pt: (b, 0, 0)),
        pl.BlockSpec((1, t_pad, kvr), lambda b, s, pt: (b, 0, 0)),
        pl.BlockSpec((1, t_pad, rope), lambda b, s, pt: (b, 0, 0)),
    ]
    in_specs += [page_spec((page, kvr), j) for j in range(g)]
    in_specs += [page_spec((rope, page), j) for j in range(g)]
    return pl.pallas_call(
        functools.partial(_attn_sample_kernel, pages_per_step=g, pages_per_chunk=pages_per_chunk,
                          page=page, t_new=t_new, t_pad=t_pad),
        grid_spec=pltpu.PrefetchScalarGridSpec(
            num_scalar_prefetch=1,
            grid=(bsz, n_steps),
            in_specs=in_specs,
            out_specs=pl.BlockSpec((1, rows, kvr), lambda b, s, pt: (b, 0, 0)),
            scratch_shapes=[
                pltpu.VMEM((g * page, kvr), BF16),
                pltpu.VMEM((rope, g * page), BF16),
                pltpu.VMEM((rows, 1), F32),
                pltpu.VMEM((rows, 1), F32),
                pltpu.VMEM((rows, kvr), F32),
            ],
        ),
        out_shape=jax.ShapeDtypeStruct((bsz, rows, kvr), F32),
        compiler_params=_cparams("parallel", "arbitrary"),
        name="attn_sample",
    )(page_table, q_lat, q_pe, ckv_new, kpe_new, *([cache_ckv] * g), *([cache_kpe_t] * g))


def _olat_kernel(o_ref, w_ref, out_ref, *, n_heads):
    for h in range(n_heads):
        out_ref[:, h * HEAD_SLOT:(h + 1) * HEAD_SLOT] = jnp.dot(
            o_ref[h].astype(BF16), w_ref[h], preferred_element_type=F32).astype(BF16)


def _olat(o_lat_heads, w_uv_pad):
    n_heads, n, _ = o_lat_heads.shape
    return pl.pallas_call(
        functools.partial(_olat_kernel, n_heads=n_heads),
        out_shape=jax.ShapeDtypeStruct((n, n_heads * HEAD_SLOT), BF16),
        compiler_params=pltpu.CompilerParams(vmem_limit_bytes=VMEM_LIMIT),
        name="o_latent",
    )(o_lat_heads, w_uv_pad)


def _out_proj_kernel(x_ref, c_ref, o_ref, wc_ref, wo_ref, g_ref, x1_ref, h_ref):
    mix = (jnp.dot(c_ref[...], wc_ref[...], preferred_element_type=F32)
           + jnp.dot(o_ref[...], wo_ref[...], preferred_element_type=F32))
    x1 = x_ref[...] + mix
    x1_ref[...] = x1
    h_ref[...] = _rms(x1, g_ref[...]).astype(BF16)


def _out_proj(x, c, o_slots, w_c, w_o, g, *, tm):
    n, d = x.shape
    row = lambda i: (i, 0)
    full = lambda i: (0, 0)
    return pl.pallas_call(
        _out_proj_kernel,
        grid=(n // tm,),
        in_specs=[
            pl.BlockSpec((tm, d), row),
            pl.BlockSpec((tm, c.shape[1]), row),
            pl.BlockSpec((tm, o_slots.shape[1]), row),
            pl.BlockSpec(w_c.shape, full),
            pl.BlockSpec(w_o.shape, full),
            pl.BlockSpec((1, d), full),
        ],
        out_specs=[pl.BlockSpec((tm, d), row), pl.BlockSpec((tm, d), row)],
        out_shape=[jax.ShapeDtypeStruct((n, d), F32), jax.ShapeDtypeStruct((n, d), BF16)],
        compiler_params=_cparams("parallel"),
        name="out_proj",
    )(x, c, o_slots, w_c, w_o, g)


def _swiglu_kernel(te_ref, nv_ref, *refs, residual):
    if residual:
        x_ref, wg_ref, wu_ref, wd_ref, res_ref, o_ref, acc_ref = refs
    else:
        x_ref, wg_ref, wu_ref, wd_ref, o_ref, acc_ref, xb_ref = refs
    m = pl.program_id(0)
    f = pl.program_id(1)
    n_f = pl.num_programs(1)
    valid = m < nv_ref[0]

    @pl.when(valid)
    def _():
        if residual:
            x = x_ref[...]
        else:
            @pl.when(f == 0)
            def _():
                xb_ref[...] = x_ref[...].astype(BF16)
            x = xb_ref[...]
        a = jnp.dot(x, wg_ref[...].astype(BF16), preferred_element_type=F32)
        b = jnp.dot(x, wu_ref[...].astype(BF16), preferred_element_type=F32)
        mid = (a * jax.nn.sigmoid(a) * b).astype(BF16)
        part = jnp.dot(mid, wd_ref[...].astype(BF16), preferred_element_type=F32)

        @pl.when(f == 0)
        def _():
            acc_ref[...] = part

        @pl.when(f > 0)
        def _():
            acc_ref[...] += part

        @pl.when(f == n_f - 1)
        def _():
            if residual:
                o_ref[...] = res_ref[...] + acc_ref[...]
            else:
                o_ref[...] = acc_ref[...]

    @pl.when(jnp.logical_not(valid) & (f == n_f - 1))
    def _():
        o_ref[...] = jnp.zeros_like(o_ref)


def _swiglu(tile_expert, n_valid, x, wg, wu, wd, res=None, *, tm, tf):
    residual = res is not None
    p, d = x.shape
    d_ff = wg.shape[2]
    n_f = d_ff // tf
    n_tiles = p // tm

    def mm(m, nv):
        return jnp.minimum(m, nv[0] - 1)

    def ff(m, f, nv):
        return jnp.where(m < nv[0], f, n_f - 1)

    in_specs = [
        pl.BlockSpec((tm, d), lambda m, f, te, nv: (mm(m, nv), 0)),
        pl.BlockSpec((None, d, tf), lambda m, f, te, nv: (te[mm(m, nv)], 0, ff(m, f, nv))),
        pl.BlockSpec((None, d, tf), lambda m, f, te, nv: (te[mm(m, nv)], 0, ff(m, f, nv))),
        pl.BlockSpec((None, tf, d), lambda m, f, te, nv: (te[mm(m, nv)], ff(m, f, nv), 0)),
    ]
    args = [x, wg, wu, wd]
    scratch = [pltpu.VMEM((tm, d), F32)]
    if residual:
        in_specs.append(pl.BlockSpec((tm, d), lambda m, f, te, nv: (mm(m, nv), 0)))
        args.append(res)
    else:
        scratch.append(pltpu.VMEM((tm, d), BF16))
    return pl.pallas_call(
        functools.partial(_swiglu_kernel, residual=residual),
        grid_spec=pltpu.PrefetchScalarGridSpec(
            num_scalar_prefetch=2,
            grid=(n_tiles, n_f),
            in_specs=in_specs,
            out_specs=pl.BlockSpec((tm, d), lambda m, f, te, nv: (m, 0)),
            scratch_shapes=scratch,
        ),
        out_shape=jax.ShapeDtypeStruct((p, d), F32),
        compiler_params=_cparams("arbitrary", "arbitrary"),
        name="swiglu_residual" if residual else "swiglu_experts",
    )(tile_expert, n_valid, *args)


def _row_copy(src, src_row, dst, dst_row, sem):
    return pltpu.make_async_copy(src.at[pl.ds(src_row, 1)], dst.at[pl.ds(dst_row, 1)], sem)


def _dispatch_kernel(pos_ref, h_ref, xs_in_ref, xs_ref, sem, *, tt):
    del xs_in_ref

    def body(r, carry):
        for k in range(TOP_K):
            _row_copy(h_ref, r, xs_ref, pos_ref[0, TOP_K * r + k], sem).start()
        return carry

    lax.fori_loop(0, tt, body, 0, unroll=8)
    for k in range(TOP_K):
        pltpu.make_async_copy(h_ref, xs_ref.at[pl.ds(0, tt)], sem).wait()


def _dispatch(pos_tiles, h, xs):
    n, d = h.shape
    n_tiles = pos_tiles.shape[0]
    tt = n // n_tiles
    return pl.pallas_call(
        functools.partial(_dispatch_kernel, tt=tt),
        grid=(n_tiles,),
        in_specs=[
            pl.BlockSpec((None, 1, TOP_K * tt), lambda i: (i, 0, 0), memory_space=pltpu.SMEM),
            pl.BlockSpec((tt, d), lambda i: (i, 0)),
            pl.BlockSpec(memory_space=pl.ANY),
        ],
        out_specs=pl.BlockSpec(memory_space=pl.ANY),
        out_shape=jax.ShapeDtypeStruct(xs.shape, xs.dtype),
        scratch_shapes=[pltpu.SemaphoreType.DMA],
        input_output_aliases={2: 0},
        compiler_params=_cparams("arbitrary"),
        name="moe_dispatch",
    )(pos_tiles, h, xs)


def _pool_kernel(*refs, tm, pos0, windows, has_halo, n_experts):
    if has_halo:
        (prev_ref, halo_ref, x_ref, g1_ref, wp_ref, sc_ref, g2_ref, wr_ref,
         x3_ref, h_ref, hn_ref, gate_ref, idx_ref, ext_ref) = refs
    else:
        (prev_ref, x_ref, g1_ref, wp_ref, sc_ref, g2_ref, wr_ref,
         x3_ref, h_ref, hn_ref, gate_ref, idx_ref, ext_ref) = refs
    t = pl.program_id(1)
    x = x_ref[0]
    hn = _rms(x, g1_ref[...])
    hn_ref[0] = hn
    if has_halo:
        @pl.when(t == 0)
        def _():
            ext_ref[0:POOL_HALO, :] = prev_ref[0]

        @pl.when(t > 0)
        def _():
            ext_ref[0:POOL_HALO, :] = _rms(halo_ref[0], g1_ref[...])
    else:
        ext_ref[0:POOL_HALO, :] = prev_ref[0]
    ext_ref[POOL_HALO:POOL_HALO + tm, :] = hn

    d = x.shape[1]
    gc = d // len(windows)
    pos = pos0 + t * tm + lax.broadcasted_iota(jnp.int32, (tm, 1), 0)
    for gi, w in enumerate(windows):
        lanes = slice(gi * gc, (gi + 1) * gc)
        s = ext_ref[POOL_HALO:POOL_HALO + tm, lanes]
        for j in range(1, w):
            s = s + ext_ref[POOL_HALO - j:POOL_HALO - j + tm, lanes]
        cnt = jnp.minimum(pos + 1, w).astype(F32)
        diff = s / cnt - hn[:, lanes]
        y = jnp.dot(diff.astype(BF16), wp_ref[gi], preferred_element_type=F32)
        x3_ref[0, :, lanes] = x[:, lanes] + y * sc_ref[:, lanes]

    x3 = x3_ref[0]
    h2 = _rms(x3, g2_ref[...])
    h_ref[0] = h2
    logits = jnp.dot(h2, wr_ref[...], preferred_element_type=F32, precision=lax.Precision.HIGHEST)
    lane = lax.broadcasted_iota(jnp.int32, logits.shape, 1)
    logits = jnp.where(lane < n_experts, logits, -jnp.inf)
    m1 = jnp.max(logits, axis=-1, keepdims=True)
    i1 = jnp.min(jnp.where(logits == m1, lane, LANES), axis=-1, keepdims=True)
    rest = jnp.where(lane == i1, -jnp.inf, logits)
    m2 = jnp.max(rest, axis=-1, keepdims=True)
    i2 = jnp.min(jnp.where(rest == m2, lane, LANES), axis=-1, keepdims=True)
    e = jnp.exp(m2 - m1)
    g1 = 1.0 / (1.0 + e)
    g2 = e / (1.0 + e)
    gate_ref[0] = jnp.where(lane == 0, g1, jnp.where(lane == 1, g2, 0.0))
    idx_ref[0] = jnp.where(lane == 0, i1, jnp.where(lane == 1, i2, 0))


def _pool_layer(prev_pad, x, g1, w_pool, scale, g2, w_router_pad, *, tm, pos0, n_experts):
    bsz, t_len, d = x.shape
    n_t = t_len // tm
    has_halo = n_t > 1
    hb = tm // POOL_HALO if has_halo else 1
    cur = lambda b, t: (b, t, 0)
    full2 = lambda b, t: (0, 0)
    in_specs = [pl.BlockSpec((1, POOL_HALO, d), lambda b, t: (b, 0, 0))]
    args = [prev_pad]
    if has_halo:
        in_specs.append(pl.BlockSpec((1, POOL_HALO, d), lambda b, t: (b, jnp.maximum(t * hb - 1, 0), 0)))
        args.append(x)
    in_specs += [
        pl.BlockSpec((1, tm, d), cur),
        pl.BlockSpec((1, d), full2),
        pl.BlockSpec(w_pool.shape, lambda b, t: (0, 0, 0)),
        pl.BlockSpec((1, d), full2),
        pl.BlockSpec((1, d), full2),
        pl.BlockSpec(w_router_pad.shape, full2),
    ]
    args += [x, g1, w_pool, scale, g2, w_router_pad]
    return pl.pallas_call(
        functools.partial(_pool_kernel, tm=tm, pos0=pos0, windows=POOL_WINDOWS, has_halo=has_halo,
                          n_experts=n_experts),
        grid=(bsz, n_t),
        in_specs=in_specs,
        out_specs=[
            pl.BlockSpec((1, tm, d), cur),
            pl.BlockSpec((1, tm, d), cur),
            pl.BlockSpec((1, tm, d), cur),
            pl.BlockSpec((1, tm, LANES), cur),
            pl.BlockSpec((1, tm, LANES), cur),
        ],
        out_shape=[
            jax.ShapeDtypeStruct((bsz, t_len, d), F32),
            jax.ShapeDtypeStruct((bsz, t_len, d), F32),
            jax.ShapeDtypeStruct((bsz, t_len, d), F32),
            jax.ShapeDtypeStruct((bsz, t_len, LANES), F32),
            jax.ShapeDtypeStruct((bsz, t_len, LANES), jnp.int32),
        ],
        scratch_shapes=[pltpu.VMEM((POOL_HALO + tm, d), F32)],
        compiler_params=_cparams("parallel", "parallel"),
        name="pool_layer",
    )(*args)


def _combine_kernel(pos_cur_ref, pos_nxt_ref, x_ref, gate_ref, g_ref, ys_ref, y_ref, buf_ref, sem,
                    *, tt, apply_norm):
    i = pl.program_id(0)
    n = pl.num_programs(0)
    slot = i % 2

    def fetch(pos_ref, s):
        def body(r, carry):
            for k in range(TOP_K):
                _row_copy(ys_ref, pos_ref[0, TOP_K * r + k], buf_ref.at[s, k], r, sem.at[s]).start()
            return carry
        lax.fori_loop(0, tt, body, 0, unroll=8)

    @pl.when(i == 0)
    def _():
        fetch(pos_cur_ref, 0)

    @pl.when(i + 1 < n)
    def _():
        fetch(pos_nxt_ref, 1 - slot)

    for k in range(TOP_K):
        pltpu.make_async_copy(ys_ref.at[pl.ds(0, tt)], buf_ref.at[slot, k], sem.at[slot]).wait()
    moe = gate_ref[:, 0:1] * buf_ref[slot, 0]
    for k in range(1, TOP_K):
        moe = moe + gate_ref[:, k:k + 1] * buf_ref[slot, k]
    y = x_ref[...] + moe
    y_ref[...] = _rms(y, g_ref[...]) if apply_norm else y


def _combine(pos_tiles, x, gates, g, ys, *, apply_norm):
    n, d = x.shape
    n_tiles = pos_tiles.shape[0]
    tt = n // n_tiles
    pos_spec = lambda fn: pl.BlockSpec((None, 1, TOP_K * tt), fn, memory_space=pltpu.SMEM)
    return pl.pallas_call(
        functools.partial(_combine_kernel, tt=tt, apply_norm=apply_norm),
        grid=(n_tiles,),
        in_specs=[
            pos_spec(lambda i: (i, 0, 0)),
            pos_spec(lambda i: (jnp.minimum(i + 1, n_tiles - 1), 0, 0)),
            pl.BlockSpec((tt, d), lambda i: (i, 0)),
            pl.BlockSpec((tt, LANES), lambda i: (i, 0)),
            pl.BlockSpec((1, d), lambda i: (0, 0)),
            pl.BlockSpec(memory_space=pl.ANY),
        ],
        out_specs=pl.BlockSpec((tt, d), lambda i: (i, 0)),
        out_shape=jax.ShapeDtypeStruct((n, d), F32),
        scratch_shapes=[pltpu.VMEM((2, TOP_K, tt, d), F32), pltpu.SemaphoreType.DMA((2,))],
        compiler_params=_cparams("arbitrary"),
        name="moe_combine",
    )(pos_tiles, pos_tiles, x, gates, g, ys)


def _final_kernel(x_ref, g_ref, y_ref):
    y_ref[...] = _rms(x_ref[...], g_ref[...])


def _final(x, g, *, tm):
    n, d = x.shape
    row = lambda i: (i, 0)
    return pl.pallas_call(
        _final_kernel,
        grid=(n // tm,),
        in_specs=[pl.BlockSpec((tm, d), row), pl.BlockSpec((1, d), lambda i: (0, 0))],
        out_specs=pl.BlockSpec((tm, d), row),
        out_shape=jax.ShapeDtypeStruct((n, d), F32),
        compiler_params=_cparams("parallel"),
        name="final_norm",
    )(x, g)


def _swap_halves(w):
    half = w.shape[-1] // 2
    return jnp.concatenate([w[..., half:], w[..., :half]], axis=-1)


def _slot(parts, total=HEAD_SLOT):
    used = sum(p.shape[-1] for p in parts)
    pad = jnp.zeros(parts[0].shape[:-1] + (total - used,), parts[0].dtype)
    return jnp.concatenate(list(parts) + [pad], axis=-1)


def _prep_even_weights(w_in, w_uq, w_ukv, w_out, *, dc, qr, kvr):
    d = w_in.shape[0]
    body = w_in[:, :2 * dc + qr + kvr]
    w_kpe = w_in[:, 2 * dc + qr + kvr:]
    z_nope = jnp.zeros((d, NOPE_DIM), w_in.dtype)
    w_in_ext = jnp.concatenate(
        [body, _slot([z_nope, w_kpe]), _slot([z_nope, _swap_halves(w_kpe)])], axis=-1).astype(BF16)

    uq = w_uq.reshape(qr, N_HEADS, NOPE_DIM + ROPE_DIM)
    uq_nope, uq_pe = uq[..., :NOPE_DIM], uq[..., NOPE_DIM:]
    qa = _slot([uq_nope, uq_pe]).reshape(qr, N_HEADS * HEAD_SLOT)
    qb = _slot([jnp.zeros_like(uq_nope), _swap_halves(uq_pe)]).reshape(qr, N_HEADS * HEAD_SLOT)
    w_uq_ext = jnp.concatenate([qa, qb], axis=-1).astype(BF16)

    ukv = w_ukv.reshape(kvr, N_HEADS, NOPE_DIM + V_DIM)
    uk, uv = ukv[..., :NOPE_DIM], ukv[..., NOPE_DIM:]
    w_kv_ext = jnp.concatenate([_slot([uk]).reshape(kvr, N_HEADS * HEAD_SLOT),
                                _slot([uv]).reshape(kvr, N_HEADS * HEAD_SLOT)], axis=-1).astype(BF16)
    w_q2lat = jnp.concatenate(
        [jnp.transpose(uk, (1, 2, 0)),
         jnp.zeros((N_HEADS, HEAD_SLOT - NOPE_DIM, kvr), w_ukv.dtype)], axis=1).astype(BF16)
    w_uv_pad = _slot([jnp.transpose(uv, (1, 0, 2))]).astype(BF16)

    w_c = w_out[:dc].astype(BF16)
    wo = w_out[dc:].reshape(N_HEADS, V_DIM, d)
    w_o = jnp.concatenate([wo, jnp.zeros((N_HEADS, HEAD_SLOT - V_DIM, d), w_out.dtype)],
                          axis=1).reshape(N_HEADS * HEAD_SLOT, d).astype(BF16)
    return w_in_ext, w_uq_ext, w_kv_ext, w_q2lat, w_uv_pad, w_c, w_o


def _rope_tables(pos):
    half = ROPE_DIM // 2
    inv_freq = jnp.power(ROPE_THETA, -jnp.arange(half, dtype=F32) / half)
    ang = pos.astype(F32)[:, None] * inv_freq[None, :]
    cos, sin = jnp.cos(ang), jnp.sin(ang)
    n = pos.shape[0]
    z_nope = jnp.zeros((n, NOPE_DIM), F32)
    cos_k = _slot([z_nope, cos, cos])
    sin_k = _slot([z_nope, -sin, sin])
    cos_q = _slot([jnp.ones((n, NOPE_DIM), F32), cos, cos]) * ATTN_SCALE
    sin_q = sin_k * ATTN_SCALE
    return cos_k, sin_k, cos_q, sin_q


def _even_layer(x, conv_prev, pos, attend, P, layer, i):
    bsz, t_len, d = x.shape
    n = bsz * t_len
    dc = P['conv_w'].shape[2]
    qr = P['q_norm'].shape[1]
    kvr = P['kv_norm'].shape[1]
    k_taps = P['conv_w'].shape[1]
    tm = _row_tile(n, 512) if t_len >= 512 else n
    w_in_ext, w_uq_ext, w_kv_ext, w_q2lat, w_uv_pad, w_c, w_o = _prep_even_weights(
        P['w_in'][i], P['w_uq'][i], P['w_ukv'][i], P['w_out'][i], dc=dc, qr=qr, kvr=kvr)
    cos_k, sin_k, cos_q, sin_q = _rope_tables(pos)
    if t_len < tm:
        reps = tm // t_len
        cos_k, sin_k, cos_q, sin_q = (jnp.tile(a, (reps, 1)) for a in (cos_k, sin_k, cos_q, sin_q))

    xf = x.reshape(n, d)
    u, qn, ckv, kpe_slot = _in_proj(
        xf, P['norm_mix'][layer][None], w_in_ext, P['q_norm'][i][None], P['kv_norm'][i][None],
        cos_k, sin_k, dc=dc, qr=qr, kvr=kvr, tm=tm)

    u3 = u.reshape(bsz, t_len, dc)
    prev_pad = jnp.pad(conv_prev.astype(F32), ((0, 0), (CONV_HALO - (k_taps - 1), 0), (0, 0)))
    c = _conv_branch(prev_pad, u3, P['conv_w'][i], P['conv_b'][i][None], P['conv_ln_g'][i][None],
                     P['conv_ln_b'][i][None], tc=min(t_len, 512))
    conv_state = jnp.concatenate([conv_prev.astype(F32), u3], axis=1)[:, -(k_taps - 1):]

    q_slots = _q_proj(qn, w_uq_ext, cos_q, sin_q, tm=tm)
    o_slots = attend(q_slots, ckv, kpe_slot, w_kv_ext, w_q2lat, w_uv_pad, tm)

    x1, h_ffn = _out_proj(xf, c.reshape(n, dc), o_slots, w_c, w_o, P['norm_ffn'][layer][None], tm=tm)

    tm_f = _row_tile(n, 1024)
    n_tiles = n // tm_f
    x2 = _swiglu(jnp.full((n_tiles,), i, jnp.int32), jnp.full((1,), n_tiles, jnp.int32), h_ffn,
                 P['ffn_w_gate'], P['ffn_w_up'], P['ffn_w_down'], x1, tm=tm_f, tf=256)
    kpe = kpe_slot[:, NOPE_DIM:NOPE_DIM + ROPE_DIM]
    return (x2.reshape(bsz, t_len, d), conv_state, ckv.reshape(bsz, t_len, kvr),
            kpe.reshape(bsz, t_len, ROPE_DIM))


def _make_attend_prompt(bsz, t_len):
    def attend(q_slots, ckv, kpe_slot, w_kv_ext, w_q2lat, w_uv_pad, tm):
        k_slots, v_slots = _kv_proj(ckv, kpe_slot, w_kv_ext, tm=tm)
        return _attn_prompt(q_slots, k_slots, v_slots, bsz=bsz, t_len=t_len, tq=min(t_len, 512))
    return attend


def _make_attend_sample(bsz, t_len, cache_ckv, cache_kpe_t, page_table, i):
    t_pad = 8

    def attend(q_slots, ckv, kpe_slot, w_kv_ext, w_q2lat, w_uv_pad, tm):
        n = bsz * t_len
        kvr = ckv.shape[1]
        q_lat = _qlat(q_slots, w_q2lat)
        q_lat = q_lat.reshape(N_HEADS, bsz, t_len, kvr)
        q_lat = jnp.pad(q_lat, ((0, 0), (0, 0), (0, t_pad - t_len), (0, 0)))
        q_lat = jnp.transpose(q_lat, (1, 0, 2, 3)).reshape(bsz, N_HEADS * t_pad, kvr)
        q_pe = q_slots.reshape(bsz, t_len, N_HEADS, HEAD_SLOT)[..., NOPE_DIM:NOPE_DIM + ROPE_DIM]
        q_pe = jnp.pad(jnp.transpose(q_pe, (0, 2, 1, 3)), ((0, 0), (0, 0), (0, t_pad - t_len), (0, 0)))
        q_pe = q_pe.reshape(bsz, N_HEADS * t_pad, ROPE_DIM)
        pad_t = ((0, 0), (0, t_pad - t_len), (0, 0))
        ckv_new = jnp.pad(ckv.reshape(bsz, t_len, kvr), pad_t)
        kpe_new = jnp.pad(kpe_slot[:, NOPE_DIM:NOPE_DIM + ROPE_DIM].reshape(bsz, t_len, ROPE_DIM), pad_t)
        o_lat = _attn_sample(page_table, q_lat, q_pe, ckv_new, kpe_new, cache_ckv, cache_kpe_t, i,
                             t_new=t_len, t_pad=t_pad, pages_per_step=32,
                             pages_per_chunk=8)
        o_lat = o_lat.reshape(bsz, N_HEADS, t_pad, kvr)[:, :, :t_len]
        o_lat = jnp.transpose(o_lat, (1, 0, 2, 3)).reshape(N_HEADS, n, kvr)
        return _olat(o_lat, w_uv_pad)
    return attend


def _odd_layer_mixer(x, pool_prev, pos0, P, layer, i):
    bsz, t_len, d = x.shape
    n_experts = P['router_w'].shape[2]
    ctx = max(POOL_WINDOWS) - 1
    prev_pad = jnp.pad(pool_prev.astype(F32), ((0, 0), (POOL_HALO - ctx, 0), (0, 0)))
    w_router_pad = jnp.pad(P['router_w'][i], ((0, 0), (0, LANES - n_experts)))
    x3, h_moe, hn, gates, ids = _pool_layer(
        prev_pad, x, P['norm_mix'][layer][None], P['pool_w'][i].astype(BF16), P['pool_scale'][i][None],
        P['norm_ffn'][layer][None], w_router_pad, tm=min(t_len, 512), pos0=pos0, n_experts=n_experts)
    pool_state = jnp.concatenate([pool_prev.astype(F32), hn], axis=1)[:, -ctx:]
    n = bsz * t_len
    return (x3.reshape(n, d), h_moe.reshape(n, d), pool_state, gates.reshape(n, LANES),
            ids.reshape(n, LANES)[:, :TOP_K])


def _moe_plan(ids, n_experts, tm):
    n = ids.shape[0]
    flat = ids.reshape(-1)
    experts = jnp.arange(n_experts, dtype=jnp.int32)
    onehot = (flat[:, None] == experts[None, :]).astype(jnp.int32)
    csum = jnp.cumsum(onehot, axis=0)
    counts = csum[-1]
    rank = jnp.sum(csum * onehot, axis=1) - 1
    padded = ((counts + tm - 1) // tm) * tm
    ends = jnp.cumsum(padded)
    starts = ends - padded
    pos = jnp.sum(starts[None, :] * onehot, axis=1) + rank
    n_tiles = (n * TOP_K + tm - 1) // tm + n_experts
    n_valid = (ends[-1] // tm).astype(jnp.int32)
    tile_start = jnp.arange(n_tiles, dtype=jnp.int32) * tm
    tile_expert = jnp.sum((ends[None, :] <= tile_start[:, None]).astype(jnp.int32), axis=1)
    tile_expert = jnp.minimum(tile_expert, n_experts - 1)
    return pos.reshape(n, TOP_K), tile_expert, n_valid.reshape(1), n_tiles


def _moe_layer(groups, P, i, g_final, *, tm, tt):
    d = groups[0][0].shape[1]
    n_experts = P['router_w'].shape[2]
    ids = jnp.concatenate([g[3] for g in groups], axis=0)
    pos, tile_expert, n_valid, n_tiles = _moe_plan(ids, n_experts, tm)
    xs = jnp.zeros((n_tiles * tm, d), F32)
    pos_tiles = []
    lo = 0
    for x3, h, _, _ in groups:
        n = x3.shape[0]
        t = min(tt, n)
        pt = pos[lo:lo + n].reshape(n // t, 1, TOP_K * t)
        xs = _dispatch(pt, h, xs)
        pos_tiles.append(pt)
        lo += n
    ys = _swiglu(tile_expert, n_valid, xs, P['moe_w_gate'][i], P['moe_w_up'][i], P['moe_w_down'][i],
                 tm=tm, tf=512)
    gain = g_final if g_final is not None else jnp.ones((1, d), F32)
    return [_combine(pt, x3, gates, gain, ys, apply_norm=g_final is not None)
            for pt, (x3, _, gates, _) in zip(pos_tiles, groups)]


def kernel(x_prompt, x_sample, cache_ckv, cache_kpe, page_table, state_conv, state_pool, norm_mix, norm_ffn,
           norm_final, w_in, conv_w, conv_b, conv_ln_g, conv_ln_b, q_norm, w_uq, kv_norm, w_ukv, w_out,
           ffn_w_gate, ffn_w_up, ffn_w_down, pool_w, pool_scale, router_w, moe_w_gate, moe_w_up, moe_w_down):
    P = {'norm_mix': norm_mix, 'norm_ffn': norm_ffn, 'norm_final': norm_final, 'w_in': w_in,
         'conv_w': conv_w, 'conv_b': conv_b, 'conv_ln_g': conv_ln_g, 'conv_ln_b': conv_ln_b,
         'q_norm': q_norm, 'w_uq': w_uq, 'kv_norm': kv_norm, 'w_ukv': w_ukv, 'w_out': w_out,
         'ffn_w_gate': ffn_w_gate, 'ffn_w_up': ffn_w_up, 'ffn_w_down': ffn_w_down,
         'pool_w': pool_w, 'pool_scale': pool_scale, 'router_w': router_w,
         'moe_w_gate': moe_w_gate, 'moe_w_up': moe_w_up, 'moe_w_down': moe_w_down}
    depth = norm_mix.shape[0]
    d = x_prompt.shape[2]
    bp, tp, _ = x_prompt.shape
    bs, ts, _ = x_sample.shape
    past = page_table.shape[1] * cache_ckv.shape[2]
    k_taps = conv_w.shape[1]
    ctx = max(POOL_WINDOWS) - 1
    pos_p = jnp.arange(tp, dtype=jnp.int32)
    pos_s = past + jnp.arange(ts, dtype=jnp.int32)
    cache_kpe_t = jnp.swapaxes(cache_kpe, 2, 3)

    xp, xs = x_prompt, x_sample
    outs_p = {'ckv': [], 'kpe': [], 'conv': [], 'pool': []}
    outs_s = {'ckv': [], 'kpe': [], 'conv': [], 'pool': []}
    for layer in range(depth):
        i = layer // 2
        last = layer == depth - 1
        if layer % 2 == 0:
            conv0 = jnp.zeros((bp, k_taps - 1, conv_w.shape[2]), F32)
            xp, cst, ckv, kpe = _even_layer(xp, conv0, pos_p, _make_attend_prompt(bp, tp), P, layer, i)
            outs_p['conv'].append(cst); outs_p['ckv'].append(ckv); outs_p['kpe'].append(kpe)
            xs, cst, ckv, kpe = _even_layer(
                xs, state_conv[i], pos_s, _make_attend_sample(bs, ts, cache_ckv, cache_kpe_t, page_table, i),
                P, layer, i)
            outs_s['conv'].append(cst); outs_s['ckv'].append(ckv); outs_s['kpe'].append(kpe)
            if last:
                xp = _final(xp.reshape(bp * tp, d), norm_final[None], tm=_row_tile(bp * tp, 512)).reshape(xp.shape)
                xs = _final(xs.reshape(bs * ts, d), norm_final[None], tm=_row_tile(bs * ts, 512)).reshape(xs.shape)
        else:
            pool0 = jnp.zeros((bp, ctx, d), F32)
            x3p, hp, pst_p, gp, ip = _odd_layer_mixer(xp, pool0, 0, P, layer, i)
            x3s, hs, pst_s, gs, is_ = _odd_layer_mixer(xs, state_pool[i], past, P, layer, i)
            outs_p['pool'].append(pst_p); outs_s['pool'].append(pst_s)
            yp, ysm = _moe_layer([(x3p, hp, gp, ip), (x3s, hs, gs, is_)], P, i,
                                 norm_final[None] if last else None, tm=1024, tt=512)
            xp, xs = yp.reshape(xp.shape), ysm.reshape(xs.shape)

    def stack(lst, shape_if_empty):
        return jnp.stack(lst) if lst else jnp.zeros(shape_if_empty, F32)

    return (xp, xs,
            jnp.stack(outs_p['ckv']), jnp.stack(outs_p['kpe']), jnp.stack(outs_p['conv']),
            stack(outs_p['pool'], (0, bp, ctx, d)),
            jnp.stack(outs_s['ckv']), jnp.stack(outs_s['kpe']), jnp.stack(outs_s['conv']),
            stack(outs_s['pool'], (0, bs, ctx, d)))
```

```python
import functools

import jax
import jax.numpy as jnp
from jax import lax
from jax.experimental import pallas as pl
from jax.experimental.pallas import tpu as pltpu

F32 = jnp.float32
BF16 = jnp.bfloat16

N_HEADS = 8
NOPE_DIM = 64
ROPE_DIM = 32
V_DIM = 64
ROPE_THETA = 10000.0
ATTN_SCALE = (NOPE_DIM + ROPE_DIM) ** -0.5
LOG2E = 1.4426950408889634
POOL_WINDOWS = (2, 4, 8, 16)
TOP_K = 2
EPS = 1e-6

HEAD_SLOT = 128
LANES = 128
SUBLANES = 8
NEG_BIG = -1e30
VMEM_LIMIT = 56 * 1024 * 1024

CONV_HALO = 32
POOL_HALO = 16


def _cparams(*semantics):
    return pltpu.CompilerParams(dimension_semantics=semantics, vmem_limit_bytes=VMEM_LIMIT)


def _rms(x, g):
    return x * lax.rsqrt(jnp.mean(x * x, axis=-1, keepdims=True) + EPS) * g


def _row_tile(n, want):
    if n <= want:
        return n
    t = want
    while t >= 8:
        if n % t == 0 and t % 8 == 0:
            return t
        t -= 8
    return n


def _in_proj_kernel(x_ref, g_ref, w_ref, qg_ref, kvg_ref, cos_ref, sin_ref,
                    u_ref, qn_ref, ckv_ref, kpe_ref, *, dc, qr, kvr):
    h = _rms(x_ref[...], g_ref[...])
    proj = jnp.dot(h.astype(BF16), w_ref[...], preferred_element_type=F32)
    a = proj[:, :dc]
    gate = proj[:, dc:2 * dc]
    u_ref[...] = a * jax.nn.sigmoid(gate)
    o = 2 * dc
    qn_ref[...] = _rms(proj[:, o:o + qr], qg_ref[...]).astype(BF16)
    o += qr
    ckv_ref[...] = _rms(proj[:, o:o + kvr], kvg_ref[...])
    o += kvr
    kpe_ref[...] = (proj[:, o:o + HEAD_SLOT] * cos_ref[...]
                    + proj[:, o + HEAD_SLOT:o + 2 * HEAD_SLOT] * sin_ref[...])


def _in_proj(x, g, w_ext, qg, kvg, cos_k, sin_k, *, dc, qr, kvr, tm):
    n, d = x.shape
    n_tab = cos_k.shape[0] // tm
    row = lambda i: (i, 0)
    full = lambda i: (0, 0)
    tab = lambda i: (i % n_tab, 0)
    return pl.pallas_call(
        functools.partial(_in_proj_kernel, dc=dc, qr=qr, kvr=kvr),
        grid=(n // tm,),
        in_specs=[
            pl.BlockSpec((tm, d), row),
            pl.BlockSpec((1, d), full),
            pl.BlockSpec(w_ext.shape, full),
            pl.BlockSpec((1, qr), full),
            pl.BlockSpec((1, kvr), full),
            pl.BlockSpec((tm, HEAD_SLOT), tab),
            pl.BlockSpec((tm, HEAD_SLOT), tab),
        ],
        out_specs=[
            pl.BlockSpec((tm, dc), row),
            pl.BlockSpec((tm, qr), row),
            pl.BlockSpec((tm, kvr), row),
            pl.BlockSpec((tm, HEAD_SLOT), row),
        ],
        out_shape=[
            jax.ShapeDtypeStruct((n, dc), F32),
            jax.ShapeDtypeStruct((n, qr), BF16),
            jax.ShapeDtypeStruct((n, kvr), F32),
            jax.ShapeDtypeStruct((n, HEAD_SLOT), F32),
        ],
        compiler_params=_cparams("parallel"),
        name="in_proj",
    )(x, g, w_ext, qg, kvg, cos_k, sin_k)


def _conv_kernel(*refs, tc, k_taps, rows_per_chunk, has_halo):
    if has_halo:
        prev_ref, halo_ref, u_ref, w_ref, b_ref, lg_ref, lb_ref, c_ref, ext_ref, sh_ref = refs
    else:
        prev_ref, u_ref, w_ref, b_ref, lg_ref, lb_ref, c_ref, ext_ref, sh_ref = refs
    if has_halo:
        t = pl.program_id(1)

        @pl.when(t == 0)
        def _():
            ext_ref[0:CONV_HALO, :] = prev_ref[0]

        @pl.when(t > 0)
        def _():
            ext_ref[0:CONV_HALO, :] = halo_ref[0]
    else:
        ext_ref[0:CONV_HALO, :] = prev_ref[0]
    ext_ref[CONV_HALO:CONV_HALO + tc, :] = u_ref[0]

    n_rows = CONV_HALO + tc
    for s in range(1, SUBLANES):
        sh_ref[s - 1, 0:n_rows - SUBLANES, :] = ext_ref[s:s + n_rows - SUBLANES, :]

    def window(start, n):
        q, s = divmod(start, SUBLANES)
        if s == 0:
            return ext_ref[start:start + n, :]
        return sh_ref[s - 1, q * SUBLANES:q * SUBLANES + n, :]

    first = CONV_HALO - (k_taps - 1)
    for c0 in range(0, tc, rows_per_chunk):
        rc = min(rows_per_chunk, tc - c0)
        acc = w_ref[0:1, :] * window(first + c0, rc)
        for k in range(1, k_taps):
            acc = acc + w_ref[k:k + 1, :] * window(first + c0 + k, rc)
        acc = acc + b_ref[...]
        mu = jnp.mean(acc, axis=-1, keepdims=True)
        xc = acc - mu
        y = xc * lax.rsqrt(jnp.mean(xc * xc, axis=-1, keepdims=True) + EPS)
        y = y * lg_ref[...] + lb_ref[...]
        c_ref[0, c0:c0 + rc, :] = (y * jax.nn.sigmoid(y)).astype(BF16)


def _conv_branch(prev_pad, u, w, b, lg, lb, *, tc):
    bsz, t_len, c = u.shape
    k_taps = w.shape[0]
    n_t = t_len // tc
    has_halo = n_t > 1
    hb = tc // CONV_HALO if has_halo else 1
    in_specs = [pl.BlockSpec((1, CONV_HALO, c), lambda bi, ti: (bi, 0, 0))]
    args = [prev_pad]
    if has_halo:
        in_specs.append(pl.BlockSpec((1, CONV_HALO, c),
                                     lambda bi, ti: (bi, jnp.maximum(ti * hb - 1, 0), 0)))
        args.append(u)
    in_specs += [
        pl.BlockSpec((1, tc, c), lambda bi, ti: (bi, ti, 0)),
        pl.BlockSpec((k_taps, c), lambda bi, ti: (0, 0)),
        pl.BlockSpec((1, c), lambda bi, ti: (0, 0)),
        pl.BlockSpec((1, c), lambda bi, ti: (0, 0)),
        pl.BlockSpec((1, c), lambda bi, ti: (0, 0)),
    ]
    args += [u, w, b, lg, lb]
    return pl.pallas_call(
        functools.partial(_conv_kernel, tc=tc, k_taps=k_taps, rows_per_chunk=64, has_halo=has_halo),
        grid=(bsz, n_t),
        in_specs=in_specs,
        out_specs=pl.BlockSpec((1, tc, c), lambda bi, ti: (bi, ti, 0)),
        out_shape=jax.ShapeDtypeStruct((bsz, t_len, c), BF16),
        scratch_shapes=[pltpu.VMEM((CONV_HALO + tc, c), F32),
                        pltpu.VMEM((SUBLANES - 1, CONV_HALO + tc, c), F32)],
        compiler_params=_cparams("parallel", "parallel"),
        name="conv_branch",
    )(*args)


def _q_proj_kernel(qn_ref, w_ref, cos_ref, sin_ref, q_ref, *, n_heads):
    proj = jnp.dot(qn_ref[...], w_ref[...], preferred_element_type=F32)
    half = n_heads * HEAD_SLOT
    cos = cos_ref[...]
    sin = sin_ref[...]
    for h in range(n_heads):
        a = proj[:, h * HEAD_SLOT:(h + 1) * HEAD_SLOT]
        b = proj[:, half + h * HEAD_SLOT:half + (h + 1) * HEAD_SLOT]
        q_ref[:, h * HEAD_SLOT:(h + 1) * HEAD_SLOT] = (a * cos + b * sin).astype(BF16)


def _q_proj(qn, w_ext, cos_q, sin_q, *, tm):
    n, qr = qn.shape
    n_tab = cos_q.shape[0] // tm
    width = N_HEADS * HEAD_SLOT
    return pl.pallas_call(
        functools.partial(_q_proj_kernel, n_heads=N_HEADS),
        grid=(n // tm,),
        in_specs=[
            pl.BlockSpec((tm, qr), lambda i: (i, 0)),
            pl.BlockSpec(w_ext.shape, lambda i: (0, 0)),
            pl.BlockSpec((tm, HEAD_SLOT), lambda i: (i % n_tab, 0)),
            pl.BlockSpec((tm, HEAD_SLOT), lambda i: (i % n_tab, 0)),
        ],
        out_specs=pl.BlockSpec((tm, width), lambda i: (i, 0)),
        out_shape=jax.ShapeDtypeStruct((n, width), BF16),
        compiler_params=_cparams("parallel"),
        name="q_proj",
    )(qn, w_ext, cos_q, sin_q)


def _kv_proj_kernel(ckv_ref, kpe_ref, w_ref, k_ref, v_ref, *, n_heads):
    proj = jnp.dot(ckv_ref[...].astype(BF16), w_ref[...], preferred_element_type=F32)
    half = n_heads * HEAD_SLOT
    kpe = kpe_ref[...]
    for h in range(n_heads):
        k_ref[:, h * HEAD_SLOT:(h + 1) * HEAD_SLOT] = (
            proj[:, h * HEAD_SLOT:(h + 1) * HEAD_SLOT] + kpe).astype(BF16)
    v = proj[:, half:]
    lane = lax.broadcasted_iota(jnp.int32, v.shape, 1)
    v_ref[...] = jnp.where(lane % HEAD_SLOT == V_DIM, 1.0, v).astype(BF16)


def _kv_proj(ckv, kpe_slot, w_ext, *, tm):
    n, kvr = ckv.shape
    width = N_HEADS * HEAD_SLOT
    return pl.pallas_call(
        functools.partial(_kv_proj_kernel, n_heads=N_HEADS),
        grid=(n // tm,),
        in_specs=[
            pl.BlockSpec((tm, kvr), lambda i: (i, 0)),
            pl.BlockSpec((tm, HEAD_SLOT), lambda i: (i, 0)),
            pl.BlockSpec(w_ext.shape, lambda i: (0, 0)),
        ],
        out_specs=[pl.BlockSpec((tm, width), lambda i: (i, 0)),
                   pl.BlockSpec((tm, width), lambda i: (i, 0))],
        out_shape=[jax.ShapeDtypeStruct((n, width), BF16),
                   jax.ShapeDtypeStruct((n, width), BF16)],
        compiler_params=_cparams("parallel"),
        name="kv_proj",
    )(ckv, kpe_slot, w_ext)


def _attn_prompt_kernel(q_ref, k_ref, v_ref, o_ref, *, tq, n_heads):
    qi = pl.program_id(1)
    row = lax.broadcasted_iota(jnp.int32, (tq, tq), 0)
    col = lax.broadcasted_iota(jnp.int32, (tq, tq), 1)
    causal = col <= row

    for h in range(n_heads):
        lanes = slice(h * HEAD_SLOT, (h + 1) * HEAD_SLOT)
        q = q_ref[:, lanes]

        def step(kt, carry, masked):
            m, acc = carry
            start = pl.multiple_of(kt * tq, tq)
            k = k_ref[pl.ds(start, tq), lanes]
            v = v_ref[pl.ds(start, tq), lanes]
            s = lax.dot_general(q, k, (((1,), (1,)), ((), ())), preferred_element_type=F32)
            if masked:
                s = jnp.where(causal, s, NEG_BIG)
            m_new = jnp.maximum(m, jnp.max(s, axis=-1, keepdims=True))
            p = jnp.exp2(s - m_new)
            acc = jnp.exp2(m - m_new) * acc + jnp.dot(p.astype(BF16), v, preferred_element_type=F32)
            return m_new, acc

        init = (jnp.full((tq, 1), NEG_BIG, F32), jnp.zeros((tq, HEAD_SLOT), F32))
        carry = lax.fori_loop(0, qi, functools.partial(step, masked=False), init)
        m, acc = step(qi, carry, True)
        o_ref[:, lanes] = (acc / acc[:, V_DIM:V_DIM + 1]).astype(BF16)


def _attn_prompt(q, k, v, *, bsz, t_len, tq):
    width = q.shape[1]
    nq = t_len // tq
    return pl.pallas_call(
        functools.partial(_attn_prompt_kernel, tq=tq, n_heads=N_HEADS),
        grid=(bsz, nq),
        in_specs=[
            pl.BlockSpec((tq, width), lambda b, i: (b * nq + i, 0)),
            pl.BlockSpec((t_len, width), lambda b, i: (b, 0)),
            pl.BlockSpec((t_len, width), lambda b, i: (b, 0)),
        ],
        out_specs=pl.BlockSpec((tq, width), lambda b, i: (b * nq + i, 0)),
        out_shape=jax.ShapeDtypeStruct(q.shape, BF16),
        compiler_params=_cparams("parallel", "arbitrary"),
        name="attn_prompt",
    )(q, k, v)


def _qlat_kernel(q_ref, w_ref, o_ref, *, n_heads):
    for h in range(n_heads):
        o_ref[h] = jnp.dot(q_ref[:, h * HEAD_SLOT:(h + 1) * HEAD_SLOT], w_ref[h],
                           preferred_element_type=F32).astype(BF16)


def _qlat(q_slots, w_q2lat):
    n = q_slots.shape[0]
    n_heads, _, kvr = w_q2lat.shape
    return pl.pallas_call(
        functools.partial(_qlat_kernel, n_heads=n_heads),
        out_shape=jax.ShapeDtypeStruct((n_heads, n, kvr), BF16),
        compiler_params=pltpu.CompilerParams(vmem_limit_bytes=VMEM_LIMIT),
        name="q_latent",
    )(q_slots, w_q2lat)


def _attn_sample_kernel(pt_ref, ql_ref, qp_ref, cn_ref, kn_ref, ckv_hbm, kpe_hbm, o_ref,
                        cbuf_ref, pbuf_ref, kbf_ref, pbf_ref, m_ref, l_ref, acc_ref, sem,
                        *, layer, pages_per_step, pages_per_chunk, page, t_new, t_pad):
    g = pages_per_step
    bsz, n_pages = pt_ref.shape
    steps_per_seq = n_pages // g
    total = bsz * steps_per_seq

    def page_copies(i, slot):
        b = i // steps_per_seq
        first = (i % steps_per_seq) * g
        copies = []
        for j in range(g):
            pid = pt_ref[b, first + j]
            copies.append(pltpu.make_async_copy(
                ckv_hbm.at[layer, pid], cbuf_ref.at[slot, pl.ds(j * page, page)], sem.at[slot]))
            copies.append(pltpu.make_async_copy(
                kpe_hbm.at[layer, pid], pbuf_ref.at[slot, j], sem.at[slot]))
        return copies

    for cp in page_copies(0, 0):
        cp.start()

    def step(i, carry):
        slot = i % 2
        b = i // steps_per_seq
        s_idx = i % steps_per_seq

        @pl.when(i + 1 < total)
        def _():
            for cp in page_copies(i + 1, 1 - slot):
                cp.start()

        for cp in page_copies(i, slot):
            cp.wait()
        _attn_sample_step(b, s_idx, slot, steps_per_seq, ql_ref, qp_ref, cn_ref, kn_ref, o_ref, cbuf_ref,
                          pbuf_ref, kbf_ref, pbf_ref, m_ref, l_ref, acc_ref, g=g,
                          pages_per_chunk=pages_per_chunk, page=page, t_new=t_new, t_pad=t_pad)
        return carry

    lax.fori_loop(0, total, step, 0)


def _attn_sample_step(b, s_idx, slot, steps_per_seq, ql_ref, qp_ref, cn_ref, kn_ref, o_ref, cbuf_ref, pbuf_ref,
                      kbf_ref, pbf_ref, m_ref, l_ref, acc_ref, *, g, pages_per_chunk, page, t_new, t_pad):
    ql = ql_ref[b]
    qp = qp_ref[b]
    rows = ql.shape[0]
    contract_last = (((1,), (1,)), ((), ()))

    @pl.when(s_idx == 0)
    def _():
        cn = cn_ref[b].astype(BF16)
        kn = kn_ref[b].astype(BF16)
        s = (lax.dot_general(ql, cn, contract_last, preferred_element_type=F32)
             + lax.dot_general(qp, kn, contract_last, preferred_element_type=F32))
        t_of_row = lax.broadcasted_iota(jnp.int32, (rows, t_pad), 0) % t_pad
        key = lax.broadcasted_iota(jnp.int32, (rows, t_pad), 1)
        ok = (key <= t_of_row) & (key < t_new)
        s = jnp.where(ok, s, NEG_BIG)
        m = jnp.max(s, axis=-1, keepdims=True)
        p = jnp.where(ok, jnp.exp2(s - m), 0.0)
        m_ref[...] = m
        l_ref[...] = jnp.sum(p, axis=-1, keepdims=True)
        acc_ref[...] = jnp.dot(p.astype(BF16), cn, preferred_element_type=F32)

    kbf_ref[...] = cbuf_ref[slot].astype(BF16)
    for j in range(g):
        pbf_ref[:, j * page:(j + 1) * page] = pbuf_ref[slot, j].astype(BF16)

    ck = pages_per_chunk * page
    parts = []
    for c in range(g // pages_per_chunk):
        kb = kbf_ref[c * ck:(c + 1) * ck, :]
        s = (lax.dot_general(ql, kb, contract_last, preferred_element_type=F32)
             + jnp.dot(qp, pbf_ref[:, c * ck:(c + 1) * ck], preferred_element_type=F32))
        mc = jnp.max(s, axis=-1, keepdims=True)
        p = jnp.exp2(s - mc)
        parts.append((mc, jnp.sum(p, axis=-1, keepdims=True),
                      jnp.dot(p.astype(BF16), kb, preferred_element_type=F32)))
    m_old = m_ref[...]
    m_new = m_old
    for mc, _, _ in parts:
        m_new = jnp.maximum(m_new, mc)
    alpha = jnp.exp2(m_old - m_new)
    l = alpha * l_ref[...]
    acc = alpha * acc_ref[...]
    for mc, lc, ac in parts:
        w = jnp.exp2(mc - m_new)
        l = l + w * lc
        acc = acc + w * ac
    m_ref[...] = m_new
    l_ref[...] = l
    acc_ref[...] = acc

    @pl.when(s_idx == steps_per_seq - 1)
    def _():
        o_ref[b] = acc_ref[...] / l_ref[...]


def _attn_sample(page_table, q_lat, q_pe, ckv_new, kpe_new, cache_ckv, cache_kpe_t, layer, *,
                 t_new, t_pad, pages_per_step, pages_per_chunk):
    bsz, rows, kvr = q_lat.shape
    rope = q_pe.shape[2]
    page = cache_ckv.shape[2]
    g = pages_per_step
    assert page_table.shape[1] % g == 0 and g % pages_per_chunk == 0
    vmem = pl.BlockSpec(memory_space=pltpu.VMEM)
    hbm = pl.BlockSpec(memory_space=pl.ANY)
    return pl.pallas_call(
        functools.partial(_attn_sample_kernel, layer=layer, pages_per_step=g, pages_per_chunk=pages_per_chunk,
                          page=page, t_new=t_new, t_pad=t_pad),
        in_specs=[pl.BlockSpec(memory_space=pltpu.SMEM), vmem, vmem, vmem, vmem, hbm, hbm],
        out_specs=vmem,
        out_shape=jax.ShapeDtypeStruct((bsz, rows, kvr), F32),
        scratch_shapes=[
            pltpu.VMEM((2, g * page, kvr), F32),
            pltpu.VMEM((2, g, rope, page), F32),
            pltpu.VMEM((g * page, kvr), BF16),
            pltpu.VMEM((rope, g * page), BF16),
            pltpu.VMEM((rows, 1), F32),
            pltpu.VMEM((rows, 1), F32),
            pltpu.VMEM((rows, kvr), F32),
            pltpu.SemaphoreType.DMA((2,)),
        ],
        compiler_params=pltpu.CompilerParams(vmem_limit_bytes=VMEM_LIMIT),
        name="attn_sample",
    )(page_table, q_lat, q_pe, ckv_new, kpe_new, cache_ckv, cache_kpe_t)


def _olat_kernel(o_ref, w_ref, out_ref, *, n_heads):
    for h in range(n_heads):
        out_ref[:, h * HEAD_SLOT:(h + 1) * HEAD_SLOT] = jnp.dot(
            o_ref[h].astype(BF16), w_ref[h], preferred_element_type=F32).astype(BF16)


def _olat(o_lat_heads, w_uv_pad):
    n_heads, n, _ = o_lat_heads.shape
    return pl.pallas_call(
        functools.partial(_olat_kernel, n_heads=n_heads),
        out_shape=jax.ShapeDtypeStruct((n, n_heads * HEAD_SLOT), BF16),
        compiler_params=pltpu.CompilerParams(vmem_limit_bytes=VMEM_LIMIT),
        name="o_latent",
    )(o_lat_heads, w_uv_pad)


def _out_proj_kernel(x_ref, c_ref, o_ref, wc_ref, wo_ref, g_ref, x1_ref, h_ref):
    mix = (jnp.dot(c_ref[...], wc_ref[...], preferred_element_type=F32)
           + jnp.dot(o_ref[...], wo_ref[...], preferred_element_type=F32))
    x1 = x_ref[...] + mix
    x1_ref[...] = x1
    h_ref[...] = _rms(x1, g_ref[...]).astype(BF16)


def _out_proj(x, c, o_slots, w_c, w_o, g, *, tm):
    n, d = x.shape
    row = lambda i: (i, 0)
    full = lambda i: (0, 0)
    return pl.pallas_call(
        _out_proj_kernel,
        grid=(n // tm,),
        in_specs=[
            pl.BlockSpec((tm, d), row),
            pl.BlockSpec((tm, c.shape[1]), row),
            pl.BlockSpec((tm, o_slots.shape[1]), row),
            pl.BlockSpec(w_c.shape, full),
            pl.BlockSpec(w_o.shape, full),
            pl.BlockSpec((1, d), full),
        ],
        out_specs=[pl.BlockSpec((tm, d), row), pl.BlockSpec((tm, d), row)],
        out_shape=[jax.ShapeDtypeStruct((n, d), F32), jax.ShapeDtypeStruct((n, d), BF16)],
        compiler_params=_cparams("parallel"),
        name="out_proj",
    )(x, c, o_slots, w_c, w_o, g)


def _swiglu_kernel(te_ref, nv_ref, *refs, residual, row_splits):
    if residual:
        x_ref, wg_ref, wu_ref, wd_ref, res_ref, o_ref = refs
    else:
        x_ref, wg_ref, wu_ref, wd_ref, o_ref, xb_ref = refs
    m = pl.program_id(0)
    f = pl.program_id(1)
    tm = o_ref.shape[0]

    @pl.when(f == 0)
    def _():
        if residual:
            o_ref[...] = res_ref[...]
        else:
            o_ref[...] = jnp.zeros_like(o_ref)

    @pl.when(m < nv_ref[0])
    def _():
        if not residual:
            @pl.when(f == 0)
            def _():
                xb_ref[...] = x_ref[...].astype(BF16)
        src = x_ref if residual else xb_ref
        wg = wg_ref[...].astype(BF16)
        wu = wu_ref[...].astype(BF16)
        wd = wd_ref[...].astype(BF16)
        hm = tm // row_splits
        for r in range(row_splits):
            rows = slice(r * hm, (r + 1) * hm)
            x = src[rows, :]
            a = jnp.dot(x, wg, preferred_element_type=F32)
            b = jnp.dot(x, wu, preferred_element_type=F32)
            mid = (a * jax.nn.sigmoid(a) * b).astype(BF16)
            o_ref[rows, :] += jnp.dot(mid, wd, preferred_element_type=F32)


def _swiglu(tile_expert, n_valid, x, wg, wu, wd, res=None, *, tm, tf):
    residual = res is not None
    p, d = x.shape
    d_ff = wg.shape[2]
    n_f = d_ff // tf
    n_tiles = p // tm

    def mm(m, nv):
        return jnp.minimum(m, nv[0] - 1)

    def ff(m, f, nv):
        return jnp.where(m < nv[0], f, n_f - 1)

    in_specs = [
        pl.BlockSpec((tm, d), lambda m, f, te, nv: (mm(m, nv), 0)),
        pl.BlockSpec((None, d, tf), lambda m, f, te, nv: (te[mm(m, nv)], 0, ff(m, f, nv))),
        pl.BlockSpec((None, d, tf), lambda m, f, te, nv: (te[mm(m, nv)], 0, ff(m, f, nv))),
        pl.BlockSpec((None, tf, d), lambda m, f, te, nv: (te[mm(m, nv)], ff(m, f, nv), 0)),
    ]
    args = [x, wg, wu, wd]
    scratch = []
    if residual:
        in_specs.append(pl.BlockSpec((tm, d), lambda m, f, te, nv: (mm(m, nv), 0)))
        args.append(res)
    else:
        scratch.append(pltpu.VMEM((tm, d), BF16))
    row_splits = 2 if tm % 32 == 0 else 1
    return pl.pallas_call(
        functools.partial(_swiglu_kernel, residual=residual, row_splits=row_splits),
        grid_spec=pltpu.PrefetchScalarGridSpec(
            num_scalar_prefetch=2,
            grid=(n_tiles, n_f),
            in_specs=in_specs,
            out_specs=pl.BlockSpec((tm, d), lambda m, f, te, nv: (m, 0)),
            scratch_shapes=scratch,
        ),
        out_shape=jax.ShapeDtypeStruct((p, d), F32),
        compiler_params=_cparams("arbitrary", "arbitrary"),
        name="swiglu_residual" if residual else "swiglu_experts",
    )(tile_expert, n_valid, *args)


def _row_copy(src, src_row, dst, dst_row, sem):
    return pltpu.make_async_copy(src.at[pl.ds(src_row, 1)], dst.at[pl.ds(dst_row, 1)], sem)


def _dispatch_kernel(pos_ref, h_ref, xs_in_ref, xs_ref, sem, *, tt):
    del xs_in_ref

    def body(r, carry):
        for k in range(TOP_K):
            _row_copy(h_ref, r, xs_ref, pos_ref[0, TOP_K * r + k], sem).start(priority=k % 2)
        return carry

    lax.fori_loop(0, tt, body, 0, unroll=8)
    for k in range(TOP_K):
        pltpu.make_async_copy(h_ref, xs_ref.at[pl.ds(0, tt)], sem).wait()


def _dispatch(pos_tiles, h, xs):
    n, d = h.shape
    n_tiles = pos_tiles.shape[0]
    tt = n // n_tiles
    return pl.pallas_call(
        functools.partial(_dispatch_kernel, tt=tt),
        grid=(n_tiles,),
        in_specs=[
            pl.BlockSpec((None, 1, TOP_K * tt), lambda i: (i, 0, 0), memory_space=pltpu.SMEM),
            pl.BlockSpec((tt, d), lambda i: (i, 0)),
            pl.BlockSpec(memory_space=pl.ANY),
        ],
        out_specs=pl.BlockSpec(memory_space=pl.ANY),
        out_shape=jax.ShapeDtypeStruct(xs.shape, xs.dtype),
        scratch_shapes=[pltpu.SemaphoreType.DMA],
        input_output_aliases={2: 0},
        compiler_params=_cparams("arbitrary"),
        name="moe_dispatch",
    )(pos_tiles, h, xs)


def _pool_kernel(*refs, tm, pos0, windows, has_halo, n_experts):
    if has_halo:
        (prev_ref, halo_ref, x_ref, g1_ref, wp_ref, sc_ref, g2_ref, wr_ref,
         x3_ref, h_ref, hn_ref, gate_ref, idx_ref, ext_ref) = refs
    else:
        (prev_ref, x_ref, g1_ref, wp_ref, sc_ref, g2_ref, wr_ref,
         x3_ref, h_ref, hn_ref, gate_ref, idx_ref, ext_ref) = refs
    t = pl.program_id(1)
    x = x_ref[0]
    hn = _rms(x, g1_ref[...])
    hn_ref[0] = hn
    if has_halo:
        @pl.when(t == 0)
        def _():
            ext_ref[0:POOL_HALO, :] = prev_ref[0]

        @pl.when(t > 0)
        def _():
            ext_ref[0:POOL_HALO, :] = _rms(halo_ref[0], g1_ref[...])
    else:
        ext_ref[0:POOL_HALO, :] = prev_ref[0]
    ext_ref[POOL_HALO:POOL_HALO + tm, :] = hn

    d = x.shape[1]
    gc = d // len(windows)
    pos = pos0 + t * tm + lax.broadcasted_iota(jnp.int32, (tm, 1), 0)
    for gi, w in enumerate(windows):
        lanes = slice(gi * gc, (gi + 1) * gc)
        s = ext_ref[POOL_HALO:POOL_HALO + tm, lanes]
        for j in range(1, w):
            s = s + ext_ref[POOL_HALO - j:POOL_HALO - j + tm, lanes]
        cnt = jnp.minimum(pos + 1, w).astype(F32)
        diff = s / cnt - hn[:, lanes]
        y = jnp.dot(diff.astype(BF16), wp_ref[gi], preferred_element_type=F32)
        x3_ref[0, :, lanes] = x[:, lanes] + y * sc_ref[:, lanes]

    x3 = x3_ref[0]
    h2 = _rms(x3, g2_ref[...])
    h_ref[0] = h2
    logits = jnp.dot(h2, wr_ref[...], preferred_element_type=F32, precision=lax.Precision.HIGHEST)
    lane = lax.broadcasted_iota(jnp.int32, logits.shape, 1)
    logits = jnp.where(lane < n_experts, logits, -jnp.inf)
    m1 = jnp.max(logits, axis=-1, keepdims=True)
    i1 = jnp.min(jnp.where(logits == m1, lane, LANES), axis=-1, keepdims=True)
    rest = jnp.where(lane == i1, -jnp.inf, logits)
    m2 = jnp.max(rest, axis=-1, keepdims=True)
    i2 = jnp.min(jnp.where(rest == m2, lane, LANES), axis=-1, keepdims=True)
    e = jnp.exp(m2 - m1)
    g1 = 1.0 / (1.0 + e)
    g2 = e / (1.0 + e)
    gate_ref[0] = jnp.where(lane == 0, g1, jnp.where(lane == 1, g2, 0.0))
    idx_ref[0] = jnp.where(lane == 0, i1, jnp.where(lane == 1, i2, 0))


def _pool_layer(prev_pad, x, g1, w_pool, scale, g2, w_router_pad, *, tm, pos0, n_experts):
    bsz, t_len, d = x.shape
    n_t = t_len // tm
    has_halo = n_t > 1
    hb = tm // POOL_HALO if has_halo else 1
    cur = lambda b, t: (b, t, 0)
    full2 = lambda b, t: (0, 0)
    in_specs = [pl.BlockSpec((1, POOL_HALO, d), lambda b, t: (b, 0, 0))]
    args = [prev_pad]
    if has_halo:
        in_specs.append(pl.BlockSpec((1, POOL_HALO, d), lambda b, t: (b, jnp.maximum(t * hb - 1, 0), 0)))
        args.append(x)
    in_specs += [
        pl.BlockSpec((1, tm, d), cur),
        pl.BlockSpec((1, d), full2),
        pl.BlockSpec(w_pool.shape, lambda b, t: (0, 0, 0)),
        pl.BlockSpec((1, d), full2),
        pl.BlockSpec((1, d), full2),
        pl.BlockSpec(w_router_pad.shape, full2),
    ]
    args += [x, g1, w_pool, scale, g2, w_router_pad]
    return pl.pallas_call(
        functools.partial(_pool_kernel, tm=tm, pos0=pos0, windows=POOL_WINDOWS, has_halo=has_halo,
                          n_experts=n_experts),
        grid=(bsz, n_t),
        in_specs=in_specs,
        out_specs=[
            pl.BlockSpec((1, tm, d), cur),
            pl.BlockSpec((1, tm, d), cur),
            pl.BlockSpec((1, tm, d), cur),
            pl.BlockSpec((1, tm, LANES), cur),
            pl.BlockSpec((1, tm, LANES), cur),
        ],
        out_shape=[
            jax.ShapeDtypeStruct((bsz, t_len, d), F32),
            jax.ShapeDtypeStruct((bsz, t_len, d), F32),
            jax.ShapeDtypeStruct((bsz, t_len, d), F32),
            jax.ShapeDtypeStruct((bsz, t_len, LANES), F32),
            jax.ShapeDtypeStruct((bsz, t_len, LANES), jnp.int32),
        ],
        scratch_shapes=[pltpu.VMEM((POOL_HALO + tm, d), F32)],
        compiler_params=_cparams("parallel", "parallel"),
        name="pool_layer",
    )(*args)


def _combine_kernel(pos_cur_ref, pos_nxt_ref, x_ref, gate_ref, g_ref, ys_ref, y_ref, buf_ref, sem,
                    *, tt, apply_norm):
    i = pl.program_id(0)
    n = pl.num_programs(0)
    slot = i % 2

    def fetch(pos_ref, s):
        def body(r, carry):
            for k in range(TOP_K):
                _row_copy(ys_ref, pos_ref[0, TOP_K * r + k], buf_ref.at[s, k], r,
                          sem.at[s]).start(priority=k % 2)
            return carry
        lax.fori_loop(0, tt, body, 0, unroll=8)

    @pl.when(i == 0)
    def _():
        fetch(pos_cur_ref, 0)

    @pl.when(i + 1 < n)
    def _():
        fetch(pos_nxt_ref, 1 - slot)

    for k in range(TOP_K):
        pltpu.make_async_copy(ys_ref.at[pl.ds(0, tt)], buf_ref.at[slot, k], sem.at[slot]).wait()
    moe = gate_ref[:, 0:1] * buf_ref[slot, 0]
    for k in range(1, TOP_K):
        moe = moe + gate_ref[:, k:k + 1] * buf_ref[slot, k]
    y = x_ref[...] + moe
    y_ref[...] = _rms(y, g_ref[...]) if apply_norm else y


def _combine(pos_tiles, x, gates, g, ys, *, apply_norm):
    n, d = x.shape
    n_tiles = pos_tiles.shape[0]
    tt = n // n_tiles
    pos_spec = lambda fn: pl.BlockSpec((None, 1, TOP_K * tt), fn, memory_space=pltpu.SMEM)
    return pl.pallas_call(
        functools.partial(_combine_kernel, tt=tt, apply_norm=apply_norm),
        grid=(n_tiles,),
        in_specs=[
            pos_spec(lambda i: (i, 0, 0)),
            pos_spec(lambda i: (jnp.minimum(i + 1, n_tiles - 1), 0, 0)),
            pl.BlockSpec((tt, d), lambda i: (i, 0)),
            pl.BlockSpec((tt, LANES), lambda i: (i, 0)),
            pl.BlockSpec((1, d), lambda i: (0, 0)),
            pl.BlockSpec(memory_space=pl.ANY),
        ],
        out_specs=pl.BlockSpec((tt, d), lambda i: (i, 0)),
        out_shape=jax.ShapeDtypeStruct((n, d), F32),
        scratch_shapes=[pltpu.VMEM((2, TOP_K, tt, d), F32), pltpu.SemaphoreType.DMA((2,))],
        compiler_params=_cparams("arbitrary"),
        name="moe_combine",
    )(pos_tiles, pos_tiles, x, gates, g, ys)


def _final_kernel(x_ref, g_ref, y_ref):
    y_ref[...] = _rms(x_ref[...], g_ref[...])


def _final(x, g, *, tm):
    n, d = x.shape
    row = lambda i: (i, 0)
    return pl.pallas_call(
        _final_kernel,
        grid=(n // tm,),
        in_specs=[pl.BlockSpec((tm, d), row), pl.BlockSpec((1, d), lambda i: (0, 0))],
        out_specs=pl.BlockSpec((tm, d), row),
        out_shape=jax.ShapeDtypeStruct((n, d), F32),
        compiler_params=_cparams("parallel"),
        name="final_norm",
    )(x, g)


def _swap_halves(w):
    half = w.shape[-1] // 2
    return jnp.concatenate([w[..., half:], w[..., :half]], axis=-1)


def _slot(parts, total=HEAD_SLOT):
    used = sum(p.shape[-1] for p in parts)
    pad = jnp.zeros(parts[0].shape[:-1] + (total - used,), parts[0].dtype)
    return jnp.concatenate(list(parts) + [pad], axis=-1)


def _prep_even_weights(w_in, w_uq, w_ukv, w_out, *, dc, qr, kvr):
    d = w_in.shape[0]
    body = w_in[:, :2 * dc + qr + kvr]
    w_kpe = w_in[:, 2 * dc + qr + kvr:]
    z_nope = jnp.zeros((d, NOPE_DIM), w_in.dtype)
    w_in_ext = jnp.concatenate(
        [body, _slot([z_nope, w_kpe]), _slot([z_nope, _swap_halves(w_kpe)])], axis=-1).astype(BF16)

    uq = w_uq.reshape(qr, N_HEADS, NOPE_DIM + ROPE_DIM)
    uq_nope, uq_pe = uq[..., :NOPE_DIM], uq[..., NOPE_DIM:]
    qa = _slot([uq_nope, uq_pe]).reshape(qr, N_HEADS * HEAD_SLOT)
    qb = _slot([jnp.zeros_like(uq_nope), _swap_halves(uq_pe)]).reshape(qr, N_HEADS * HEAD_SLOT)
    w_uq_ext = jnp.concatenate([qa, qb], axis=-1).astype(BF16)

    ukv = w_ukv.reshape(kvr, N_HEADS, NOPE_DIM + V_DIM)
    uk, uv = ukv[..., :NOPE_DIM], ukv[..., NOPE_DIM:]
    w_kv_ext = jnp.concatenate([_slot([uk]).reshape(kvr, N_HEADS * HEAD_SLOT),
                                _slot([uv]).reshape(kvr, N_HEADS * HEAD_SLOT)], axis=-1).astype(BF16)
    w_q2lat = jnp.concatenate(
        [jnp.transpose(uk, (1, 2, 0)),
         jnp.zeros((N_HEADS, HEAD_SLOT - NOPE_DIM, kvr), w_ukv.dtype)], axis=1).astype(BF16)
    w_uv_pad = _slot([jnp.transpose(uv, (1, 0, 2))]).astype(BF16)

    w_c = w_out[:dc].astype(BF16)
    wo = w_out[dc:].reshape(N_HEADS, V_DIM, d)
    w_o = jnp.concatenate([wo, jnp.zeros((N_HEADS, HEAD_SLOT - V_DIM, d), w_out.dtype)],
                          axis=1).reshape(N_HEADS * HEAD_SLOT, d).astype(BF16)
    return w_in_ext, w_uq_ext, w_kv_ext, w_q2lat, w_uv_pad, w_c, w_o


def _rope_tables(pos):
    half = ROPE_DIM // 2
    inv_freq = jnp.power(ROPE_THETA, -jnp.arange(half, dtype=F32) / half)
    ang = pos.astype(F32)[:, None] * inv_freq[None, :]
    cos, sin = jnp.cos(ang), jnp.sin(ang)
    n = pos.shape[0]
    z_nope = jnp.zeros((n, NOPE_DIM), F32)
    cos_k = _slot([z_nope, cos, cos])
    sin_k = _slot([z_nope, -sin, sin])
    cos_q = _slot([jnp.ones((n, NOPE_DIM), F32), cos, cos]) * (ATTN_SCALE * LOG2E)
    sin_q = sin_k * (ATTN_SCALE * LOG2E)
    return cos_k, sin_k, cos_q, sin_q


def _even_layer(x, conv_prev, pos, attend, P, layer, i):
    bsz, t_len, d = x.shape
    n = bsz * t_len
    dc = P['conv_w'].shape[2]
    qr = P['q_norm'].shape[1]
    kvr = P['kv_norm'].shape[1]
    k_taps = P['conv_w'].shape[1]
    tm = _row_tile(n, 512) if t_len >= 512 else n
    w_in_ext, w_uq_ext, w_kv_ext, w_q2lat, w_uv_pad, w_c, w_o = _prep_even_weights(
        P['w_in'][i], P['w_uq'][i], P['w_ukv'][i], P['w_out'][i], dc=dc, qr=qr, kvr=kvr)
    cos_k, sin_k, cos_q, sin_q = _rope_tables(pos)
    if t_len < tm:
        reps = tm // t_len
        cos_k, sin_k, cos_q, sin_q = (jnp.tile(a, (reps, 1)) for a in (cos_k, sin_k, cos_q, sin_q))

    xf = x.reshape(n, d)
    u, qn, ckv, kpe_slot = _in_proj(
        xf, P['norm_mix'][layer][None], w_in_ext, P['q_norm'][i][None], P['kv_norm'][i][None],
        cos_k, sin_k, dc=dc, qr=qr, kvr=kvr, tm=tm)

    u3 = u.reshape(bsz, t_len, dc)
    prev_pad = jnp.pad(conv_prev.astype(F32), ((0, 0), (CONV_HALO - (k_taps - 1), 0), (0, 0)))
    c = _conv_branch(prev_pad, u3, P['conv_w'][i], P['conv_b'][i][None], P['conv_ln_g'][i][None],
                     P['conv_ln_b'][i][None], tc=min(t_len, 512))
    conv_state = jnp.concatenate([conv_prev.astype(F32), u3], axis=1)[:, -(k_taps - 1):]

    q_slots = _q_proj(qn, w_uq_ext, cos_q, sin_q, tm=tm)
    o_slots = attend(q_slots, ckv, kpe_slot, w_kv_ext, w_q2lat, w_uv_pad, tm)

    x1, h_ffn = _out_proj(xf, c.reshape(n, dc), o_slots, w_c, w_o, P['norm_ffn'][layer][None], tm=tm)

    tm_f = _row_tile(n, 1024)
    n_tiles = n // tm_f
    x2 = _swiglu(jnp.full((n_tiles,), i, jnp.int32), jnp.full((1,), n_tiles, jnp.int32), h_ffn,
                 P['ffn_w_gate'], P['ffn_w_up'], P['ffn_w_down'], x1, tm=tm_f, tf=256)
    kpe = kpe_slot[:, NOPE_DIM:NOPE_DIM + ROPE_DIM]
    return (x2.reshape(bsz, t_len, d), conv_state, ckv.reshape(bsz, t_len, kvr),
            kpe.reshape(bsz, t_len, ROPE_DIM))


def _make_attend_prompt(bsz, t_len):
    def attend(q_slots, ckv, kpe_slot, w_kv_ext, w_q2lat, w_uv_pad, tm):
        k_slots, v_slots = _kv_proj(ckv, kpe_slot, w_kv_ext, tm=tm)
        return _attn_prompt(q_slots, k_slots, v_slots, bsz=bsz, t_len=t_len, tq=min(t_len, 512))
    return attend


def _make_attend_sample(bsz, t_len, cache_ckv, cache_kpe_t, page_table, i):
    t_pad = 8

    def attend(q_slots, ckv, kpe_slot, w_kv_ext, w_q2lat, w_uv_pad, tm):
        n = bsz * t_len
        kvr = ckv.shape[1]
        q_lat = _qlat(q_slots, w_q2lat)
        q_lat = q_lat.reshape(N_HEADS, bsz, t_len, kvr)
        q_lat = jnp.pad(q_lat, ((0, 0), (0, 0), (0, t_pad - t_len), (0, 0)))
        q_lat = jnp.transpose(q_lat, (1, 0, 2, 3)).reshape(bsz, N_HEADS * t_pad, kvr)
        q_pe = q_slots.reshape(bsz, t_len, N_HEADS, HEAD_SLOT)[..., NOPE_DIM:NOPE_DIM + ROPE_DIM]
        q_pe = jnp.pad(jnp.transpose(q_pe, (0, 2, 1, 3)), ((0, 0), (0, 0), (0, t_pad - t_len), (0, 0)))
        q_pe = q_pe.reshape(bsz, N_HEADS * t_pad, ROPE_DIM)
        pad_t = ((0, 0), (0, t_pad - t_len), (0, 0))
        ckv_new = jnp.pad(ckv.reshape(bsz, t_len, kvr), pad_t)
        kpe_new = jnp.pad(kpe_slot[:, NOPE_DIM:NOPE_DIM + ROPE_DIM].reshape(bsz, t_len, ROPE_DIM), pad_t)
        o_lat = _attn_sample(page_table, q_lat, q_pe, ckv_new, kpe_new, cache_ckv, cache_kpe_t, i,
                             t_new=t_len, t_pad=t_pad, pages_per_step=32,
                             pages_per_chunk=32)
        o_lat = o_lat.reshape(bsz, N_HEADS, t_pad, kvr)[:, :, :t_len]
        o_lat = jnp.transpose(o_lat, (1, 0, 2, 3)).reshape(N_HEADS, n, kvr)
        return _olat(o_lat, w_uv_pad)
    return attend


def _odd_layer_mixer(x, pool_prev, pos0, P, layer, i):
    bsz, t_len, d = x.shape
    n_experts = P['router_w'].shape[2]
    ctx = max(POOL_WINDOWS) - 1
    prev_pad = jnp.pad(pool_prev.astype(F32), ((0, 0), (POOL_HALO - ctx, 0), (0, 0)))
    w_router_pad = jnp.pad(P['router_w'][i], ((0, 0), (0, LANES - n_experts)))
    x3, h_moe, hn, gates, ids = _pool_layer(
        prev_pad, x, P['norm_mix'][layer][None], P['pool_w'][i].astype(BF16), P['pool_scale'][i][None],
        P['norm_ffn'][layer][None], w_router_pad, tm=min(t_len, 512), pos0=pos0, n_experts=n_experts)
    pool_state = jnp.concatenate([pool_prev.astype(F32), hn], axis=1)[:, -ctx:]
    n = bsz * t_len
    return (x3.reshape(n, d), h_moe.reshape(n, d), pool_state, gates.reshape(n, LANES),
            ids.reshape(n, LANES)[:, :TOP_K])


def _moe_plan(ids, n_experts, tm):
    n = ids.shape[0]
    flat = ids.reshape(-1)
    experts = jnp.arange(n_experts, dtype=jnp.int32)
    onehot = (flat[:, None] == experts[None, :]).astype(jnp.int32)
    csum = jnp.cumsum(onehot, axis=0)
    counts = csum[-1]
    rank = jnp.sum(csum * onehot, axis=1) - 1
    padded = ((counts + tm - 1) // tm) * tm
    ends = jnp.cumsum(padded)
    starts = ends - padded
    pos = jnp.sum(starts[None, :] * onehot, axis=1) + rank
    n_tiles = (n * TOP_K + tm - 1) // tm + n_experts
    n_valid = (ends[-1] // tm).astype(jnp.int32)
    tile_start = jnp.arange(n_tiles, dtype=jnp.int32) * tm
    tile_expert = jnp.sum((ends[None, :] <= tile_start[:, None]).astype(jnp.int32), axis=1)
    tile_expert = jnp.minimum(tile_expert, n_experts - 1)
    return pos.reshape(n, TOP_K), tile_expert, n_valid.reshape(1), n_tiles


def _moe_layer(groups, P, i, g_final, *, tm, tt):
    d = groups[0][0].shape[1]
    n_experts = P['router_w'].shape[2]
    ids = jnp.concatenate([g[3] for g in groups], axis=0)
    pos, tile_expert, n_valid, n_tiles = _moe_plan(ids, n_experts, tm)
    xs = jnp.zeros((n_tiles * tm, d), F32)
    pos_tiles = []
    lo = 0
    for x3, h, _, _ in groups:
        n = x3.shape[0]
        t = min(tt, n)
        pt = pos[lo:lo + n].reshape(n // t, 1, TOP_K * t)
        xs = _dispatch(pt, h, xs)
        pos_tiles.append(pt)
        lo += n
    ys = _swiglu(tile_expert, n_valid, xs, P['moe_w_gate'][i], P['moe_w_up'][i], P['moe_w_down'][i],
                 tm=tm, tf=512)
    gain = g_final if g_final is not None else jnp.ones((1, d), F32)
    return [_combine(pt, x3, gates, gain, ys, apply_norm=g_final is not None)
            for pt, (x3, _, gates, _) in zip(pos_tiles, groups)]


def kernel(x_prompt, x_sample, cache_ckv, cache_kpe, page_table, state_conv, state_pool, norm_mix, norm_ffn,
           norm_final, w_in, conv_w, conv_b, conv_ln_g, conv_ln_b, q_norm, w_uq, kv_norm, w_ukv, w_out,
           ffn_w_gate, ffn_w_up, ffn_w_down, pool_w, pool_scale, router_w, moe_w_gate, moe_w_up, moe_w_down):
    P = {'norm_mix': norm_mix, 'norm_ffn': norm_ffn, 'norm_final': norm_final, 'w_in': w_in,
         'conv_w': conv_w, 'conv_b': conv_b, 'conv_ln_g': conv_ln_g, 'conv_ln_b': conv_ln_b,
         'q_norm': q_norm, 'w_uq': w_uq, 'kv_norm': kv_norm, 'w_ukv': w_ukv, 'w_out': w_out,
         'ffn_w_gate': ffn_w_gate, 'ffn_w_up': ffn_w_up, 'ffn_w_down': ffn_w_down,
         'pool_w': pool_w, 'pool_scale': pool_scale, 'router_w': router_w,
         'moe_w_gate': moe_w_gate, 'moe_w_up': moe_w_up, 'moe_w_down': moe_w_down}
    depth = norm_mix.shape[0]
    d = x_prompt.shape[2]
    bp, tp, _ = x_prompt.shape
    bs, ts, _ = x_sample.shape
    past = page_table.shape[1] * cache_ckv.shape[2]
    k_taps = conv_w.shape[1]
    ctx = max(POOL_WINDOWS) - 1
    pos_p = jnp.arange(tp, dtype=jnp.int32)
    pos_s = past + jnp.arange(ts, dtype=jnp.int32)
    cache_kpe_t = jnp.swapaxes(cache_kpe, 2, 3)

    xp, xs = x_prompt, x_sample
    outs_p = {'ckv': [], 'kpe': [], 'conv': [], 'pool': []}
    outs_s = {'ckv': [], 'kpe': [], 'conv': [], 'pool': []}
    for layer in range(depth):
        i = layer // 2
        last = layer == depth - 1
        if layer % 2 == 0:
            conv0 = jnp.zeros((bp, k_taps - 1, conv_w.shape[2]), F32)
            xp, cst, ckv, kpe = _even_layer(xp, conv0, pos_p, _make_attend_prompt(bp, tp), P, layer, i)
            outs_p['conv'].append(cst); outs_p['ckv'].append(ckv); outs_p['kpe'].append(kpe)
            xs, cst, ckv, kpe = _even_layer(
                xs, state_conv[i], pos_s, _make_attend_sample(bs, ts, cache_ckv, cache_kpe_t, page_table, i),
                P, layer, i)
            outs_s['conv'].append(cst); outs_s['ckv'].append(ckv); outs_s['kpe'].append(kpe)
            if last:
                xp = _final(xp.reshape(bp * tp, d), norm_final[None], tm=_row_tile(bp * tp, 512)).reshape(xp.shape)
                xs = _final(xs.reshape(bs * ts, d), norm_final[None], tm=_row_tile(bs * ts, 512)).reshape(xs.shape)
        else:
            pool0 = jnp.zeros((bp, ctx, d), F32)
            x3p, hp, pst_p, gp, ip = _odd_layer_mixer(xp, pool0, 0, P, layer, i)
            x3s, hs, pst_s, gs, is_ = _odd_layer_mixer(xs, state_pool[i], past, P, layer, i)
            outs_p['pool'].append(pst_p); outs_s['pool'].append(pst_s)
            yp, ysm = _moe_layer([(x3p, hp, gp, ip), (x3s, hs, gs, is_)], P, i,
                                 norm_final[None] if last else None, tm=1024, tt=512)
            xp, xs = yp.reshape(xp.shape), ysm.reshape(xs.shape)

    def stack(lst, shape_if_empty):
        return jnp.stack(lst) if lst else jnp.zeros(shape_if_empty, F32)

    return (xp, xs,
            jnp.stack(outs_p['ckv']), jnp.stack(outs_p['kpe']), jnp.stack(outs_p['conv']),
            stack(outs_p['pool'], (0, bp, ctx, d)),
            jnp.stack(outs_s['ckv']), jnp.stack(outs_s['kpe']), jnp.stack(outs_s['conv']),
            stack(outs_s['pool'], (0, bs, ctx, d)))
```

```python
import functools

import jax
import jax.numpy as jnp
from jax import lax
from jax.experimental import pallas as pl
from jax.experimental.pallas import tpu as pltpu

F32 = jnp.float32
BF16 = jnp.bfloat16

N_HEADS = 8
NOPE_DIM = 64
ROPE_DIM = 32
V_DIM = 64
ROPE_THETA = 10000.0
ATTN_SCALE = (NOPE_DIM + ROPE_DIM) ** -0.5
LOG2E = 1.4426950408889634
POOL_WINDOWS = (2, 4, 8, 16)
TOP_K = 2
EPS = 1e-6

HEAD_SLOT = 128
LANES = 128
SUBLANES = 8
NEG_BIG = -1e30
VMEM_LIMIT = 56 * 1024 * 1024

CONV_HALO = 32
POOL_HALO = 16


def _cparams(*semantics):
    return pltpu.CompilerParams(dimension_semantics=semantics, vmem_limit_bytes=VMEM_LIMIT)


def _rms(x, g):
    return x * lax.rsqrt(jnp.mean(x * x, axis=-1, keepdims=True) + EPS) * g


def _row_tile(n, want):
    if n <= want:
        return n
    t = want
    while t >= 8:
        if n % t == 0 and t % 8 == 0:
            return t
        t -= 8
    return n


def _in_proj_kernel(x_ref, g_ref, w_ref, qg_ref, kvg_ref, cos_ref, sin_ref,
                    u_ref, qn_ref, ckv_ref, kpe_ref, *, dc, qr, kvr):
    h = _rms(x_ref[...], g_ref[...])
    proj = jnp.dot(h.astype(BF16), w_ref[...], preferred_element_type=F32)
    a = proj[:, :dc]
    gate = proj[:, dc:2 * dc]
    u_ref[...] = a * jax.nn.sigmoid(gate)
    o = 2 * dc
    qn_ref[...] = _rms(proj[:, o:o + qr], qg_ref[...]).astype(BF16)
    o += qr
    ckv_ref[...] = _rms(proj[:, o:o + kvr], kvg_ref[...])
    o += kvr
    kpe_ref[...] = (proj[:, o:o + HEAD_SLOT] * cos_ref[...]
                    + proj[:, o + HEAD_SLOT:o + 2 * HEAD_SLOT] * sin_ref[...])


def _in_proj(x, g, w_ext, qg, kvg, cos_k, sin_k, *, dc, qr, kvr, tm):
    n, d = x.shape
    n_tab = cos_k.shape[0] // tm
    row = lambda i: (i, 0)
    full = lambda i: (0, 0)
    tab = lambda i: (i % n_tab, 0)
    return pl.pallas_call(
        functools.partial(_in_proj_kernel, dc=dc, qr=qr, kvr=kvr),
        grid=(n // tm,),
        in_specs=[
            pl.BlockSpec((tm, d), row),
            pl.BlockSpec((1, d), full),
            pl.BlockSpec(w_ext.shape, full),
            pl.BlockSpec((1, qr), full),
            pl.BlockSpec((1, kvr), full),
            pl.BlockSpec((tm, HEAD_SLOT), tab),
            pl.BlockSpec((tm, HEAD_SLOT), tab),
        ],
        out_specs=[
            pl.BlockSpec((tm, dc), row),
            pl.BlockSpec((tm, qr), row),
            pl.BlockSpec((tm, kvr), row),
            pl.BlockSpec((tm, HEAD_SLOT), row),
        ],
        out_shape=[
            jax.ShapeDtypeStruct((n, dc), F32),
            jax.ShapeDtypeStruct((n, qr), BF16),
            jax.ShapeDtypeStruct((n, kvr), F32),
            jax.ShapeDtypeStruct((n, HEAD_SLOT), F32),
        ],
        compiler_params=_cparams("parallel"),
        name="in_proj",
    )(x, g, w_ext, qg, kvg, cos_k, sin_k)


def _conv_kernel(*refs, tc, k_taps, rows_per_chunk, has_halo):
    if has_halo:
        prev_ref, halo_ref, u_ref, w_ref, b_ref, lg_ref, lb_ref, c_ref, ext_ref, sh_ref = refs
    else:
        prev_ref, u_ref, w_ref, b_ref, lg_ref, lb_ref, c_ref, ext_ref, sh_ref = refs
    if has_halo:
        t = pl.program_id(1)

        @pl.when(t == 0)
        def _():
            ext_ref[0:CONV_HALO, :] = prev_ref[0]

        @pl.when(t > 0)
        def _():
            ext_ref[0:CONV_HALO, :] = halo_ref[0]
    else:
        ext_ref[0:CONV_HALO, :] = prev_ref[0]
    ext_ref[CONV_HALO:CONV_HALO + tc, :] = u_ref[0]

    n_rows = CONV_HALO + tc
    for s in range(1, SUBLANES):
        sh_ref[s - 1, 0:n_rows - SUBLANES, :] = ext_ref[s:s + n_rows - SUBLANES, :]

    def window(start, n):
        q, s = divmod(start, SUBLANES)
        if s == 0:
            return ext_ref[start:start + n, :]
        return sh_ref[s - 1, q * SUBLANES:q * SUBLANES + n, :]

    first = CONV_HALO - (k_taps - 1)
    for c0 in range(0, tc, rows_per_chunk):
        rc = min(rows_per_chunk, tc - c0)
        acc = w_ref[0:1, :] * window(first + c0, rc)
        for k in range(1, k_taps):
            acc = acc + w_ref[k:k + 1, :] * window(first + c0 + k, rc)
        acc = acc + b_ref[...]
        mu = jnp.mean(acc, axis=-1, keepdims=True)
        xc = acc - mu
        y = xc * lax.rsqrt(jnp.mean(xc * xc, axis=-1, keepdims=True) + EPS)
        y = y * lg_ref[...] + lb_ref[...]
        c_ref[0, c0:c0 + rc, :] = (y * jax.nn.sigmoid(y)).astype(BF16)


def _conv_branch(prev_pad, u, w, b, lg, lb, *, tc):
    bsz, t_len, c = u.shape
    k_taps = w.shape[0]
    n_t = t_len // tc
    has_halo = n_t > 1
    hb = tc // CONV_HALO if has_halo else 1
    in_specs = [pl.BlockSpec((1, CONV_HALO, c), lambda bi, ti: (bi, 0, 0))]
    args = [prev_pad]
    if has_halo:
        in_specs.append(pl.BlockSpec((1, CONV_HALO, c),
                                     lambda bi, ti: (bi, jnp.maximum(ti * hb - 1, 0), 0)))
        args.append(u)
    in_specs += [
        pl.BlockSpec((1, tc, c), lambda bi, ti: (bi, ti, 0)),
        pl.BlockSpec((k_taps, c), lambda bi, ti: (0, 0)),
        pl.BlockSpec((1, c), lambda bi, ti: (0, 0)),
        pl.BlockSpec((1, c), lambda bi, ti: (0, 0)),
        pl.BlockSpec((1, c), lambda bi, ti: (0, 0)),
    ]
    args += [u, w, b, lg, lb]
    return pl.pallas_call(
        functools.partial(_conv_kernel, tc=tc, k_taps=k_taps, rows_per_chunk=64, has_halo=has_halo),
        grid=(bsz, n_t),
        in_specs=in_specs,
        out_specs=pl.BlockSpec((1, tc, c), lambda bi, ti: (bi, ti, 0)),
        out_shape=jax.ShapeDtypeStruct((bsz, t_len, c), BF16),
        scratch_shapes=[pltpu.VMEM((CONV_HALO + tc, c), F32),
                        pltpu.VMEM((SUBLANES - 1, CONV_HALO + tc, c), F32)],
        compiler_params=_cparams("parallel", "parallel"),
        name="conv_branch",
    )(*args)


def _q_proj_kernel(qn_ref, w_ref, cos_ref, sin_ref, q_ref, *, n_heads):
    proj = jnp.dot(qn_ref[...], w_ref[...], preferred_element_type=F32)
    half = n_heads * HEAD_SLOT
    cos = cos_ref[...]
    sin = sin_ref[...]
    for h in range(n_heads):
        a = proj[:, h * HEAD_SLOT:(h + 1) * HEAD_SLOT]
        b = proj[:, half + h * HEAD_SLOT:half + (h + 1) * HEAD_SLOT]
        q_ref[:, h * HEAD_SLOT:(h + 1) * HEAD_SLOT] = (a * cos + b * sin).astype(BF16)


def _q_proj(qn, w_ext, cos_q, sin_q, *, tm):
    n, qr = qn.shape
    n_tab = cos_q.shape[0] // tm
    width = N_HEADS * HEAD_SLOT
    return pl.pallas_call(
        functools.partial(_q_proj_kernel, n_heads=N_HEADS),
        grid=(n // tm,),
        in_specs=[
            pl.BlockSpec((tm, qr), lambda i: (i, 0)),
            pl.BlockSpec(w_ext.shape, lambda i: (0, 0)),
            pl.BlockSpec((tm, HEAD_SLOT), lambda i: (i % n_tab, 0)),
            pl.BlockSpec((tm, HEAD_SLOT), lambda i: (i % n_tab, 0)),
        ],
        out_specs=pl.BlockSpec((tm, width), lambda i: (i, 0)),
        out_shape=jax.ShapeDtypeStruct((n, width), BF16),
        compiler_params=_cparams("parallel"),
        name="q_proj",
    )(qn, w_ext, cos_q, sin_q)


def _kv_proj_kernel(ckv_ref, kpe_ref, w_ref, k_ref, v_ref, *, n_heads):
    proj = jnp.dot(ckv_ref[...].astype(BF16), w_ref[...], preferred_element_type=F32)
    half = n_heads * HEAD_SLOT
    kpe = kpe_ref[...]
    for h in range(n_heads):
        k_ref[:, h * HEAD_SLOT:(h + 1) * HEAD_SLOT] = (
            proj[:, h * HEAD_SLOT:(h + 1) * HEAD_SLOT] + kpe).astype(BF16)
    v = proj[:, half:]
    lane = lax.broadcasted_iota(jnp.int32, v.shape, 1)
    v_ref[...] = jnp.where(lane % HEAD_SLOT == V_DIM, 1.0, v).astype(BF16)


def _kv_proj(ckv, kpe_slot, w_ext, *, tm):
    n, kvr = ckv.shape
    width = N_HEADS * HEAD_SLOT
    return pl.pallas_call(
        functools.partial(_kv_proj_kernel, n_heads=N_HEADS),
        grid=(n // tm,),
        in_specs=[
            pl.BlockSpec((tm, kvr), lambda i: (i, 0)),
            pl.BlockSpec((tm, HEAD_SLOT), lambda i: (i, 0)),
            pl.BlockSpec(w_ext.shape, lambda i: (0, 0)),
        ],
        out_specs=[pl.BlockSpec((tm, width), lambda i: (i, 0)),
                   pl.BlockSpec((tm, width), lambda i: (i, 0))],
        out_shape=[jax.ShapeDtypeStruct((n, width), BF16),
                   jax.ShapeDtypeStruct((n, width), BF16)],
        compiler_params=_cparams("parallel"),
        name="kv_proj",
    )(ckv, kpe_slot, w_ext)


def _attn_prompt_kernel(q_ref, k_ref, v_ref, o_ref, *, tq, n_heads):
    qi = pl.program_id(1)
    row = lax.broadcasted_iota(jnp.int32, (tq, tq), 0)
    col = lax.broadcasted_iota(jnp.int32, (tq, tq), 1)
    causal = col <= row

    head_lanes = [slice(h * HEAD_SLOT, (h + 1) * HEAD_SLOT) for h in range(n_heads)]

    def step(kt, carry, masked):
        ms, accs = carry
        start = pl.multiple_of(kt * tq, tq)
        new_ms, new_accs = [], []
        for h, lanes in enumerate(head_lanes):
            k = k_ref[pl.ds(start, tq), lanes]
            v = v_ref[pl.ds(start, tq), lanes]
            s = lax.dot_general(q_ref[:, lanes], k, (((1,), (1,)), ((), ())), preferred_element_type=F32)
            if masked:
                s = jnp.where(causal, s, NEG_BIG)
            m_new = jnp.maximum(ms[h], jnp.max(s, axis=-1, keepdims=True))
            p = jnp.exp2(s - m_new)
            new_accs.append(jnp.exp2(ms[h] - m_new) * accs[h]
                            + jnp.dot(p.astype(BF16), v, preferred_element_type=F32))
            new_ms.append(m_new)
        return tuple(new_ms), tuple(new_accs)

    init = (tuple(jnp.full((tq, 1), NEG_BIG, F32) for _ in head_lanes),
            tuple(jnp.zeros((tq, HEAD_SLOT), F32) for _ in head_lanes))
    carry = lax.fori_loop(0, qi, functools.partial(step, masked=False), init)
    _, accs = step(qi, carry, True)
    for lanes, acc in zip(head_lanes, accs):
        o_ref[:, lanes] = (acc / acc[:, V_DIM:V_DIM + 1]).astype(BF16)


def _attn_prompt(q, k, v, *, bsz, t_len, tq):
    width = q.shape[1]
    nq = t_len // tq
    return pl.pallas_call(
        functools.partial(_attn_prompt_kernel, tq=tq, n_heads=N_HEADS),
        grid=(bsz, nq),
        in_specs=[
            pl.BlockSpec((tq, width), lambda b, i: (b * nq + i, 0)),
            pl.BlockSpec((t_len, width), lambda b, i: (b, 0)),
            pl.BlockSpec((t_len, width), lambda b, i: (b, 0)),
        ],
        out_specs=pl.BlockSpec((tq, width), lambda b, i: (b * nq + i, 0)),
        out_shape=jax.ShapeDtypeStruct(q.shape, BF16),
        compiler_params=_cparams("parallel", "arbitrary"),
        name="attn_prompt",
    )(q, k, v)


def _qlat_kernel(q_ref, w_ref, o_ref, *, n_heads):
    for h in range(n_heads):
        o_ref[h] = jnp.dot(q_ref[:, h * HEAD_SLOT:(h + 1) * HEAD_SLOT], w_ref[h],
                           preferred_element_type=F32).astype(BF16)


def _qlat(q_slots, w_q2lat):
    n = q_slots.shape[0]
    n_heads, _, kvr = w_q2lat.shape
    return pl.pallas_call(
        functools.partial(_qlat_kernel, n_heads=n_heads),
        out_shape=jax.ShapeDtypeStruct((n_heads, n, kvr), BF16),
        compiler_params=pltpu.CompilerParams(vmem_limit_bytes=VMEM_LIMIT),
        name="q_latent",
    )(q_slots, w_q2lat)


def _attn_sample_kernel(pt_ref, ql_ref, qp_ref, cn_ref, kn_ref, ckv_hbm, kpe_hbm, o_ref,
                        cbuf_ref, pbuf_ref, kbf_ref, pbf_ref, m_ref, l_ref, acc_ref, sem,
                        *, layer, pages_per_step, pages_per_chunk, page, t_new, t_pad):
    g = pages_per_step
    bsz, n_pages = pt_ref.shape
    steps_per_seq = n_pages // g
    total = bsz * steps_per_seq

    def page_copies(i, slot):
        b = i // steps_per_seq
        first = (i % steps_per_seq) * g
        copies = []
        for j in range(g):
            pid = pt_ref[b, first + j]
            copies.append(pltpu.make_async_copy(
                ckv_hbm.at[layer, pid], cbuf_ref.at[slot, pl.ds(j * page, page)], sem.at[slot]))
            copies.append(pltpu.make_async_copy(
                kpe_hbm.at[layer, pid], pbuf_ref.at[slot, j], sem.at[slot]))
        return copies

    for cp in page_copies(0, 0):
        cp.start()

    def step(i, carry):
        slot = i % 2
        b = i // steps_per_seq
        s_idx = i % steps_per_seq

        @pl.when(i + 1 < total)
        def _():
            for cp in page_copies(i + 1, 1 - slot):
                cp.start()

        for cp in page_copies(i, slot):
            cp.wait()
        _attn_sample_step(b, s_idx, slot, steps_per_seq, ql_ref, qp_ref, cn_ref, kn_ref, o_ref, cbuf_ref,
                          pbuf_ref, kbf_ref, pbf_ref, m_ref, l_ref, acc_ref, g=g,
                          pages_per_chunk=pages_per_chunk, page=page, t_new=t_new, t_pad=t_pad)
        return carry

    lax.fori_loop(0, total, step, 0)


def _attn_sample_step(b, s_idx, slot, steps_per_seq, ql_ref, qp_ref, cn_ref, kn_ref, o_ref, cbuf_ref, pbuf_ref,
                      kbf_ref, pbf_ref, m_ref, l_ref, acc_ref, *, g, pages_per_chunk, page, t_new, t_pad):
    ql = ql_ref[b]
    qp = qp_ref[b]
    rows = ql.shape[0]
    contract_last = (((1,), (1,)), ((), ()))

    @pl.when(s_idx == 0)
    def _():
        cn = cn_ref[b].astype(BF16)
        kn = kn_ref[b].astype(BF16)
        s = (lax.dot_general(ql, cn, contract_last, preferred_element_type=F32)
             + lax.dot_general(qp, kn, contract_last, preferred_element_type=F32))
        t_of_row = lax.broadcasted_iota(jnp.int32, (rows, t_pad), 0) % t_pad
        key = lax.broadcasted_iota(jnp.int32, (rows, t_pad), 1)
        ok = (key <= t_of_row) & (key < t_new)
        s = jnp.where(ok, s, NEG_BIG)
        m = jnp.max(s, axis=-1, keepdims=True)
        p = jnp.where(ok, jnp.exp2(s - m), 0.0)
        m_ref[...] = m
        l_ref[...] = jnp.sum(p, axis=-1, keepdims=True)
        acc_ref[...] = jnp.dot(p.astype(BF16), cn, preferred_element_type=F32)

    kbf_ref[...] = cbuf_ref[slot].astype(BF16)
    for j in range(g):
        pbf_ref[:, j * page:(j + 1) * page] = pbuf_ref[slot, j].astype(BF16)

    ck = pages_per_chunk * page
    parts = []
    for c in range(g // pages_per_chunk):
        kb = kbf_ref[c * ck:(c + 1) * ck, :]
        s = (lax.dot_general(ql, kb, contract_last, preferred_element_type=F32)
             + jnp.dot(qp, pbf_ref[:, c * ck:(c + 1) * ck], preferred_element_type=F32))
        mc = jnp.max(s, axis=-1, keepdims=True)
        p = jnp.exp2(s - mc)
        parts.append((mc, jnp.sum(p, axis=-1, keepdims=True),
                      jnp.dot(p.astype(BF16), kb, preferred_element_type=F32)))
    m_old = m_ref[...]
    m_new = m_old
    for mc, _, _ in parts:
        m_new = jnp.maximum(m_new, mc)
    alpha = jnp.exp2(m_old - m_new)
    l = alpha * l_ref[...]
    acc = alpha * acc_ref[...]
    for mc, lc, ac in parts:
        w = jnp.exp2(mc - m_new)
        l = l + w * lc
        acc = acc + w * ac
    m_ref[...] = m_new
    l_ref[...] = l
    acc_ref[...] = acc

    @pl.when(s_idx == steps_per_seq - 1)
    def _():
        o_ref[b] = acc_ref[...] / l_ref[...]


def _attn_sample(page_table, q_lat, q_pe, ckv_new, kpe_new, cache_ckv, cache_kpe_t, layer, *,
                 t_new, t_pad, pages_per_step, pages_per_chunk):
    bsz, rows, kvr = q_lat.shape
    rope = q_pe.shape[2]
    page = cache_ckv.shape[2]
    g = pages_per_step
    assert page_table.shape[1] % g == 0 and g % pages_per_chunk == 0
    vmem = pl.BlockSpec(memory_space=pltpu.VMEM)
    hbm = pl.BlockSpec(memory_space=pl.ANY)
    return pl.pallas_call(
        functools.partial(_attn_sample_kernel, layer=layer, pages_per_step=g, pages_per_chunk=pages_per_chunk,
                          page=page, t_new=t_new, t_pad=t_pad),
        in_specs=[pl.BlockSpec(memory_space=pltpu.SMEM), vmem, vmem, vmem, vmem, hbm, hbm],
        out_specs=vmem,
        out_shape=jax.ShapeDtypeStruct((bsz, rows, kvr), F32),
        scratch_shapes=[
            pltpu.VMEM((2, g * page, kvr), F32),
            pltpu.VMEM((2, g, rope, page), F32),
            pltpu.VMEM((g * page, kvr), BF16),
            pltpu.VMEM((rope, g * page), BF16),
            pltpu.VMEM((rows, 1), F32),
            pltpu.VMEM((rows, 1), F32),
            pltpu.VMEM((rows, kvr), F32),
            pltpu.SemaphoreType.DMA((2,)),
        ],
        compiler_params=pltpu.CompilerParams(vmem_limit_bytes=VMEM_LIMIT),
        name="attn_sample",
    )(page_table, q_lat, q_pe, ckv_new, kpe_new, cache_ckv, cache_kpe_t)


def _olat_kernel(o_ref, w_ref, out_ref, *, n_heads):
    for h in range(n_heads):
        out_ref[:, h * HEAD_SLOT:(h + 1) * HEAD_SLOT] = jnp.dot(
            o_ref[h].astype(BF16), w_ref[h], preferred_element_type=F32).astype(BF16)


def _olat(o_lat_heads, w_uv_pad):
    n_heads, n, _ = o_lat_heads.shape
    return pl.pallas_call(
        functools.partial(_olat_kernel, n_heads=n_heads),
        out_shape=jax.ShapeDtypeStruct((n, n_heads * HEAD_SLOT), BF16),
        compiler_params=pltpu.CompilerParams(vmem_limit_bytes=VMEM_LIMIT),
        name="o_latent",
    )(o_lat_heads, w_uv_pad)


def _out_proj_kernel(x_ref, c_ref, o_ref, wc_ref, wo_ref, g_ref, x1_ref, h_ref):
    mix = (jnp.dot(c_ref[...], wc_ref[...], preferred_element_type=F32)
           + jnp.dot(o_ref[...], wo_ref[...], preferred_element_type=F32))
    x1 = x_ref[...] + mix
    x1_ref[...] = x1
    h_ref[...] = _rms(x1, g_ref[...]).astype(BF16)


def _out_proj(x, c, o_slots, w_c, w_o, g, *, tm):
    n, d = x.shape
    row = lambda i: (i, 0)
    full = lambda i: (0, 0)
    return pl.pallas_call(
        _out_proj_kernel,
        grid=(n // tm,),
        in_specs=[
            pl.BlockSpec((tm, d), row),
            pl.BlockSpec((tm, c.shape[1]), row),
            pl.BlockSpec((tm, o_slots.shape[1]), row),
            pl.BlockSpec(w_c.shape, full),
            pl.BlockSpec(w_o.shape, full),
            pl.BlockSpec((1, d), full),
        ],
        out_specs=[pl.BlockSpec((tm, d), row), pl.BlockSpec((tm, d), row)],
        out_shape=[jax.ShapeDtypeStruct((n, d), F32), jax.ShapeDtypeStruct((n, d), BF16)],
        compiler_params=_cparams("parallel"),
        name="out_proj",
    )(x, c, o_slots, w_c, w_o, g)


def _swiglu_rows(x, wg, wu, wd):
    a = jnp.dot(x, wg, preferred_element_type=F32)
    b = jnp.dot(x, wu, preferred_element_type=F32)
    mid = (a * jax.nn.sigmoid(a) * b).astype(BF16)
    return jnp.dot(mid, wd, preferred_element_type=F32)


def _ffn_kernel(x_ref, res_ref, wg_hbm, wu_hbm, wd_hbm, o_ref, wg_ref, wu_ref, wd_ref, sem, *, row_splits):
    @pl.when(pl.program_id(0) == 0)
    def _():
        copies = [pltpu.make_async_copy(src, dst, sem.at[j]) for j, (src, dst) in enumerate(
            ((wg_hbm, wg_ref), (wu_hbm, wu_ref), (wd_hbm, wd_ref)))]
        for cp in copies:
            cp.start()
        for cp in copies:
            cp.wait()

    hm = o_ref.shape[0] // row_splits
    for r in range(row_splits):
        rows = slice(r * hm, (r + 1) * hm)
        o_ref[rows, :] = res_ref[rows, :] + _swiglu_rows(x_ref[rows, :], wg_ref[...], wu_ref[...], wd_ref[...])


def _ffn(x, res, wg, wu, wd, *, tm):
    n, d = x.shape
    d_ff = wg.shape[1]
    row = lambda i: (i, 0)
    hbm = pl.BlockSpec(memory_space=pl.ANY)
    return pl.pallas_call(
        functools.partial(_ffn_kernel, row_splits=2 if tm % 32 == 0 else 1),
        grid=(n // tm,),
        in_specs=[pl.BlockSpec((tm, d), row), pl.BlockSpec((tm, d), row), hbm, hbm, hbm],
        out_specs=pl.BlockSpec((tm, d), row),
        out_shape=jax.ShapeDtypeStruct((n, d), F32),
        scratch_shapes=[pltpu.VMEM((d, d_ff), BF16), pltpu.VMEM((d, d_ff), BF16), pltpu.VMEM((d_ff, d), BF16),
                        pltpu.SemaphoreType.DMA((3,))],
        compiler_params=_cparams("arbitrary"),
        name="ffn_dense",
    )(x, res, wg, wu, wd)


def _swiglu_kernel(te_ref, nv_ref, tr_ref, x_ref, wg_ref, wu_ref, wd_ref, o_ref, xb_ref):
    m = pl.program_id(0)
    f = pl.program_id(1)
    tm = o_ref.shape[0]
    hm = tm // 2

    @pl.when(f == 0)
    def _():
        o_ref[...] = jnp.zeros_like(o_ref)

    def accumulate(row_groups):
        wg = wg_ref[...].astype(BF16)
        wu = wu_ref[...].astype(BF16)
        wd = wd_ref[...].astype(BF16)
        for r in row_groups:
            rows = slice(r * hm, (r + 1) * hm)
            o_ref[rows, :] += _swiglu_rows(xb_ref[rows, :], wg, wu, wd)

    valid = m < nv_ref[0]
    both = tr_ref[jnp.minimum(m, nv_ref[0] - 1)] > hm

    @pl.when(valid & (f == 0))
    def _():
        xb_ref[...] = x_ref[...].astype(BF16)

    @pl.when(valid & both)
    def _():
        accumulate((0, 1))

    @pl.when(valid & jnp.logical_not(both))
    def _():
        accumulate((0,))


def _swiglu(tile_expert, n_valid, tile_rows, x, wg, wu, wd, *, tm, tf):
    p, d = x.shape
    d_ff = wg.shape[2]
    n_f = d_ff // tf
    n_tiles = p // tm

    def mm(m, nv):
        return jnp.minimum(m, nv[0] - 1)

    def ff(m, f, nv):
        return jnp.where(m < nv[0], f, n_f - 1)

    return pl.pallas_call(
        _swiglu_kernel,
        grid_spec=pltpu.PrefetchScalarGridSpec(
            num_scalar_prefetch=3,
            grid=(n_tiles, n_f),
            in_specs=[
                pl.BlockSpec((tm, d), lambda m, f, te, nv, tr: (mm(m, nv), 0)),
                pl.BlockSpec((None, d, tf), lambda m, f, te, nv, tr: (te[mm(m, nv)], 0, ff(m, f, nv))),
                pl.BlockSpec((None, d, tf), lambda m, f, te, nv, tr: (te[mm(m, nv)], 0, ff(m, f, nv))),
                pl.BlockSpec((None, tf, d), lambda m, f, te, nv, tr: (te[mm(m, nv)], ff(m, f, nv), 0)),
            ],
            out_specs=pl.BlockSpec((tm, d), lambda m, f, te, nv, tr: (m, 0)),
            scratch_shapes=[pltpu.VMEM((tm, d), BF16)],
        ),
        out_shape=jax.ShapeDtypeStruct((p, d), F32),
        compiler_params=_cparams("arbitrary", "arbitrary"),
        name="swiglu_experts",
    )(tile_expert, n_valid, tile_rows, x, wg, wu, wd)


def _row_copy(src, src_row, dst, dst_row, sem):
    return pltpu.make_async_copy(src.at[pl.ds(src_row, 1)], dst.at[pl.ds(dst_row, 1)], sem)


def _dispatch_kernel(pos_ref, h_ref, xs_in_ref, xs_ref, sem, *, tt):
    del xs_in_ref

    def body(r, carry):
        for k in range(TOP_K):
            _row_copy(h_ref, r, xs_ref, pos_ref[0, TOP_K * r + k], sem).start(priority=k % 2)
        return carry

    lax.fori_loop(0, tt, body, 0, unroll=8)
    for k in range(TOP_K):
        pltpu.make_async_copy(h_ref, xs_ref.at[pl.ds(0, tt)], sem).wait()


def _dispatch(pos_tiles, h, xs):
    n, d = h.shape
    n_tiles = pos_tiles.shape[0]
    tt = n // n_tiles
    return pl.pallas_call(
        functools.partial(_dispatch_kernel, tt=tt),
        grid=(n_tiles,),
        in_specs=[
            pl.BlockSpec((None, 1, TOP_K * tt), lambda i: (i, 0, 0), memory_space=pltpu.SMEM),
            pl.BlockSpec((tt, d), lambda i: (i, 0)),
            pl.BlockSpec(memory_space=pl.ANY),
        ],
        out_specs=pl.BlockSpec(memory_space=pl.ANY),
        out_shape=jax.ShapeDtypeStruct(xs.shape, xs.dtype),
        scratch_shapes=[pltpu.SemaphoreType.DMA],
        input_output_aliases={2: 0},
        compiler_params=_cparams("arbitrary"),
        name="moe_dispatch",
    )(pos_tiles, h, xs)


def _pool_kernel(*refs, tm, pos0, windows, has_halo, n_experts):
    if has_halo:
        (prev_ref, halo_ref, x_ref, g1_ref, wp_ref, sc_ref, g2_ref, wr_ref,
         x3_ref, h_ref, hn_ref, gate_ref, idx_ref, ext_ref, lv_ref) = refs
    else:
        (prev_ref, x_ref, g1_ref, wp_ref, sc_ref, g2_ref, wr_ref,
         x3_ref, h_ref, hn_ref, gate_ref, idx_ref, ext_ref, lv_ref) = refs
    t = pl.program_id(1)
    x = x_ref[0]
    hn = _rms(x, g1_ref[...])
    hn_ref[0] = hn
    top = SUBLANES
    base = top + POOL_HALO
    end = base + tm
    d = x.shape[1]
    ext_ref[0:top, :] = jnp.zeros((top, d), F32)
    if has_halo:
        @pl.when(t == 0)
        def _():
            ext_ref[top:base, :] = prev_ref[0]

        @pl.when(t > 0)
        def _():
            ext_ref[top:base, :] = _rms(halo_ref[0], g1_ref[...])
    else:
        ext_ref[top:base, :] = prev_ref[0]
    ext_ref[base:end, :] = hn

    gc = d // len(windows)

    def level_rows(k, lo_row, hi_row, lanes):
        if k == 0:
            return ext_ref[lo_row:hi_row, lanes]
        return lv_ref[k - 1, lo_row:hi_row, lanes]

    for k in range(1, len(windows)):
        lanes = slice((k - 1) * gc, d)
        shift = 1 << (k - 1)
        lv_ref[k - 1, 0:top, lanes] = jnp.zeros((top, d - (k - 1) * gc), F32)
        lv_ref[k - 1, top:end, lanes] = (level_rows(k - 1, top, end, lanes)
                                         + level_rows(k - 1, top - shift, end - shift, lanes))

    pos = pos0 + t * tm + lax.broadcasted_iota(jnp.int32, (tm, 1), 0)
    for gi, w in enumerate(windows):
        lanes = slice(gi * gc, (gi + 1) * gc)
        if gi + 1 < len(windows):
            s = level_rows(gi + 1, base, end, lanes)
        else:
            s = level_rows(gi, base, end, lanes) + level_rows(gi, base - w // 2, end - w // 2, lanes)
        cnt = jnp.minimum(pos + 1, w).astype(F32)
        diff = s / cnt - hn[:, lanes]
        y = jnp.dot(diff.astype(BF16), wp_ref[gi], preferred_element_type=F32)
        x3_ref[0, :, lanes] = x[:, lanes] + y * sc_ref[:, lanes]

    x3 = x3_ref[0]
    h2 = _rms(x3, g2_ref[...])
    h_ref[0] = h2
    logits = jnp.dot(h2, wr_ref[...], preferred_element_type=F32, precision=lax.Precision.HIGHEST)
    lane = lax.broadcasted_iota(jnp.int32, logits.shape, 1)
    logits = jnp.where(lane < n_experts, logits, -jnp.inf)
    m1 = jnp.max(logits, axis=-1, keepdims=True)
    i1 = jnp.min(jnp.where(logits == m1, lane, LANES), axis=-1, keepdims=True)
    rest = jnp.where(lane == i1, -jnp.inf, logits)
    m2 = jnp.max(rest, axis=-1, keepdims=True)
    i2 = jnp.min(jnp.where(rest == m2, lane, LANES), axis=-1, keepdims=True)
    e = jnp.exp(m2 - m1)
    g1 = 1.0 / (1.0 + e)
    g2 = e / (1.0 + e)
    gate_ref[0] = jnp.where(lane == 0, g1, jnp.where(lane == 1, g2, 0.0))
    idx_ref[0] = jnp.where(lane == 0, i1, jnp.where(lane == 1, i2, 0))


def _pool_layer(prev_pad, x, g1, w_pool, scale, g2, w_router_pad, *, tm, pos0, n_experts):
    bsz, t_len, d = x.shape
    assert all(w == 2 ** (g + 1) for g, w in enumerate(POOL_WINDOWS)) and POOL_HALO >= max(POOL_WINDOWS) - 1
    n_t = t_len // tm
    has_halo = n_t > 1
    hb = tm // POOL_HALO if has_halo else 1
    n_rows = SUBLANES + POOL_HALO + tm
    cur = lambda b, t: (b, t, 0)
    full2 = lambda b, t: (0, 0)
    in_specs = [pl.BlockSpec((1, POOL_HALO, d), lambda b, t: (b, 0, 0))]
    args = [prev_pad]
    if has_halo:
        in_specs.append(pl.BlockSpec((1, POOL_HALO, d), lambda b, t: (b, jnp.maximum(t * hb - 1, 0), 0)))
        args.append(x)
    in_specs += [
        pl.BlockSpec((1, tm, d), cur),
        pl.BlockSpec((1, d), full2),
        pl.BlockSpec(w_pool.shape, lambda b, t: (0, 0, 0)),
        pl.BlockSpec((1, d), full2),
        pl.BlockSpec((1, d), full2),
        pl.BlockSpec(w_router_pad.shape, full2),
    ]
    args += [x, g1, w_pool, scale, g2, w_router_pad]
    return pl.pallas_call(
        functools.partial(_pool_kernel, tm=tm, pos0=pos0, windows=POOL_WINDOWS, has_halo=has_halo,
                          n_experts=n_experts),
        grid=(bsz, n_t),
        in_specs=in_specs,
        out_specs=[
            pl.BlockSpec((1, tm, d), cur),
            pl.BlockSpec((1, tm, d), cur),
            pl.BlockSpec((1, tm, d), cur),
            pl.BlockSpec((1, tm, LANES), cur),
            pl.BlockSpec((1, tm, LANES), cur),
        ],
        out_shape=[
            jax.ShapeDtypeStruct((bsz, t_len, d), F32),
            jax.ShapeDtypeStruct((bsz, t_len, d), F32),
            jax.ShapeDtypeStruct((bsz, t_len, d), F32),
            jax.ShapeDtypeStruct((bsz, t_len, LANES), F32),
            jax.ShapeDtypeStruct((bsz, t_len, LANES), jnp.int32),
        ],
        scratch_shapes=[pltpu.VMEM((n_rows, d), F32), pltpu.VMEM((len(POOL_WINDOWS) - 1, n_rows, d), F32)],
        compiler_params=_cparams("parallel", "parallel"),
        name="pool_layer",
    )(*args)


def _combine_kernel(pos_cur_ref, pos_nxt_ref, x_ref, gate_ref, g_ref, ys_ref, y_ref, buf_ref, sem,
                    *, tt, apply_norm):
    i = pl.program_id(0)
    n = pl.num_programs(0)
    slot = i % 2

    def fetch(pos_ref, s):
        def body(r, carry):
            for k in range(TOP_K):
                _row_copy(ys_ref, pos_ref[0, TOP_K * r + k], buf_ref.at[s, k], r,
                          sem.at[s]).start(priority=k % 2)
            return carry
        lax.fori_loop(0, tt, body, 0, unroll=8)

    @pl.when(i == 0)
    def _():
        fetch(pos_cur_ref, 0)

    @pl.when(i + 1 < n)
    def _():
        fetch(pos_nxt_ref, 1 - slot)

    for k in range(TOP_K):
        pltpu.make_async_copy(ys_ref.at[pl.ds(0, tt)], buf_ref.at[slot, k], sem.at[slot]).wait()
    moe = gate_ref[:, 0:1] * buf_ref[slot, 0]
    for k in range(1, TOP_K):
        moe = moe + gate_ref[:, k:k + 1] * buf_ref[slot, k]
    y = x_ref[...] + moe
    y_ref[...] = _rms(y, g_ref[...]) if apply_norm else y


def _combine(pos_tiles, x, gates, g, ys, *, apply_norm):
    n, d = x.shape
    n_tiles = pos_tiles.shape[0]
    tt = n // n_tiles
    pos_spec = lambda fn: pl.BlockSpec((None, 1, TOP_K * tt), fn, memory_space=pltpu.SMEM)
    return pl.pallas_call(
        functools.partial(_combine_kernel, tt=tt, apply_norm=apply_norm),
        grid=(n_tiles,),
        in_specs=[
            pos_spec(lambda i: (i, 0, 0)),
            pos_spec(lambda i: (jnp.minimum(i + 1, n_tiles - 1), 0, 0)),
            pl.BlockSpec((tt, d), lambda i: (i, 0)),
            pl.BlockSpec((tt, LANES), lambda i: (i, 0)),
            pl.BlockSpec((1, d), lambda i: (0, 0)),
            pl.BlockSpec(memory_space=pl.ANY),
        ],
        out_specs=pl.BlockSpec((tt, d), lambda i: (i, 0)),
        out_shape=jax.ShapeDtypeStruct((n, d), F32),
        scratch_shapes=[pltpu.VMEM((2, TOP_K, tt, d), F32), pltpu.SemaphoreType.DMA((2,))],
        compiler_params=_cparams("arbitrary"),
        name="moe_combine",
    )(pos_tiles, pos_tiles, x, gates, g, ys)


def _final_kernel(x_ref, g_ref, y_ref):
    y_ref[...] = _rms(x_ref[...], g_ref[...])


def _final(x, g, *, tm):
    n, d = x.shape
    row = lambda i: (i, 0)
    return pl.pallas_call(
        _final_kernel,
        grid=(n // tm,),
        in_specs=[pl.BlockSpec((tm, d), row), pl.BlockSpec((1, d), lambda i: (0, 0))],
        out_specs=pl.BlockSpec((tm, d), row),
        out_shape=jax.ShapeDtypeStruct((n, d), F32),
        compiler_params=_cparams("parallel"),
        name="final_norm",
    )(x, g)


def _swap_halves(w):
    half = w.shape[-1] // 2
    return jnp.concatenate([w[..., half:], w[..., :half]], axis=-1)


def _slot(parts, total=HEAD_SLOT):
    used = sum(p.shape[-1] for p in parts)
    pad = jnp.zeros(parts[0].shape[:-1] + (total - used,), parts[0].dtype)
    return jnp.concatenate(list(parts) + [pad], axis=-1)


def _prep_even_weights(w_in, w_uq, w_ukv, w_out, *, dc, qr, kvr):
    d = w_in.shape[0]
    body = w_in[:, :2 * dc + qr + kvr]
    w_kpe = w_in[:, 2 * dc + qr + kvr:]
    z_nope = jnp.zeros((d, NOPE_DIM), w_in.dtype)
    w_in_ext = jnp.concatenate(
        [body, _slot([z_nope, w_kpe]), _slot([z_nope, _swap_halves(w_kpe)])], axis=-1).astype(BF16)

    uq = w_uq.reshape(qr, N_HEADS, NOPE_DIM + ROPE_DIM)
    uq_nope, uq_pe = uq[..., :NOPE_DIM], uq[..., NOPE_DIM:]
    qa = _slot([uq_nope, uq_pe]).reshape(qr, N_HEADS * HEAD_SLOT)
    qb = _slot([jnp.zeros_like(uq_nope), _swap_halves(uq_pe)]).reshape(qr, N_HEADS * HEAD_SLOT)
    w_uq_ext = jnp.concatenate([qa, qb], axis=-1).astype(BF16)

    ukv = w_ukv.reshape(kvr, N_HEADS, NOPE_DIM + V_DIM)
    uk, uv = ukv[..., :NOPE_DIM], ukv[..., NOPE_DIM:]
    w_kv_ext = jnp.concatenate([_slot([uk]).reshape(kvr, N_HEADS * HEAD_SLOT),
                                _slot([uv]).reshape(kvr, N_HEADS * HEAD_SLOT)], axis=-1).astype(BF16)
    w_q2lat = jnp.concatenate(
        [jnp.transpose(uk, (1, 2, 0)),
         jnp.zeros((N_HEADS, HEAD_SLOT - NOPE_DIM, kvr), w_ukv.dtype)], axis=1).astype(BF16)
    w_uv_pad = _slot([jnp.transpose(uv, (1, 0, 2))]).astype(BF16)

    w_c = w_out[:dc].astype(BF16)
    wo = w_out[dc:].reshape(N_HEADS, V_DIM, d)
    w_o = jnp.concatenate([wo, jnp.zeros((N_HEADS, HEAD_SLOT - V_DIM, d), w_out.dtype)],
                          axis=1).reshape(N_HEADS * HEAD_SLOT, d).astype(BF16)
    return w_in_ext, w_uq_ext, w_kv_ext, w_q2lat, w_uv_pad, w_c, w_o


def _rope_tables(pos):
    half = ROPE_DIM // 2
    inv_freq = jnp.power(ROPE_THETA, -jnp.arange(half, dtype=F32) / half)
    ang = pos.astype(F32)[:, None] * inv_freq[None, :]
    cos, sin = jnp.cos(ang), jnp.sin(ang)
    n = pos.shape[0]
    z_nope = jnp.zeros((n, NOPE_DIM), F32)
    cos_k = _slot([z_nope, cos, cos])
    sin_k = _slot([z_nope, -sin, sin])
    cos_q = _slot([jnp.ones((n, NOPE_DIM), F32), cos, cos]) * (ATTN_SCALE * LOG2E)
    sin_q = sin_k * (ATTN_SCALE * LOG2E)
    return cos_k, sin_k, cos_q, sin_q


def _even_layer(x, conv_prev, pos, attend, P, layer, i):
    bsz, t_len, d = x.shape
    n = bsz * t_len
    dc = P['conv_w'].shape[2]
    qr = P['q_norm'].shape[1]
    kvr = P['kv_norm'].shape[1]
    k_taps = P['conv_w'].shape[1]
    tm = _row_tile(n, 512) if t_len >= 512 else n
    w_in_ext, w_uq_ext, w_kv_ext, w_q2lat, w_uv_pad, w_c, w_o = _prep_even_weights(
        P['w_in'][i], P['w_uq'][i], P['w_ukv'][i], P['w_out'][i], dc=dc, qr=qr, kvr=kvr)
    cos_k, sin_k, cos_q, sin_q = _rope_tables(pos)
    if t_len < tm:
        reps = tm // t_len
        cos_k, sin_k, cos_q, sin_q = (jnp.tile(a, (reps, 1)) for a in (cos_k, sin_k, cos_q, sin_q))

    xf = x.reshape(n, d)
    u, qn, ckv, kpe_slot = _in_proj(
        xf, P['norm_mix'][layer][None], w_in_ext, P['q_norm'][i][None], P['kv_norm'][i][None],
        cos_k, sin_k, dc=dc, qr=qr, kvr=kvr, tm=tm)

    u3 = u.reshape(bsz, t_len, dc)
    prev_pad = jnp.pad(conv_prev.astype(F32), ((0, 0), (CONV_HALO - (k_taps - 1), 0), (0, 0)))
    c = _conv_branch(prev_pad, u3, P['conv_w'][i], P['conv_b'][i][None], P['conv_ln_g'][i][None],
                     P['conv_ln_b'][i][None], tc=min(t_len, 512))
    conv_state = jnp.concatenate([conv_prev.astype(F32), u3], axis=1)[:, -(k_taps - 1):]

    q_slots = _q_proj(qn, w_uq_ext, cos_q, sin_q, tm=tm)
    o_slots = attend(q_slots, ckv, kpe_slot, w_kv_ext, w_q2lat, w_uv_pad, tm)

    x1, h_ffn = _out_proj(xf, c.reshape(n, dc), o_slots, w_c, w_o, P['norm_ffn'][layer][None], tm=tm)

    x2 = _ffn(h_ffn, x1, P['ffn_w_gate'][i].astype(BF16), P['ffn_w_up'][i].astype(BF16),
              P['ffn_w_down'][i].astype(BF16), tm=tm)
    kpe = kpe_slot[:, NOPE_DIM:NOPE_DIM + ROPE_DIM]
    return (x2.reshape(bsz, t_len, d), conv_state, ckv.reshape(bsz, t_len, kvr),
            kpe.reshape(bsz, t_len, ROPE_DIM))


def _make_attend_prompt(bsz, t_len):
    def attend(q_slots, ckv, kpe_slot, w_kv_ext, w_q2lat, w_uv_pad, tm):
        k_slots, v_slots = _kv_proj(ckv, kpe_slot, w_kv_ext, tm=tm)
        return _attn_prompt(q_slots, k_slots, v_slots, bsz=bsz, t_len=t_len, tq=min(t_len, 512))
    return attend


def _make_attend_sample(bsz, t_len, cache_ckv, cache_kpe_t, page_table, i):
    t_pad = 8

    def attend(q_slots, ckv, kpe_slot, w_kv_ext, w_q2lat, w_uv_pad, tm):
        n = bsz * t_len
        kvr = ckv.shape[1]
        q_lat = _qlat(q_slots, w_q2lat)
        q_lat = q_lat.reshape(N_HEADS, bsz, t_len, kvr)
        q_lat = jnp.pad(q_lat, ((0, 0), (0, 0), (0, t_pad - t_len), (0, 0)))
        q_lat = jnp.transpose(q_lat, (1, 0, 2, 3)).reshape(bsz, N_HEADS * t_pad, kvr)
        q_pe = q_slots.reshape(bsz, t_len, N_HEADS, HEAD_SLOT)[..., NOPE_DIM:NOPE_DIM + ROPE_DIM]
        q_pe = jnp.pad(jnp.transpose(q_pe, (0, 2, 1, 3)), ((0, 0), (0, 0), (0, t_pad - t_len), (0, 0)))
        q_pe = q_pe.reshape(bsz, N_HEADS * t_pad, ROPE_DIM)
        pad_t = ((0, 0), (0, t_pad - t_len), (0, 0))
        ckv_new = jnp.pad(ckv.reshape(bsz, t_len, kvr), pad_t)
        kpe_new = jnp.pad(kpe_slot[:, NOPE_DIM:NOPE_DIM + ROPE_DIM].reshape(bsz, t_len, ROPE_DIM), pad_t)
        o_lat = _attn_sample(page_table, q_lat, q_pe, ckv_new, kpe_new, cache_ckv, cache_kpe_t, i,
                             t_new=t_len, t_pad=t_pad, pages_per_step=32,
                             pages_per_chunk=32)
        o_lat = o_lat.reshape(bsz, N_HEADS, t_pad, kvr)[:, :, :t_len]
        o_lat = jnp.transpose(o_lat, (1, 0, 2, 3)).reshape(N_HEADS, n, kvr)
        return _olat(o_lat, w_uv_pad)
    return attend


def _odd_layer_mixer(x, pool_prev, pos0, P, layer, i):
    bsz, t_len, d = x.shape
    n_experts = P['router_w'].shape[2]
    ctx = max(POOL_WINDOWS) - 1
    prev_pad = jnp.pad(pool_prev.astype(F32), ((0, 0), (POOL_HALO - ctx, 0), (0, 0)))
    w_router_pad = jnp.pad(P['router_w'][i], ((0, 0), (0, LANES - n_experts)))
    x3, h_moe, hn, gates, ids = _pool_layer(
        prev_pad, x, P['norm_mix'][layer][None], P['pool_w'][i].astype(BF16), P['pool_scale'][i][None],
        P['norm_ffn'][layer][None], w_router_pad, tm=min(t_len, 512), pos0=pos0, n_experts=n_experts)
    pool_state = jnp.concatenate([pool_prev.astype(F32), hn], axis=1)[:, -ctx:]
    n = bsz * t_len
    return (x3.reshape(n, d), h_moe.reshape(n, d), pool_state, gates.reshape(n, LANES),
            ids.reshape(n, LANES)[:, :TOP_K])


def _moe_plan(ids, n_experts, tm):
    n = ids.shape[0]
    flat = ids.reshape(-1)
    experts = jnp.arange(n_experts, dtype=jnp.int32)
    onehot = (flat[:, None] == experts[None, :]).astype(jnp.int32)
    csum = jnp.cumsum(onehot, axis=0)
    counts = csum[-1]
    rank = jnp.sum(csum * onehot, axis=1) - 1
    padded = ((counts + tm - 1) // tm) * tm
    ends = jnp.cumsum(padded)
    starts = ends - padded
    pos = jnp.sum(starts[None, :] * onehot, axis=1) + rank
    n_tiles = (n * TOP_K + tm - 1) // tm + n_experts
    n_valid = (ends[-1] // tm).astype(jnp.int32)
    tile_start = jnp.arange(n_tiles, dtype=jnp.int32) * tm
    tile_expert = jnp.sum((ends[None, :] <= tile_start[:, None]).astype(jnp.int32), axis=1)
    tile_expert = jnp.minimum(tile_expert, n_experts - 1)
    group_end = starts + counts
    tile_onehot = (tile_expert[:, None] == experts[None, :]).astype(jnp.int32)
    tile_rows = jnp.clip(jnp.sum(group_end[None, :] * tile_onehot, axis=1) - tile_start, 0, tm)
    return pos.reshape(n, TOP_K), tile_expert, n_valid.reshape(1), tile_rows.astype(jnp.int32), n_tiles


def _moe_layer(groups, P, i, g_final, *, tm, tt):
    d = groups[0][0].shape[1]
    n_experts = P['router_w'].shape[2]
    ids = jnp.concatenate([g[3] for g in groups], axis=0)
    pos, tile_expert, n_valid, tile_rows, n_tiles = _moe_plan(ids, n_experts, tm)
    xs = jnp.zeros((n_tiles * tm, d), F32)
    pos_tiles = []
    lo = 0
    for x3, h, _, _ in groups:
        n = x3.shape[0]
        t = min(tt, n)
        pt = pos[lo:lo + n].reshape(n // t, 1, TOP_K * t)
        xs = _dispatch(pt, h, xs)
        pos_tiles.append(pt)
        lo += n
    ys = _swiglu(tile_expert, n_valid, tile_rows, xs, P['moe_w_gate'][i], P['moe_w_up'][i],
                 P['moe_w_down'][i], tm=tm, tf=512)
    gain = g_final if g_final is not None else jnp.ones((1, d), F32)
    return [_combine(pt, x3, gates, gain, ys, apply_norm=g_final is not None)
            for pt, (x3, _, gates, _) in zip(pos_tiles, groups)]


def kernel(x_prompt, x_sample, cache_ckv, cache_kpe, page_table, state_conv, state_pool, norm_mix, norm_ffn,
           norm_final, w_in, conv_w, conv_b, conv_ln_g, conv_ln_b, q_norm, w_uq, kv_norm, w_ukv, w_out,
           ffn_w_gate, ffn_w_up, ffn_w_down, pool_w, pool_scale, router_w, moe_w_gate, moe_w_up, moe_w_down):
    P = {'norm_mix': norm_mix, 'norm_ffn': norm_ffn, 'norm_final': norm_final, 'w_in': w_in,
         'conv_w': conv_w, 'conv_b': conv_b, 'conv_ln_g': conv_ln_g, 'conv_ln_b': conv_ln_b,
         'q_norm': q_norm, 'w_uq': w_uq, 'kv_norm': kv_norm, 'w_ukv': w_ukv, 'w_out': w_out,
         'ffn_w_gate': ffn_w_gate, 'ffn_w_up': ffn_w_up, 'ffn_w_down': ffn_w_down,
         'pool_w': pool_w, 'pool_scale': pool_scale, 'router_w': router_w,
         'moe_w_gate': moe_w_gate, 'moe_w_up': moe_w_up, 'moe_w_down': moe_w_down}
    depth = norm_mix.shape[0]
    d = x_prompt.shape[2]
    bp, tp, _ = x_prompt.shape
    bs, ts, _ = x_sample.shape
    past = page_table.shape[1] * cache_ckv.shape[2]
    k_taps = conv_w.shape[1]
    ctx = max(POOL_WINDOWS) - 1
    pos_p = jnp.arange(tp, dtype=jnp.int32)
    pos_s = past + jnp.arange(ts, dtype=jnp.int32)
    cache_kpe_t = jnp.swapaxes(cache_kpe, 2, 3)

    xp, xs = x_prompt, x_sample
    outs_p = {'ckv': [], 'kpe': [], 'conv': [], 'pool': []}
    outs_s = {'ckv': [], 'kpe': [], 'conv': [], 'pool': []}
    for layer in range(depth):
        i = layer // 2
        last = layer == depth - 1
        if layer % 2 == 0:
            conv0 = jnp.zeros((bp, k_taps - 1, conv_w.shape[2]), F32)
            xp, cst, ckv, kpe = _even_layer(xp, conv0, pos_p, _make_attend_prompt(bp, tp), P, layer, i)
            outs_p['conv'].append(cst); outs_p['ckv'].append(ckv); outs_p['kpe'].append(kpe)
            xs, cst, ckv, kpe = _even_layer(
                xs, state_conv[i], pos_s, _make_attend_sample(bs, ts, cache_ckv, cache_kpe_t, page_table, i),
                P, layer, i)
            outs_s['conv'].append(cst); outs_s['ckv'].append(ckv); outs_s['kpe'].append(kpe)
            if last:
                xp = _final(xp.reshape(bp * tp, d), norm_final[None], tm=_row_tile(bp * tp, 512)).reshape(xp.shape)
                xs = _final(xs.reshape(bs * ts, d), norm_final[None], tm=_row_tile(bs * ts, 512)).reshape(xs.shape)
        else:
            pool0 = jnp.zeros((bp, ctx, d), F32)
            x3p, hp, pst_p, gp, ip = _odd_layer_mixer(xp, pool0, 0, P, layer, i)
            x3s, hs, pst_s, gs, is_ = _odd_layer_mixer(xs, state_pool[i], past, P, layer, i)
            outs_p['pool'].append(pst_p); outs_s['pool'].append(pst_s)
            yp, ysm = _moe_layer([(x3p, hp, gp, ip), (x3s, hs, gs, is_)], P, i,
                                 norm_final[None] if last else None, tm=1024, tt=512)
            xp, xs = yp.reshape(xp.shape), ysm.reshape(xs.shape)

    def stack(lst, shape_if_empty):
        return jnp.stack(lst) if lst else jnp.zeros(shape_if_empty, F32)

    return (xp, xs,
            jnp.stack(outs_p['ckv']), jnp.stack(outs_p['kpe']), jnp.stack(outs_p['conv']),
            stack(outs_p['pool'], (0, bp, ctx, d)),
            jnp.stack(outs_s['ckv']), jnp.stack(outs_s['kpe']), jnp.stack(outs_s['conv']),
            stack(outs_s['pool'], (0, bs, ctx, d)))
```

```python
import functools

import jax
import jax.numpy as jnp
from jax import lax
from jax.experimental import pallas as pl
from jax.experimental.pallas import tpu as pltpu

F32 = jnp.float32
BF16 = jnp.bfloat16

N_HEADS = 8
NOPE_DIM = 64
ROPE_DIM = 32
V_DIM = 64
ROPE_THETA = 10000.0
ATTN_SCALE = (NOPE_DIM + ROPE_DIM) ** -0.5
LOG2E = 1.4426950408889634
POOL_WINDOWS = (2, 4, 8, 16)
TOP_K = 2
EPS = 1e-6

HEAD_SLOT = 128
LANES = 128
SUBLANES = 8
NEG_BIG = -1e30
VMEM_LIMIT = 56 * 1024 * 1024

CONV_HALO = 32
POOL_HALO = 16


def _cparams(*semantics):
    return pltpu.CompilerParams(dimension_semantics=semantics, vmem_limit_bytes=VMEM_LIMIT)


def _rms(x, g):
    return x * lax.rsqrt(jnp.mean(x * x, axis=-1, keepdims=True) + EPS) * g


def _row_tile(n, want):
    if n <= want:
        return n
    t = want
    while t >= 8:
        if n % t == 0 and t % 8 == 0:
            return t
        t -= 8
    return n


def _in_proj_kernel(x_ref, g_ref, w_ref, qg_ref, kvg_ref, cos_ref, sin_ref,
                    u_ref, qn_ref, ckv_ref, kpe_ref, *, dc, qr, kvr):
    h = _rms(x_ref[...], g_ref[...])
    proj = jnp.dot(h.astype(BF16), w_ref[...], preferred_element_type=F32)
    a = proj[:, :dc]
    gate = proj[:, dc:2 * dc]
    u_ref[...] = a * jax.nn.sigmoid(gate)
    o = 2 * dc
    qn_ref[...] = _rms(proj[:, o:o + qr], qg_ref[...]).astype(BF16)
    o += qr
    ckv_ref[...] = _rms(proj[:, o:o + kvr], kvg_ref[...])
    o += kvr
    kpe_ref[...] = (proj[:, o:o + HEAD_SLOT] * cos_ref[...]
                    + proj[:, o + HEAD_SLOT:o + 2 * HEAD_SLOT] * sin_ref[...])


def _in_proj(x, g, w_ext, qg, kvg, cos_k, sin_k, *, dc, qr, kvr, tm):
    n, d = x.shape
    n_tab = cos_k.shape[0] // tm
    row = lambda i: (i, 0)
    full = lambda i: (0, 0)
    tab = lambda i: (i % n_tab, 0)
    return pl.pallas_call(
        functools.partial(_in_proj_kernel, dc=dc, qr=qr, kvr=kvr),
        grid=(n // tm,),
        in_specs=[
            pl.BlockSpec((tm, d), row),
            pl.BlockSpec((1, d), full),
            pl.BlockSpec(w_ext.shape, full),
            pl.BlockSpec((1, qr), full),
            pl.BlockSpec((1, kvr), full),
            pl.BlockSpec((tm, HEAD_SLOT), tab),
            pl.BlockSpec((tm, HEAD_SLOT), tab),
        ],
        out_specs=[
            pl.BlockSpec((tm, dc), row),
            pl.BlockSpec((tm, qr), row),
            pl.BlockSpec((tm, kvr), row),
            pl.BlockSpec((tm, HEAD_SLOT), row),
        ],
        out_shape=[
            jax.ShapeDtypeStruct((n, dc), F32),
            jax.ShapeDtypeStruct((n, qr), BF16),
            jax.ShapeDtypeStruct((n, kvr), F32),
            jax.ShapeDtypeStruct((n, HEAD_SLOT), F32),
        ],
        compiler_params=_cparams("parallel"),
        name="in_proj",
    )(x, g, w_ext, qg, kvg, cos_k, sin_k)


def _conv_kernel(*refs, tc, k_taps, rows_per_chunk, has_halo):
    if has_halo:
        prev_ref, halo_ref, u_ref, w_ref, b_ref, lg_ref, lb_ref, c_ref, ext_ref, sh_ref = refs
    else:
        prev_ref, u_ref, w_ref, b_ref, lg_ref, lb_ref, c_ref, ext_ref, sh_ref = refs
    if has_halo:
        t = pl.program_id(1)

        @pl.when(t == 0)
        def _():
            ext_ref[0:CONV_HALO, :] = prev_ref[0]

        @pl.when(t > 0)
        def _():
            ext_ref[0:CONV_HALO, :] = halo_ref[0]
    else:
        ext_ref[0:CONV_HALO, :] = prev_ref[0]
    ext_ref[CONV_HALO:CONV_HALO + tc, :] = u_ref[0]

    n_rows = CONV_HALO + tc
    for s in range(1, SUBLANES):
        sh_ref[s - 1, 0:n_rows - SUBLANES, :] = ext_ref[s:s + n_rows - SUBLANES, :]

    def window(start, n):
        q, s = divmod(start, SUBLANES)
        if s == 0:
            return ext_ref[start:start + n, :]
        return sh_ref[s - 1, q * SUBLANES:q * SUBLANES + n, :]

    first = CONV_HALO - (k_taps - 1)
    for c0 in range(0, tc, rows_per_chunk):
        rc = min(rows_per_chunk, tc - c0)
        acc = w_ref[0:1, :] * window(first + c0, rc)
        for k in range(1, k_taps):
            acc = acc + w_ref[k:k + 1, :] * window(first + c0 + k, rc)
        acc = acc + b_ref[...]
        mu = jnp.mean(acc, axis=-1, keepdims=True)
        xc = acc - mu
        y = xc * lax.rsqrt(jnp.mean(xc * xc, axis=-1, keepdims=True) + EPS)
        y = y * lg_ref[...] + lb_ref[...]
        c_ref[0, c0:c0 + rc, :] = (y * jax.nn.sigmoid(y)).astype(BF16)


def _conv_branch(prev_pad, u, w, b, lg, lb, *, tc):
    bsz, t_len, c = u.shape
    k_taps = w.shape[0]
    n_t = t_len // tc
    has_halo = n_t > 1
    hb = tc // CONV_HALO if has_halo else 1
    in_specs = [pl.BlockSpec((1, CONV_HALO, c), lambda bi, ti: (bi, 0, 0))]
    args = [prev_pad]
    if has_halo:
        in_specs.append(pl.BlockSpec((1, CONV_HALO, c),
                                     lambda bi, ti: (bi, jnp.maximum(ti * hb - 1, 0), 0)))
        args.append(u)
    in_specs += [
        pl.BlockSpec((1, tc, c), lambda bi, ti: (bi, ti, 0)),
        pl.BlockSpec((k_taps, c), lambda bi, ti: (0, 0)),
        pl.BlockSpec((1, c), lambda bi, ti: (0, 0)),
        pl.BlockSpec((1, c), lambda bi, ti: (0, 0)),
        pl.BlockSpec((1, c), lambda bi, ti: (0, 0)),
    ]
    args += [u, w, b, lg, lb]
    return pl.pallas_call(
        functools.partial(_conv_kernel, tc=tc, k_taps=k_taps, rows_per_chunk=64, has_halo=has_halo),
        grid=(bsz, n_t),
        in_specs=in_specs,
        out_specs=pl.BlockSpec((1, tc, c), lambda bi, ti: (bi, ti, 0)),
        out_shape=jax.ShapeDtypeStruct((bsz, t_len, c), BF16),
        scratch_shapes=[pltpu.VMEM((CONV_HALO + tc, c), F32),
                        pltpu.VMEM((SUBLANES - 1, CONV_HALO + tc, c), F32)],
        compiler_params=_cparams("parallel", "parallel"),
        name="conv_branch",
    )(*args)


def _q_proj_kernel(qn_ref, w_ref, cos_ref, sin_ref, q_ref, *, n_heads):
    proj = jnp.dot(qn_ref[...], w_ref[...], preferred_element_type=F32)
    half = n_heads * HEAD_SLOT
    cos = cos_ref[...]
    sin = sin_ref[...]
    for h in range(n_heads):
        a = proj[:, h * HEAD_SLOT:(h + 1) * HEAD_SLOT]
        b = proj[:, half + h * HEAD_SLOT:half + (h + 1) * HEAD_SLOT]
        q_ref[:, h * HEAD_SLOT:(h + 1) * HEAD_SLOT] = (a * cos + b * sin).astype(BF16)


def _q_proj(qn, w_ext, cos_q, sin_q, *, tm):
    n, qr = qn.shape
    n_tab = cos_q.shape[0] // tm
    width = N_HEADS * HEAD_SLOT
    return pl.pallas_call(
        functools.partial(_q_proj_kernel, n_heads=N_HEADS),
        grid=(n // tm,),
        in_specs=[
            pl.BlockSpec((tm, qr), lambda i: (i, 0)),
            pl.BlockSpec(w_ext.shape, lambda i: (0, 0)),
            pl.BlockSpec((tm, HEAD_SLOT), lambda i: (i % n_tab, 0)),
            pl.BlockSpec((tm, HEAD_SLOT), lambda i: (i % n_tab, 0)),
        ],
        out_specs=pl.BlockSpec((tm, width), lambda i: (i, 0)),
        out_shape=jax.ShapeDtypeStruct((n, width), BF16),
        compiler_params=_cparams("parallel"),
        name="q_proj",
    )(qn, w_ext, cos_q, sin_q)


def _kv_proj_kernel(ckv_ref, kpe_ref, w_ref, k_ref, v_ref, *, n_heads):
    proj = jnp.dot(ckv_ref[...].astype(BF16), w_ref[...], preferred_element_type=F32)
    half = n_heads * HEAD_SLOT
    kpe = kpe_ref[...]
    for h in range(n_heads):
        k_ref[:, h * HEAD_SLOT:(h + 1) * HEAD_SLOT] = (
            proj[:, h * HEAD_SLOT:(h + 1) * HEAD_SLOT] + kpe).astype(BF16)
    v = proj[:, half:]
    lane = lax.broadcasted_iota(jnp.int32, v.shape, 1)
    v_ref[...] = jnp.where(lane % HEAD_SLOT == V_DIM, 1.0, v).astype(BF16)


def _kv_proj(ckv, kpe_slot, w_ext, *, tm):
    n, kvr = ckv.shape
    width = N_HEADS * HEAD_SLOT
    return pl.pallas_call(
        functools.partial(_kv_proj_kernel, n_heads=N_HEADS),
        grid=(n // tm,),
        in_specs=[
            pl.BlockSpec((tm, kvr), lambda i: (i, 0)),
            pl.BlockSpec((tm, HEAD_SLOT), lambda i: (i, 0)),
            pl.BlockSpec(w_ext.shape, lambda i: (0, 0)),
        ],
        out_specs=[pl.BlockSpec((tm, width), lambda i: (i, 0)),
                   pl.BlockSpec((tm, width), lambda i: (i, 0))],
        out_shape=[jax.ShapeDtypeStruct((n, width), BF16),
                   jax.ShapeDtypeStruct((n, width), BF16)],
        compiler_params=_cparams("parallel"),
        name="kv_proj",
    )(ckv, kpe_slot, w_ext)


def _attn_prompt_kernel(q_ref, k_ref, v_ref, o_ref, *, tq, n_heads):
    qi = pl.program_id(1)
    row = lax.broadcasted_iota(jnp.int32, (tq, tq), 0)
    col = lax.broadcasted_iota(jnp.int32, (tq, tq), 1)
    causal = col <= row

    head_lanes = [slice(h * HEAD_SLOT, (h + 1) * HEAD_SLOT) for h in range(n_heads)]

    def step(kt, carry, masked):
        ms, accs = carry
        start = pl.multiple_of(kt * tq, tq)
        new_ms, new_accs = [], []
        for h, lanes in enumerate(head_lanes):
            k = k_ref[pl.ds(start, tq), lanes]
            v = v_ref[pl.ds(start, tq), lanes]
            s = lax.dot_general(q_ref[:, lanes], k, (((1,), (1,)), ((), ())), preferred_element_type=F32)
            if masked:
                s = jnp.where(causal, s, NEG_BIG)
            m_new = jnp.maximum(ms[h], jnp.max(s, axis=-1, keepdims=True))
            p = jnp.exp2(s - m_new)
            new_accs.append(jnp.exp2(ms[h] - m_new) * accs[h]
                            + jnp.dot(p.astype(BF16), v, preferred_element_type=F32))
            new_ms.append(m_new)
        return tuple(new_ms), tuple(new_accs)

    init = (tuple(jnp.full((tq, 1), NEG_BIG, F32) for _ in head_lanes),
            tuple(jnp.zeros((tq, HEAD_SLOT), F32) for _ in head_lanes))
    carry = lax.fori_loop(0, qi, functools.partial(step, masked=False), init)
    _, accs = step(qi, carry, True)
    for lanes, acc in zip(head_lanes, accs):
        o_ref[:, lanes] = (acc / acc[:, V_DIM:V_DIM + 1]).astype(BF16)


def _attn_prompt(q, k, v, *, bsz, t_len, tq):
    width = q.shape[1]
    nq = t_len // tq
    return pl.pallas_call(
        functools.partial(_attn_prompt_kernel, tq=tq, n_heads=N_HEADS),
        grid=(bsz, nq),
        in_specs=[
            pl.BlockSpec((tq, width), lambda b, i: (b * nq + i, 0)),
            pl.BlockSpec((t_len, width), lambda b, i: (b, 0)),
            pl.BlockSpec((t_len, width), lambda b, i: (b, 0)),
        ],
        out_specs=pl.BlockSpec((tq, width), lambda b, i: (b * nq + i, 0)),
        out_shape=jax.ShapeDtypeStruct(q.shape, BF16),
        compiler_params=_cparams("parallel", "arbitrary"),
        name="attn_prompt",
    )(q, k, v)


def _qlat_kernel(q_ref, w_ref, o_ref, *, n_heads):
    for h in range(n_heads):
        o_ref[h] = jnp.dot(q_ref[:, h * HEAD_SLOT:(h + 1) * HEAD_SLOT], w_ref[h],
                           preferred_element_type=F32).astype(BF16)


def _qlat(q_slots, w_q2lat):
    n = q_slots.shape[0]
    n_heads, _, kvr = w_q2lat.shape
    return pl.pallas_call(
        functools.partial(_qlat_kernel, n_heads=n_heads),
        out_shape=jax.ShapeDtypeStruct((n_heads, n, kvr), BF16),
        compiler_params=pltpu.CompilerParams(vmem_limit_bytes=VMEM_LIMIT),
        name="q_latent",
    )(q_slots, w_q2lat)


def _attn_sample_kernel(pt_ref, ql_ref, qp_ref, cn_ref, kn_ref, ckv_hbm, kpe_hbm, o_ref,
                        cbuf_ref, pbuf_ref, kbf_ref, pbf_ref, m_ref, l_ref, acc_ref, sem,
                        *, layer, pages_per_step, pages_per_chunk, page, t_new, t_pad):
    g = pages_per_step
    bsz, n_pages = pt_ref.shape
    steps_per_seq = n_pages // g
    total = bsz * steps_per_seq

    def page_copies(i, slot):
        b = i // steps_per_seq
        first = (i % steps_per_seq) * g
        copies = []
        for j in range(g):
            pid = pt_ref[b, first + j]
            copies.append(pltpu.make_async_copy(
                ckv_hbm.at[layer, pid], cbuf_ref.at[slot, pl.ds(j * page, page)], sem.at[slot]))
            copies.append(pltpu.make_async_copy(
                kpe_hbm.at[layer, pid], pbuf_ref.at[slot, j], sem.at[slot]))
        return copies

    for cp in page_copies(0, 0):
        cp.start()

    def step(i, carry):
        slot = i % 2
        b = i // steps_per_seq
        s_idx = i % steps_per_seq

        @pl.when(i + 1 < total)
        def _():
            for cp in page_copies(i + 1, 1 - slot):
                cp.start()

        pltpu.make_async_copy(cbuf_ref.at[slot], cbuf_ref.at[slot], sem.at[slot]).wait()
        pltpu.make_async_copy(pbuf_ref.at[slot], pbuf_ref.at[slot], sem.at[slot]).wait()
        _attn_sample_step(b, s_idx, slot, steps_per_seq, ql_ref, qp_ref, cn_ref, kn_ref, o_ref, cbuf_ref,
                          pbuf_ref, kbf_ref, pbf_ref, m_ref, l_ref, acc_ref, g=g,
                          pages_per_chunk=pages_per_chunk, page=page, t_new=t_new, t_pad=t_pad)
        return carry

    lax.fori_loop(0, total, step, 0)


def _attn_sample_step(b, s_idx, slot, steps_per_seq, ql_ref, qp_ref, cn_ref, kn_ref, o_ref, cbuf_ref, pbuf_ref,
                      kbf_ref, pbf_ref, m_ref, l_ref, acc_ref, *, g, pages_per_chunk, page, t_new, t_pad):
    ql = ql_ref[b]
    qp = qp_ref[b]
    rows = ql.shape[0]
    contract_last = (((1,), (1,)), ((), ()))

    @pl.when(s_idx == 0)
    def _():
        cn = cn_ref[b].astype(BF16)
        kn = kn_ref[b].astype(BF16)
        s = (lax.dot_general(ql, cn, contract_last, preferred_element_type=F32)
             + lax.dot_general(qp, kn, contract_last, preferred_element_type=F32))
        t_of_row = lax.broadcasted_iota(jnp.int32, (rows, t_pad), 0) % t_pad
        key = lax.broadcasted_iota(jnp.int32, (rows, t_pad), 1)
        ok = (key <= t_of_row) & (key < t_new)
        s = jnp.where(ok, s, NEG_BIG)
        m = jnp.max(s, axis=-1, keepdims=True)
        p = jnp.where(ok, jnp.exp2(s - m), 0.0)
        m_ref[...] = m
        l_ref[...] = jnp.sum(p, axis=-1, keepdims=True)
        acc_ref[...] = jnp.dot(p.astype(BF16), cn, preferred_element_type=F32)

    kbf_ref[...] = cbuf_ref[slot].astype(BF16)
    for j in range(g):
        pbf_ref[:, j * page:(j + 1) * page] = pbuf_ref[slot, j].astype(BF16)

    ck = pages_per_chunk * page
    parts = []
    for c in range(g // pages_per_chunk):
        kb = kbf_ref[c * ck:(c + 1) * ck, :]
        s = (lax.dot_general(ql, kb, contract_last, preferred_element_type=F32)
             + jnp.dot(qp, pbf_ref[:, c * ck:(c + 1) * ck], preferred_element_type=F32))
        mc = jnp.max(s, axis=-1, keepdims=True)
        p = jnp.exp2(s - mc)
        parts.append((mc, jnp.sum(p, axis=-1, keepdims=True),
                      jnp.dot(p.astype(BF16), kb, preferred_element_type=F32)))
    m_old = m_ref[...]
    m_new = m_old
    for mc, _, _ in parts:
        m_new = jnp.maximum(m_new, mc)
    alpha = jnp.exp2(m_old - m_new)
    l = alpha * l_ref[...]
    acc = alpha * acc_ref[...]
    for mc, lc, ac in parts:
        w = jnp.exp2(mc - m_new)
        l = l + w * lc
        acc = acc + w * ac
    m_ref[...] = m_new
    l_ref[...] = l
    acc_ref[...] = acc

    @pl.when(s_idx == steps_per_seq - 1)
    def _():
        o_ref[b] = acc_ref[...] / l_ref[...]


def _attn_sample(page_table, q_lat, q_pe, ckv_new, kpe_new, cache_ckv, cache_kpe_t, layer, *,
                 t_new, t_pad, pages_per_step, pages_per_chunk):
    bsz, rows, kvr = q_lat.shape
    rope = q_pe.shape[2]
    page = cache_ckv.shape[2]
    g = pages_per_step
    assert page_table.shape[1] % g == 0 and g % pages_per_chunk == 0
    vmem = pl.BlockSpec(memory_space=pltpu.VMEM)
    hbm = pl.BlockSpec(memory_space=pl.ANY)
    return pl.pallas_call(
        functools.partial(_attn_sample_kernel, layer=layer, pages_per_step=g, pages_per_chunk=pages_per_chunk,
                          page=page, t_new=t_new, t_pad=t_pad),
        in_specs=[pl.BlockSpec(memory_space=pltpu.SMEM), vmem, vmem, vmem, vmem, hbm, hbm],
        out_specs=vmem,
        out_shape=jax.ShapeDtypeStruct((bsz, rows, kvr), F32),
        scratch_shapes=[
            pltpu.VMEM((2, g * page, kvr), F32),
            pltpu.VMEM((2, g, rope, page), F32),
            pltpu.VMEM((g * page, kvr), BF16),
            pltpu.VMEM((rope, g * page), BF16),
            pltpu.VMEM((rows, 1), F32),
            pltpu.VMEM((rows, 1), F32),
            pltpu.VMEM((rows, kvr), F32),
            pltpu.SemaphoreType.DMA((2,)),
        ],
        compiler_params=pltpu.CompilerParams(vmem_limit_bytes=VMEM_LIMIT),
        name="attn_sample",
    )(page_table, q_lat, q_pe, ckv_new, kpe_new, cache_ckv, cache_kpe_t)


def _olat_kernel(o_ref, w_ref, out_ref, *, n_heads):
    for h in range(n_heads):
        out_ref[:, h * HEAD_SLOT:(h + 1) * HEAD_SLOT] = jnp.dot(
            o_ref[h].astype(BF16), w_ref[h], preferred_element_type=F32).astype(BF16)


def _olat(o_lat_heads, w_uv_pad):
    n_heads, n, _ = o_lat_heads.shape
    return pl.pallas_call(
        functools.partial(_olat_kernel, n_heads=n_heads),
        out_shape=jax.ShapeDtypeStruct((n, n_heads * HEAD_SLOT), BF16),
        compiler_params=pltpu.CompilerParams(vmem_limit_bytes=VMEM_LIMIT),
        name="o_latent",
    )(o_lat_heads, w_uv_pad)


def _out_proj_kernel(x_ref, c_ref, o_ref, wc_ref, wo_ref, g_ref, x1_ref, h_ref):
    mix = (jnp.dot(c_ref[...], wc_ref[...], preferred_element_type=F32)
           + jnp.dot(o_ref[...], wo_ref[...], preferred_element_type=F32))
    x1 = x_ref[...] + mix
    x1_ref[...] = x1
    h_ref[...] = _rms(x1, g_ref[...]).astype(BF16)


def _out_proj(x, c, o_slots, w_c, w_o, g, *, tm):
    n, d = x.shape
    row = lambda i: (i, 0)
    full = lambda i: (0, 0)
    return pl.pallas_call(
        _out_proj_kernel,
        grid=(n // tm,),
        in_specs=[
            pl.BlockSpec((tm, d), row),
            pl.BlockSpec((tm, c.shape[1]), row),
            pl.BlockSpec((tm, o_slots.shape[1]), row),
            pl.BlockSpec(w_c.shape, full),
            pl.BlockSpec(w_o.shape, full),
            pl.BlockSpec((1, d), full),
        ],
        out_specs=[pl.BlockSpec((tm, d), row), pl.BlockSpec((tm, d), row)],
        out_shape=[jax.ShapeDtypeStruct((n, d), F32), jax.ShapeDtypeStruct((n, d), BF16)],
        compiler_params=_cparams("parallel"),
        name="out_proj",
    )(x, c, o_slots, w_c, w_o, g)


def _swiglu_rows(x, wg, wu, wd):
    a = jnp.dot(x, wg, preferred_element_type=F32)
    b = jnp.dot(x, wu, preferred_element_type=F32)
    mid = (a * jax.nn.sigmoid(a) * b).astype(BF16)
    return jnp.dot(mid, wd, preferred_element_type=F32)


def _ffn_kernel(x_ref, res_ref, wg_hbm, wu_hbm, wd_hbm, o_ref, wg_ref, wu_ref, wd_ref, sem, *, row_splits):
    @pl.when(pl.program_id(0) == 0)
    def _():
        copies = [pltpu.make_async_copy(src, dst, sem.at[j]) for j, (src, dst) in enumerate(
            ((wg_hbm, wg_ref), (wu_hbm, wu_ref), (wd_hbm, wd_ref)))]
        for cp in copies:
            cp.start()
        for cp in copies:
            cp.wait()

    hm = o_ref.shape[0] // row_splits
    for r in range(row_splits):
        rows = slice(r * hm, (r + 1) * hm)
        o_ref[rows, :] = res_ref[rows, :] + _swiglu_rows(x_ref[rows, :], wg_ref[...], wu_ref[...], wd_ref[...])


def _ffn(x, res, wg, wu, wd, *, tm):
    n, d = x.shape
    d_ff = wg.shape[1]
    row = lambda i: (i, 0)
    hbm = pl.BlockSpec(memory_space=pl.ANY)
    return pl.pallas_call(
        functools.partial(_ffn_kernel, row_splits=2 if tm % 32 == 0 else 1),
        grid=(n // tm,),
        in_specs=[pl.BlockSpec((tm, d), row), pl.BlockSpec((tm, d), row), hbm, hbm, hbm],
        out_specs=pl.BlockSpec((tm, d), row),
        out_shape=jax.ShapeDtypeStruct((n, d), F32),
        scratch_shapes=[pltpu.VMEM((d, d_ff), BF16), pltpu.VMEM((d, d_ff), BF16), pltpu.VMEM((d_ff, d), BF16),
                        pltpu.SemaphoreType.DMA((3,))],
        compiler_params=_cparams("arbitrary"),
        name="ffn_dense",
    )(x, res, wg, wu, wd)


def _swiglu_kernel(te_ref, nv_ref, tr_ref, x_ref, wg_ref, wu_ref, wd_ref, o_ref, xb_ref, *, row_groups):
    m = pl.program_id(0)
    f = pl.program_id(1)
    tm = o_ref.shape[0]
    hm = tm // row_groups

    @pl.when(f == 0)
    def _():
        o_ref[...] = jnp.zeros_like(o_ref)

    valid = m < nv_ref[0]
    groups_needed = (tr_ref[jnp.minimum(m, nv_ref[0] - 1)] + hm - 1) // hm

    @pl.when(valid & (f == 0))
    def _():
        xb_ref[...] = x_ref[...].astype(BF16)

    @pl.when(valid & (groups_needed == row_groups))
    def _():
        wg = wg_ref[...].astype(BF16)
        wu = wu_ref[...].astype(BF16)
        wd = wd_ref[...].astype(BF16)
        for r in range(row_groups):
            rows = slice(r * hm, (r + 1) * hm)
            o_ref[rows, :] += _swiglu_rows(xb_ref[rows, :], wg, wu, wd)

    @pl.when(valid & (groups_needed < row_groups))
    def _():
        wg = wg_ref[...].astype(BF16)
        wu = wu_ref[...].astype(BF16)
        wd = wd_ref[...].astype(BF16)

        def group(r, carry):
            rows = pl.ds(pl.multiple_of(r * hm, hm), hm)
            o_ref[rows, :] += _swiglu_rows(xb_ref[rows, :], wg, wu, wd)
            return carry

        lax.fori_loop(0, groups_needed, group, 0)


def _swiglu(tile_expert, n_valid, tile_rows, x, wg, wu, wd, *, tm, tf):
    p, d = x.shape
    d_ff = wg.shape[2]
    n_f = d_ff // tf
    n_tiles = p // tm

    def mm(m, nv):
        return jnp.minimum(m, nv[0] - 1)

    def ff(m, f, nv):
        return jnp.where(m < nv[0], f, n_f - 1)

    return pl.pallas_call(
        functools.partial(_swiglu_kernel, row_groups=4 if tm % 64 == 0 else 1),
        grid_spec=pltpu.PrefetchScalarGridSpec(
            num_scalar_prefetch=3,
            grid=(n_tiles, n_f),
            in_specs=[
                pl.BlockSpec((tm, d), lambda m, f, te, nv, tr: (mm(m, nv), 0)),
                pl.BlockSpec((None, d, tf), lambda m, f, te, nv, tr: (te[mm(m, nv)], 0, ff(m, f, nv))),
                pl.BlockSpec((None, d, tf), lambda m, f, te, nv, tr: (te[mm(m, nv)], 0, ff(m, f, nv))),
                pl.BlockSpec((None, tf, d), lambda m, f, te, nv, tr: (te[mm(m, nv)], ff(m, f, nv), 0)),
            ],
            out_specs=pl.BlockSpec((tm, d), lambda m, f, te, nv, tr: (m, 0)),
            scratch_shapes=[pltpu.VMEM((tm, d), BF16)],
        ),
        out_shape=jax.ShapeDtypeStruct((p, d), F32),
        compiler_params=_cparams("arbitrary", "arbitrary"),
        name="swiglu_experts",
    )(tile_expert, n_valid, tile_rows, x, wg, wu, wd)


def _row_copy(src, src_row, dst, dst_row, sem):
    return pltpu.make_async_copy(src.at[pl.ds(src_row, 1)], dst.at[pl.ds(dst_row, 1)], sem)


def _dispatch_kernel(pos_ref, h_ref, xs_in_ref, xs_ref, sem, *, tt):
    del xs_in_ref

    def body(r, carry):
        for k in range(TOP_K):
            _row_copy(h_ref, r, xs_ref, pos_ref[0, TOP_K * r + k], sem).start(priority=k % 2)
        return carry

    lax.fori_loop(0, tt, body, 0, unroll=8)
    for k in range(TOP_K):
        pltpu.make_async_copy(h_ref, xs_ref.at[pl.ds(0, tt)], sem).wait()


def _dispatch(pos_tiles, h, xs):
    n, d = h.shape
    n_tiles = pos_tiles.shape[0]
    tt = n // n_tiles
    return pl.pallas_call(
        functools.partial(_dispatch_kernel, tt=tt),
        grid=(n_tiles,),
        in_specs=[
            pl.BlockSpec((None, 1, TOP_K * tt), lambda i: (i, 0, 0), memory_space=pltpu.SMEM),
            pl.BlockSpec((tt, d), lambda i: (i, 0)),
            pl.BlockSpec(memory_space=pl.ANY),
        ],
        out_specs=pl.BlockSpec(memory_space=pl.ANY),
        out_shape=jax.ShapeDtypeStruct(xs.shape, xs.dtype),
        scratch_shapes=[pltpu.SemaphoreType.DMA],
        input_output_aliases={2: 0},
        compiler_params=_cparams("arbitrary"),
        name="moe_dispatch",
    )(pos_tiles, h, xs)


def _pool_kernel(*refs, tm, pos0, windows, has_halo, n_experts):
    if has_halo:
        (prev_ref, halo_ref, x_ref, g1_ref, wp_ref, sc_ref, g2_ref, wr_ref,
         x3_ref, h_ref, hn_ref, gate_ref, idx_ref, ext_ref, lv_ref) = refs
    else:
        (prev_ref, x_ref, g1_ref, wp_ref, sc_ref, g2_ref, wr_ref,
         x3_ref, h_ref, hn_ref, gate_ref, idx_ref, ext_ref, lv_ref) = refs
    t = pl.program_id(1)
    x = x_ref[0]
    hn = _rms(x, g1_ref[...])
    hn_ref[0] = hn
    top = SUBLANES
    base = top + POOL_HALO
    end = base + tm
    d = x.shape[1]
    ext_ref[0:top, :] = jnp.zeros((top, d), F32)
    if has_halo:
        @pl.when(t == 0)
        def _():
            ext_ref[top:base, :] = prev_ref[0]

        @pl.when(t > 0)
        def _():
            ext_ref[top:base, :] = _rms(halo_ref[0], g1_ref[...])
    else:
        ext_ref[top:base, :] = prev_ref[0]
    ext_ref[base:end, :] = hn

    gc = d // len(windows)

    def level_rows(k, lo_row, hi_row, lanes):
        if k == 0:
            return ext_ref[lo_row:hi_row, lanes]
        return lv_ref[k - 1, lo_row:hi_row, lanes]

    for k in range(1, len(windows)):
        lanes = slice((k - 1) * gc, d)
        shift = 1 << (k - 1)
        lv_ref[k - 1, 0:top, lanes] = jnp.zeros((top, d - (k - 1) * gc), F32)
        lv_ref[k - 1, top:end, lanes] = (level_rows(k - 1, top, end, lanes)
                                         + level_rows(k - 1, top - shift, end - shift, lanes))

    pos = pos0 + t * tm + lax.broadcasted_iota(jnp.int32, (tm, 1), 0)
    for gi, w in enumerate(windows):
        lanes = slice(gi * gc, (gi + 1) * gc)
        if gi + 1 < len(windows):
            s = level_rows(gi + 1, base, end, lanes)
        else:
            s = level_rows(gi, base, end, lanes) + level_rows(gi, base - w // 2, end - w // 2, lanes)
        cnt = jnp.minimum(pos + 1, w).astype(F32)
        diff = s / cnt - hn[:, lanes]
        y = jnp.dot(diff.astype(BF16), wp_ref[gi], preferred_element_type=F32)
        x3_ref[0, :, lanes] = x[:, lanes] + y * sc_ref[:, lanes]

    x3 = x3_ref[0]
    h2 = _rms(x3, g2_ref[...])
    h_ref[0] = h2
    logits = jnp.dot(h2, wr_ref[...], preferred_element_type=F32, precision=lax.Precision.HIGHEST)
    lane = lax.broadcasted_iota(jnp.int32, logits.shape, 1)
    logits = jnp.where(lane < n_experts, logits, -jnp.inf)
    m1 = jnp.max(logits, axis=-1, keepdims=True)
    i1 = jnp.min(jnp.where(logits == m1, lane, LANES), axis=-1, keepdims=True)
    rest = jnp.where(lane == i1, -jnp.inf, logits)
    m2 = jnp.max(rest, axis=-1, keepdims=True)
    i2 = jnp.min(jnp.where(rest == m2, lane, LANES), axis=-1, keepdims=True)
    e = jnp.exp(m2 - m1)
    g1 = 1.0 / (1.0 + e)
    g2 = e / (1.0 + e)
    gate_ref[0] = jnp.where(lane == 0, g1, jnp.where(lane == 1, g2, 0.0))
    idx_ref[0] = jnp.where(lane == 0, i1, jnp.where(lane == 1, i2, 0))


def _pool_layer(prev_pad, x, g1, w_pool, scale, g2, w_router_pad, *, tm, pos0, n_experts):
    bsz, t_len, d = x.shape
    assert all(w == 2 ** (g + 1) for g, w in enumerate(POOL_WINDOWS)) and POOL_HALO >= max(POOL_WINDOWS) - 1
    n_t = t_len // tm
    has_halo = n_t > 1
    hb = tm // POOL_HALO if has_halo else 1
    n_rows = SUBLANES + POOL_HALO + tm
    cur = lambda b, t: (b, t, 0)
    full2 = lambda b, t: (0, 0)
    in_specs = [pl.BlockSpec((1, POOL_HALO, d), lambda b, t: (b, 0, 0))]
    args = [prev_pad]
    if has_halo:
        in_specs.append(pl.BlockSpec((1, POOL_HALO, d), lambda b, t: (b, jnp.maximum(t * hb - 1, 0), 0)))
        args.append(x)
    in_specs += [
        pl.BlockSpec((1, tm, d), cur),
        pl.BlockSpec((1, d), full2),
        pl.BlockSpec(w_pool.shape, lambda b, t: (0, 0, 0)),
        pl.BlockSpec((1, d), full2),
        pl.BlockSpec((1, d), full2),
        pl.BlockSpec(w_router_pad.shape, full2),
    ]
    args += [x, g1, w_pool, scale, g2, w_router_pad]
    return pl.pallas_call(
        functools.partial(_pool_kernel, tm=tm, pos0=pos0, windows=POOL_WINDOWS, has_halo=has_halo,
                          n_experts=n_experts),
        grid=(bsz, n_t),
        in_specs=in_specs,
        out_specs=[
            pl.BlockSpec((1, tm, d), cur),
            pl.BlockSpec((1, tm, d), cur),
            pl.BlockSpec((1, tm, d), cur),
            pl.BlockSpec((1, tm, LANES), cur),
            pl.BlockSpec((1, tm, LANES), cur),
        ],
        out_shape=[
            jax.ShapeDtypeStruct((bsz, t_len, d), F32),
            jax.ShapeDtypeStruct((bsz, t_len, d), F32),
            jax.ShapeDtypeStruct((bsz, t_len, d), F32),
            jax.ShapeDtypeStruct((bsz, t_len, LANES), F32),
            jax.ShapeDtypeStruct((bsz, t_len, LANES), jnp.int32),
        ],
        scratch_shapes=[pltpu.VMEM((n_rows, d), F32), pltpu.VMEM((len(POOL_WINDOWS) - 1, n_rows, d), F32)],
        compiler_params=_cparams("parallel", "parallel"),
        name="pool_layer",
    )(*args)


def _combine_kernel(pos_cur_ref, pos_nxt_ref, x_ref, gate_ref, g_ref, ys_ref, y_ref, buf_ref, sem,
                    *, tt, apply_norm):
    i = pl.program_id(0)
    n = pl.num_programs(0)
    slot = i % 2

    def fetch(pos_ref, s):
        def body(r, carry):
            for k in range(TOP_K):
                _row_copy(ys_ref, pos_ref[0, TOP_K * r + k], buf_ref.at[s, k], r,
                          sem.at[s]).start(priority=k % 2)
            return carry
        lax.fori_loop(0, tt, body, 0, unroll=8)

    @pl.when(i == 0)
    def _():
        fetch(pos_cur_ref, 0)

    @pl.when(i + 1 < n)
    def _():
        fetch(pos_nxt_ref, 1 - slot)

    for k in range(TOP_K):
        pltpu.make_async_copy(ys_ref.at[pl.ds(0, tt)], buf_ref.at[slot, k], sem.at[slot]).wait()
    moe = gate_ref[:, 0:1] * buf_ref[slot, 0]
    for k in range(1, TOP_K):
        moe = moe + gate_ref[:, k:k + 1] * buf_ref[slot, k]
    y = x_ref[...] + moe
    y_ref[...] = _rms(y, g_ref[...]) if apply_norm else y


def _combine(pos_tiles, x, gates, g, ys, *, apply_norm):
    n, d = x.shape
    n_tiles = pos_tiles.shape[0]
    tt = n // n_tiles
    pos_spec = lambda fn: pl.BlockSpec((None, 1, TOP_K * tt), fn, memory_space=pltpu.SMEM)
    return pl.pallas_call(
        functools.partial(_combine_kernel, tt=tt, apply_norm=apply_norm),
        grid=(n_tiles,),
        in_specs=[
            pos_spec(lambda i: (i, 0, 0)),
            pos_spec(lambda i: (jnp.minimum(i + 1, n_tiles - 1), 0, 0)),
            pl.BlockSpec((tt, d), lambda i: (i, 0)),
            pl.BlockSpec((tt, LANES), lambda i: (i, 0)),
            pl.BlockSpec((1, d), lambda i: (0, 0)),
            pl.BlockSpec(memory_space=pl.ANY),
        ],
        out_specs=pl.BlockSpec((tt, d), lambda i: (i, 0)),
        out_shape=jax.ShapeDtypeStruct((n, d), F32),
        scratch_shapes=[pltpu.VMEM((2, TOP_K, tt, d), F32), pltpu.SemaphoreType.DMA((2,))],
        compiler_params=_cparams("arbitrary"),
        name="moe_combine",
    )(pos_tiles, pos_tiles, x, gates, g, ys)


def _final_kernel(x_ref, g_ref, y_ref):
    y_ref[...] = _rms(x_ref[...], g_ref[...])


def _final(x, g, *, tm):
    n, d = x.shape
    row = lambda i: (i, 0)
    return pl.pallas_call(
        _final_kernel,
        grid=(n // tm,),
        in_specs=[pl.BlockSpec((tm, d), row), pl.BlockSpec((1, d), lambda i: (0, 0))],
        out_specs=pl.BlockSpec((tm, d), row),
        out_shape=jax.ShapeDtypeStruct((n, d), F32),
        compiler_params=_cparams("parallel"),
        name="final_norm",
    )(x, g)


def _swap_halves(w):
    half = w.shape[-1] // 2
    return jnp.concatenate([w[..., half:], w[..., :half]], axis=-1)


def _slot(parts, total=HEAD_SLOT):
    used = sum(p.shape[-1] for p in parts)
    pad = jnp.zeros(parts[0].shape[:-1] + (total - used,), parts[0].dtype)
    return jnp.concatenate(list(parts) + [pad], axis=-1)


def _prep_even_weights(w_in, w_uq, w_ukv, w_out, *, dc, qr, kvr):
    d = w_in.shape[0]
    body = w_in[:, :2 * dc + qr + kvr]
    w_kpe = w_in[:, 2 * dc + qr + kvr:]
    z_nope = jnp.zeros((d, NOPE_DIM), w_in.dtype)
    w_in_ext = jnp.concatenate(
        [body, _slot([z_nope, w_kpe]), _slot([z_nope, _swap_halves(w_kpe)])], axis=-1).astype(BF16)

    uq = w_uq.reshape(qr, N_HEADS, NOPE_DIM + ROPE_DIM)
    uq_nope, uq_pe = uq[..., :NOPE_DIM], uq[..., NOPE_DIM:]
    qa = _slot([uq_nope, uq_pe]).reshape(qr, N_HEADS * HEAD_SLOT)
    qb = _slot([jnp.zeros_like(uq_nope), _swap_halves(uq_pe)]).reshape(qr, N_HEADS * HEAD_SLOT)
    w_uq_ext = jnp.concatenate([qa, qb], axis=-1).astype(BF16)

    ukv = w_ukv.reshape(kvr, N_HEADS, NOPE_DIM + V_DIM)
    uk, uv = ukv[..., :NOPE_DIM], ukv[..., NOPE_DIM:]
    w_kv_ext = jnp.concatenate([_slot([uk]).reshape(kvr, N_HEADS * HEAD_SLOT),
                                _slot([uv]).reshape(kvr, N_HEADS * HEAD_SLOT)], axis=-1).astype(BF16)
    w_q2lat = jnp.concatenate(
        [jnp.transpose(uk, (1, 2, 0)),
         jnp.zeros((N_HEADS, HEAD_SLOT - NOPE_DIM, kvr), w_ukv.dtype)], axis=1).astype(BF16)
    w_uv_pad = _slot([jnp.transpose(uv, (1, 0, 2))]).astype(BF16)

    w_c = w_out[:dc].astype(BF16)
    wo = w_out[dc:].reshape(N_HEADS, V_DIM, d)
    w_o = jnp.concatenate([wo, jnp.zeros((N_HEADS, HEAD_SLOT - V_DIM, d), w_out.dtype)],
                          axis=1).reshape(N_HEADS * HEAD_SLOT, d).astype(BF16)
    return w_in_ext, w_uq_ext, w_kv_ext, w_q2lat, w_uv_pad, w_c, w_o


def _rope_tables(pos):
    half = ROPE_DIM // 2
    inv_freq = jnp.power(ROPE_THETA, -jnp.arange(half, dtype=F32) / half)
    ang = pos.astype(F32)[:, None] * inv_freq[None, :]
    cos, sin = jnp.cos(ang), jnp.sin(ang)
    n = pos.shape[0]
    z_nope = jnp.zeros((n, NOPE_DIM), F32)
    cos_k = _slot([z_nope, cos, cos])
    sin_k = _slot([z_nope, -sin, sin])
    cos_q = _slot([jnp.ones((n, NOPE_DIM), F32), cos, cos]) * (ATTN_SCALE * LOG2E)
    sin_q = sin_k * (ATTN_SCALE * LOG2E)
    return cos_k, sin_k, cos_q, sin_q


def _even_layer(x, conv_prev, pos, attend, P, layer, i):
    bsz, t_len, d = x.shape
    n = bsz * t_len
    dc = P['conv_w'].shape[2]
    qr = P['q_norm'].shape[1]
    kvr = P['kv_norm'].shape[1]
    k_taps = P['conv_w'].shape[1]
    tm = _row_tile(n, 512) if t_len >= 512 else n
    w_in_ext, w_uq_ext, w_kv_ext, w_q2lat, w_uv_pad, w_c, w_o = _prep_even_weights(
        P['w_in'][i], P['w_uq'][i], P['w_ukv'][i], P['w_out'][i], dc=dc, qr=qr, kvr=kvr)
    cos_k, sin_k, cos_q, sin_q = _rope_tables(pos)
    if t_len < tm:
        reps = tm // t_len
        cos_k, sin_k, cos_q, sin_q = (jnp.tile(a, (reps, 1)) for a in (cos_k, sin_k, cos_q, sin_q))

    xf = x.reshape(n, d)
    u, qn, ckv, kpe_slot = _in_proj(
        xf, P['norm_mix'][layer][None], w_in_ext, P['q_norm'][i][None], P['kv_norm'][i][None],
        cos_k, sin_k, dc=dc, qr=qr, kvr=kvr, tm=tm)

    u3 = u.reshape(bsz, t_len, dc)
    prev_pad = jnp.pad(conv_prev.astype(F32), ((0, 0), (CONV_HALO - (k_taps - 1), 0), (0, 0)))
    c = _conv_branch(prev_pad, u3, P['conv_w'][i], P['conv_b'][i][None], P['conv_ln_g'][i][None],
                     P['conv_ln_b'][i][None], tc=min(t_len, 512))
    conv_state = jnp.concatenate([conv_prev.astype(F32), u3], axis=1)[:, -(k_taps - 1):]

    q_slots = _q_proj(qn, w_uq_ext, cos_q, sin_q, tm=tm)
    o_slots = attend(q_slots, ckv, kpe_slot, w_kv_ext, w_q2lat, w_uv_pad, tm)

    x1, h_ffn = _out_proj(xf, c.reshape(n, dc), o_slots, w_c, w_o, P['norm_ffn'][layer][None], tm=tm)

    x2 = _ffn(h_ffn, x1, P['ffn_w_gate'][i].astype(BF16), P['ffn_w_up'][i].astype(BF16),
              P['ffn_w_down'][i].astype(BF16), tm=tm)
    kpe = kpe_slot[:, NOPE_DIM:NOPE_DIM + ROPE_DIM]
    return (x2.reshape(bsz, t_len, d), conv_state, ckv.reshape(bsz, t_len, kvr),
            kpe.reshape(bsz, t_len, ROPE_DIM))


def _make_attend_prompt(bsz, t_len):
    def attend(q_slots, ckv, kpe_slot, w_kv_ext, w_q2lat, w_uv_pad, tm):
        k_slots, v_slots = _kv_proj(ckv, kpe_slot, w_kv_ext, tm=tm)
        return _attn_prompt(q_slots, k_slots, v_slots, bsz=bsz, t_len=t_len, tq=min(t_len, 512))
    return attend


def _make_attend_sample(bsz, t_len, cache_ckv, cache_kpe_t, page_table, i):
    t_pad = 8

    def attend(q_slots, ckv, kpe_slot, w_kv_ext, w_q2lat, w_uv_pad, tm):
        n = bsz * t_len
        kvr = ckv.shape[1]
        q_lat = _qlat(q_slots, w_q2lat)
        q_lat = q_lat.reshape(N_HEADS, bsz, t_len, kvr)
        q_lat = jnp.pad(q_lat, ((0, 0), (0, 0), (0, t_pad - t_len), (0, 0)))
        q_lat = jnp.transpose(q_lat, (1, 0, 2, 3)).reshape(bsz, N_HEADS * t_pad, kvr)
        q_pe = q_slots.reshape(bsz, t_len, N_HEADS, HEAD_SLOT)[..., NOPE_DIM:NOPE_DIM + ROPE_DIM]
        q_pe = jnp.pad(jnp.transpose(q_pe, (0, 2, 1, 3)), ((0, 0), (0, 0), (0, t_pad - t_len), (0, 0)))
        q_pe = q_pe.reshape(bsz, N_HEADS * t_pad, ROPE_DIM)
        pad_t = ((0, 0), (0, t_pad - t_len), (0, 0))
        ckv_new = jnp.pad(ckv.reshape(bsz, t_len, kvr), pad_t)
        kpe_new = jnp.pad(kpe_slot[:, NOPE_DIM:NOPE_DIM + ROPE_DIM].reshape(bsz, t_len, ROPE_DIM), pad_t)
        o_lat = _attn_sample(page_table, q_lat, q_pe, ckv_new, kpe_new, cache_ckv, cache_kpe_t, i,
                             t_new=t_len, t_pad=t_pad, pages_per_step=64,
                             pages_per_chunk=64)
        o_lat = o_lat.reshape(bsz, N_HEADS, t_pad, kvr)[:, :, :t_len]
        o_lat = jnp.transpose(o_lat, (1, 0, 2, 3)).reshape(N_HEADS, n, kvr)
        return _olat(o_lat, w_uv_pad)
    return attend


def _odd_layer_mixer(x, pool_prev, pos0, P, layer, i):
    bsz, t_len, d = x.shape
    n_experts = P['router_w'].shape[2]
    ctx = max(POOL_WINDOWS) - 1
    prev_pad = jnp.pad(pool_prev.astype(F32), ((0, 0), (POOL_HALO - ctx, 0), (0, 0)))
    w_router_pad = jnp.pad(P['router_w'][i], ((0, 0), (0, LANES - n_experts)))
    x3, h_moe, hn, gates, ids = _pool_layer(
        prev_pad, x, P['norm_mix'][layer][None], P['pool_w'][i].astype(BF16), P['pool_scale'][i][None],
        P['norm_ffn'][layer][None], w_router_pad, tm=min(t_len, 512), pos0=pos0, n_experts=n_experts)
    pool_state = jnp.concatenate([pool_prev.astype(F32), hn], axis=1)[:, -ctx:]
    n = bsz * t_len
    return (x3.reshape(n, d), h_moe.reshape(n, d), pool_state, gates.reshape(n, LANES),
            ids.reshape(n, LANES)[:, :TOP_K])


def _moe_plan(ids, n_experts, tm):
    n = ids.shape[0]
    flat = ids.reshape(-1)
    experts = jnp.arange(n_experts, dtype=jnp.int32)
    onehot = (flat[:, None] == experts[None, :]).astype(jnp.int32)
    csum = jnp.cumsum(onehot, axis=0)
    counts = csum[-1]
    rank = jnp.sum(csum * onehot, axis=1) - 1
    padded = ((counts + tm - 1) // tm) * tm
    ends = jnp.cumsum(padded)
    starts = ends - padded
    pos = jnp.sum(starts[None, :] * onehot, axis=1) + rank
    n_tiles = (n * TOP_K + tm - 1) // tm + n_experts
    n_valid = (ends[-1] // tm).astype(jnp.int32)
    tile_start = jnp.arange(n_tiles, dtype=jnp.int32) * tm
    tile_expert = jnp.sum((ends[None, :] <= tile_start[:, None]).astype(jnp.int32), axis=1)
    tile_expert = jnp.minimum(tile_expert, n_experts - 1)
    group_end = starts + counts
    tile_onehot = (tile_expert[:, None] == experts[None, :]).astype(jnp.int32)
    tile_rows = jnp.clip(jnp.sum(group_end[None, :] * tile_onehot, axis=1) - tile_start, 0, tm)
    return pos.reshape(n, TOP_K), tile_expert, n_valid.reshape(1), tile_rows.astype(jnp.int32), n_tiles


def _moe_layer(groups, P, i, g_final, *, tm, tt):
    d = groups[0][0].shape[1]
    n_experts = P['router_w'].shape[2]
    ids = jnp.concatenate([g[3] for g in groups], axis=0)
    pos, tile_expert, n_valid, tile_rows, n_tiles = _moe_plan(ids, n_experts, tm)
    xs = jnp.zeros((n_tiles * tm, d), F32)
    pos_tiles = []
    lo = 0
    for x3, h, _, _ in groups:
        n = x3.shape[0]
        t = min(tt, n)
        pt = pos[lo:lo + n].reshape(n // t, 1, TOP_K * t)
        xs = _dispatch(pt, h, xs)
        pos_tiles.append(pt)
        lo += n
    ys = _swiglu(tile_expert, n_valid, tile_rows, xs, P['moe_w_gate'][i], P['moe_w_up'][i],
                 P['moe_w_down'][i], tm=tm, tf=512)
    gain = g_final if g_final is not None else jnp.ones((1, d), F32)
    return [_combine(pt, x3, gates, gain, ys, apply_norm=g_final is not None)
            for pt, (x3, _, gates, _) in zip(pos_tiles, groups)]


def kernel(x_prompt, x_sample, cache_ckv, cache_kpe, page_table, state_conv, state_pool, norm_mix, norm_ffn,
           norm_final, w_in, conv_w, conv_b, conv_ln_g, conv_ln_b, q_norm, w_uq, kv_norm, w_ukv, w_out,
           ffn_w_gate, ffn_w_up, ffn_w_down, pool_w, pool_scale, router_w, moe_w_gate, moe_w_up, moe_w_down):
    P = {'norm_mix': norm_mix, 'norm_ffn': norm_ffn, 'norm_final': norm_final, 'w_in': w_in,
         'conv_w': conv_w, 'conv_b': conv_b, 'conv_ln_g': conv_ln_g, 'conv_ln_b': conv_ln_b,
         'q_norm': q_norm, 'w_uq': w_uq, 'kv_norm': kv_norm, 'w_ukv': w_ukv, 'w_out': w_out,
         'ffn_w_gate': ffn_w_gate, 'ffn_w_up': ffn_w_up, 'ffn_w_down': ffn_w_down,
         'pool_w': pool_w, 'pool_scale': pool_scale, 'router_w': router_w,
         'moe_w_gate': moe_w_gate, 'moe_w_up': moe_w_up, 'moe_w_down': moe_w_down}
    depth = norm_mix.shape[0]
    d = x_prompt.shape[2]
    bp, tp, _ = x_prompt.shape
    bs, ts, _ = x_sample.shape
    past = page_table.shape[1] * cache_ckv.shape[2]
    k_taps = conv_w.shape[1]
    ctx = max(POOL_WINDOWS) - 1
    pos_p = jnp.arange(tp, dtype=jnp.int32)
    pos_s = past + jnp.arange(ts, dtype=jnp.int32)
    cache_kpe_t = jnp.swapaxes(cache_kpe, 2, 3)

    xp, xs = x_prompt, x_sample
    outs_p = {'ckv': [], 'kpe': [], 'conv': [], 'pool': []}
    outs_s = {'ckv': [], 'kpe': [], 'conv': [], 'pool': []}
    for layer in range(depth):
        i = layer // 2
        last = layer == depth - 1
        if layer % 2 == 0:
            conv0 = jnp.zeros((bp, k_taps - 1, conv_w.shape[2]), F32)
            xp, cst, ckv, kpe = _even_layer(xp, conv0, pos_p, _make_attend_prompt(bp, tp), P, layer, i)
            outs_p['conv'].append(cst); outs_p['ckv'].append(ckv); outs_p['kpe'].append(kpe)
            xs, cst, ckv, kpe = _even_layer(
                xs, state_conv[i], pos_s, _make_attend_sample(bs, ts, cache_ckv, cache_kpe_t, page_table, i),
                P, layer, i)
            outs_s['conv'].append(cst); outs_s['ckv'].append(ckv); outs_s['kpe'].append(kpe)
            if last:
                xp = _final(xp.reshape(bp * tp, d), norm_final[None], tm=_row_tile(bp * tp, 512)).reshape(xp.shape)
                xs = _final(xs.reshape(bs * ts, d), norm_final[None], tm=_row_tile(bs * ts, 512)).reshape(xs.shape)
        else:
            pool0 = jnp.zeros((bp, ctx, d), F32)
            x3p, hp, pst_p, gp, ip = _odd_layer_mixer(xp, pool0, 0, P, layer, i)
            x3s, hs, pst_s, gs, is_ = _odd_layer_mixer(xs, state_pool[i], past, P, layer, i)
            outs_p['pool'].append(pst_p); outs_s['pool'].append(pst_s)
            yp, ysm = _moe_layer([(x3p, hp, gp, ip), (x3s, hs, gs, is_)], P, i,
                                 norm_final[None] if last else None, tm=1024, tt=512)
            xp, xs = yp.reshape(xp.shape), ysm.reshape(xs.shape)

    def stack(lst, shape_if_empty):
        return jnp.stack(lst) if lst else jnp.zeros(shape_if_empty, F32)

    return (xp, xs,
            jnp.stack(outs_p['ckv']), jnp.stack(outs_p['kpe']), jnp.stack(outs_p['conv']),
            stack(outs_p['pool'], (0, bp, ctx, d)),
            jnp.stack(outs_s['ckv']), jnp.stack(outs_s['kpe']), jnp.stack(outs_s['conv']),
            stack(outs_s['pool'], (0, bs, ctx, d)))
```

```python
import functools

import jax
import jax.numpy as jnp
from jax import lax
from jax.experimental import pallas as pl
from jax.experimental.pallas import tpu as pltpu

F32 = jnp.float32
BF16 = jnp.bfloat16

N_HEADS = 8
NOPE_DIM = 64
ROPE_DIM = 32
V_DIM = 64
ROPE_THETA = 10000.0
ATTN_SCALE = (NOPE_DIM + ROPE_DIM) ** -0.5
LOG2E = 1.4426950408889634
POOL_WINDOWS = (2, 4, 8, 16)
TOP_K = 2
EPS = 1e-6

HEAD_SLOT = 128
LANES = 128
SUBLANES = 8
NEG_BIG = -1e30
VMEM_LIMIT = 56 * 1024 * 1024

CONV_HALO = 32
POOL_HALO = 16


def _cparams(*semantics):
    return pltpu.CompilerParams(dimension_semantics=semantics, vmem_limit_bytes=VMEM_LIMIT)


def _rms(x, g):
    return x * lax.rsqrt(jnp.mean(x * x, axis=-1, keepdims=True) + EPS) * g


def _row_tile(n, want):
    if n <= want:
        return n
    t = want
    while t >= 8:
        if n % t == 0 and t % 8 == 0:
            return t
        t -= 8
    return n


def _in_proj_kernel(x_ref, g_ref, w_ref, qg_ref, kvg_ref, cos_ref, sin_ref,
                    u_ref, qn_ref, ckv_ref, kpe_ref, *, dc, qr, kvr):
    h = _rms(x_ref[...], g_ref[...])
    proj = jnp.dot(h.astype(BF16), w_ref[...], preferred_element_type=F32)
    a = proj[:, :dc]
    gate = proj[:, dc:2 * dc]
    u_ref[...] = a * jax.nn.sigmoid(gate)
    o = 2 * dc
    qn_ref[...] = _rms(proj[:, o:o + qr], qg_ref[...]).astype(BF16)
    o += qr
    ckv_ref[...] = _rms(proj[:, o:o + kvr], kvg_ref[...])
    o += kvr
    kpe_ref[...] = (proj[:, o:o + HEAD_SLOT] * cos_ref[...]
                    + proj[:, o + HEAD_SLOT:o + 2 * HEAD_SLOT] * sin_ref[...])


def _in_proj(x, g, w_ext, qg, kvg, cos_k, sin_k, *, dc, qr, kvr, tm):
    n, d = x.shape
    n_tab = cos_k.shape[0] // tm
    row = lambda i: (i, 0)
    full = lambda i: (0, 0)
    tab = lambda i: (i % n_tab, 0)
    return pl.pallas_call(
        functools.partial(_in_proj_kernel, dc=dc, qr=qr, kvr=kvr),
        grid=(n // tm,),
        in_specs=[
            pl.BlockSpec((tm, d), row),
            pl.BlockSpec((1, d), full),
            pl.BlockSpec(w_ext.shape, full),
            pl.BlockSpec((1, qr), full),
            pl.BlockSpec((1, kvr), full),
            pl.BlockSpec((tm, HEAD_SLOT), tab),
            pl.BlockSpec((tm, HEAD_SLOT), tab),
        ],
        out_specs=[
            pl.BlockSpec((tm, dc), row),
            pl.BlockSpec((tm, qr), row),
            pl.BlockSpec((tm, kvr), row),
            pl.BlockSpec((tm, HEAD_SLOT), row),
        ],
        out_shape=[
            jax.ShapeDtypeStruct((n, dc), F32),
            jax.ShapeDtypeStruct((n, qr), BF16),
            jax.ShapeDtypeStruct((n, kvr), F32),
            jax.ShapeDtypeStruct((n, HEAD_SLOT), F32),
        ],
        compiler_params=_cparams("parallel"),
        name="in_proj",
    )(x, g, w_ext, qg, kvg, cos_k, sin_k)


def _conv_kernel(*refs, tc, k_taps, rows_per_chunk, has_halo, seqs):
    if has_halo:
        prev_ref, halo_ref, u_ref, w_ref, b_ref, lg_ref, lb_ref, c_ref, ext_ref, sh_ref = refs
    else:
        prev_ref, u_ref, w_ref, b_ref, lg_ref, lb_ref, c_ref, ext_ref, sh_ref = refs
    n_rows = CONV_HALO + tc
    first = CONV_HALO - (k_taps - 1)

    def window(start, n):
        q, s = divmod(start, SUBLANES)
        if s == 0:
            return ext_ref[start:start + n, :]
        return sh_ref[s - 1, q * SUBLANES:q * SUBLANES + n, :]

    for sq in range(seqs):
        if has_halo:
            t = pl.program_id(1)

            @pl.when(t == 0)
            def _():
                ext_ref[0:CONV_HALO, :] = prev_ref[sq]

            @pl.when(t > 0)
            def _():
                ext_ref[0:CONV_HALO, :] = halo_ref[sq]
        else:
            ext_ref[0:CONV_HALO, :] = prev_ref[sq]
        ext_ref[CONV_HALO:n_rows, :] = u_ref[sq]

        for s in range(1, SUBLANES):
            sh_ref[s - 1, 0:n_rows - SUBLANES, :] = ext_ref[s:s + n_rows - SUBLANES, :]

        for c0 in range(0, tc, rows_per_chunk):
            rc = min(rows_per_chunk, tc - c0)
            acc = w_ref[0:1, :] * window(first + c0, rc)
            for k in range(1, k_taps):
                acc = acc + w_ref[k:k + 1, :] * window(first + c0 + k, rc)
            acc = acc + b_ref[...]
            mu = jnp.mean(acc, axis=-1, keepdims=True)
            xc = acc - mu
            y = xc * lax.rsqrt(jnp.mean(xc * xc, axis=-1, keepdims=True) + EPS)
            y = y * lg_ref[...] + lb_ref[...]
            c_ref[sq, c0:c0 + rc, :] = (y * jax.nn.sigmoid(y)).astype(BF16)


def _seqs_per_step(bsz, n_t, want=8):
    if n_t > 1:
        return 1
    g = min(want, bsz)
    while bsz % g:
        g -= 1
    return g


def _conv_branch(prev_pad, u, w, b, lg, lb, *, tc):
    bsz, t_len, c = u.shape
    k_taps = w.shape[0]
    n_t = t_len // tc
    has_halo = n_t > 1
    hb = tc // CONV_HALO if has_halo else 1
    seqs = _seqs_per_step(bsz, n_t)
    in_specs = [pl.BlockSpec((seqs, CONV_HALO, c), lambda bi, ti: (bi, 0, 0))]
    args = [prev_pad]
    if has_halo:
        in_specs.append(pl.BlockSpec((seqs, CONV_HALO, c),
                                     lambda bi, ti: (bi, jnp.maximum(ti * hb - 1, 0), 0)))
        args.append(u)
    in_specs += [
        pl.BlockSpec((seqs, tc, c), lambda bi, ti: (bi, ti, 0)),
        pl.BlockSpec((k_taps, c), lambda bi, ti: (0, 0)),
        pl.BlockSpec((1, c), lambda bi, ti: (0, 0)),
        pl.BlockSpec((1, c), lambda bi, ti: (0, 0)),
        pl.BlockSpec((1, c), lambda bi, ti: (0, 0)),
    ]
    args += [u, w, b, lg, lb]
    return pl.pallas_call(
        functools.partial(_conv_kernel, tc=tc, k_taps=k_taps, rows_per_chunk=64, has_halo=has_halo, seqs=seqs),
        grid=(bsz // seqs, n_t),
        in_specs=in_specs,
        out_specs=pl.BlockSpec((seqs, tc, c), lambda bi, ti: (bi, ti, 0)),
        out_shape=jax.ShapeDtypeStruct((bsz, t_len, c), BF16),
        scratch_shapes=[pltpu.VMEM((CONV_HALO + tc, c), F32),
                        pltpu.VMEM((SUBLANES - 1, CONV_HALO + tc, c), F32)],
        compiler_params=_cparams("parallel", "parallel"),
        name="conv_branch",
    )(*args)


def _q_proj_kernel(qn_ref, w_ref, cos_ref, sin_ref, q_ref, *, n_heads):
    proj = jnp.dot(qn_ref[...], w_ref[...], preferred_element_type=F32)
    half = n_heads * HEAD_SLOT
    cos = cos_ref[...]
    sin = sin_ref[...]
    for h in range(n_heads):
        a = proj[:, h * HEAD_SLOT:(h + 1) * HEAD_SLOT]
        b = proj[:, half + h * HEAD_SLOT:half + (h + 1) * HEAD_SLOT]
        q_ref[:, h * HEAD_SLOT:(h + 1) * HEAD_SLOT] = (a * cos + b * sin).astype(BF16)


def _q_proj(qn, w_ext, cos_q, sin_q, *, tm):
    n, qr = qn.shape
    n_tab = cos_q.shape[0] // tm
    width = N_HEADS * HEAD_SLOT
    return pl.pallas_call(
        functools.partial(_q_proj_kernel, n_heads=N_HEADS),
        grid=(n // tm,),
        in_specs=[
            pl.BlockSpec((tm, qr), lambda i: (i, 0)),
            pl.BlockSpec(w_ext.shape, lambda i: (0, 0)),
            pl.BlockSpec((tm, HEAD_SLOT), lambda i: (i % n_tab, 0)),
            pl.BlockSpec((tm, HEAD_SLOT), lambda i: (i % n_tab, 0)),
        ],
        out_specs=pl.BlockSpec((tm, width), lambda i: (i, 0)),
        out_shape=jax.ShapeDtypeStruct((n, width), BF16),
        compiler_params=_cparams("parallel"),
        name="q_proj",
    )(qn, w_ext, cos_q, sin_q)


def _kv_proj_kernel(ckv_ref, kpe_ref, w_ref, k_ref, v_ref, *, n_heads):
    proj = jnp.dot(ckv_ref[...].astype(BF16), w_ref[...], preferred_element_type=F32)
    half = n_heads * HEAD_SLOT
    kpe = kpe_ref[...]
    for h in range(n_heads):
        k_ref[:, h * HEAD_SLOT:(h + 1) * HEAD_SLOT] = (
            proj[:, h * HEAD_SLOT:(h + 1) * HEAD_SLOT] + kpe).astype(BF16)
    v = proj[:, half:]
    lane = lax.broadcasted_iota(jnp.int32, v.shape, 1)
    v_ref[...] = jnp.where(lane % HEAD_SLOT == V_DIM, 1.0, v).astype(BF16)


def _kv_proj(ckv, kpe_slot, w_ext, *, tm):
    n, kvr = ckv.shape
    width = N_HEADS * HEAD_SLOT
    return pl.pallas_call(
        functools.partial(_kv_proj_kernel, n_heads=N_HEADS),
        grid=(n // tm,),
        in_specs=[
            pl.BlockSpec((tm, kvr), lambda i: (i, 0)),
            pl.BlockSpec((tm, HEAD_SLOT), lambda i: (i, 0)),
            pl.BlockSpec(w_ext.shape, lambda i: (0, 0)),
        ],
        out_specs=[pl.BlockSpec((tm, width), lambda i: (i, 0)),
                   pl.BlockSpec((tm, width), lambda i: (i, 0))],
        out_shape=[jax.ShapeDtypeStruct((n, width), BF16),
                   jax.ShapeDtypeStruct((n, width), BF16)],
        compiler_params=_cparams("parallel"),
        name="kv_proj",
    )(ckv, kpe_slot, w_ext)


def _attn_prompt_kernel(q_ref, k_ref, v_ref, o_ref, *, tq, n_heads):
    qi = pl.program_id(1)
    row = lax.broadcasted_iota(jnp.int32, (tq, tq), 0)
    col = lax.broadcasted_iota(jnp.int32, (tq, tq), 1)
    causal = col <= row

    head_lanes = [slice(h * HEAD_SLOT, (h + 1) * HEAD_SLOT) for h in range(n_heads)]

    def step(kt, carry, masked):
        ms, accs = carry
        start = pl.multiple_of(kt * tq, tq)
        new_ms, new_accs = [], []
        for h, lanes in enumerate(head_lanes):
            k = k_ref[pl.ds(start, tq), lanes]
            v = v_ref[pl.ds(start, tq), lanes]
            s = lax.dot_general(q_ref[:, lanes], k, (((1,), (1,)), ((), ())), preferred_element_type=F32)
            if masked:
                s = jnp.where(causal, s, NEG_BIG)
            m_new = jnp.maximum(ms[h], jnp.max(s, axis=-1, keepdims=True))
            p = jnp.exp2(s - m_new)
            new_accs.append(jnp.exp2(ms[h] - m_new) * accs[h]
                            + jnp.dot(p.astype(BF16), v, preferred_element_type=F32))
            new_ms.append(m_new)
        return tuple(new_ms), tuple(new_accs)

    init = (tuple(jnp.full((tq, 1), NEG_BIG, F32) for _ in head_lanes),
            tuple(jnp.zeros((tq, HEAD_SLOT), F32) for _ in head_lanes))
    carry = lax.fori_loop(0, qi, functools.partial(step, masked=False), init)
    _, accs = step(qi, carry, True)
    for lanes, acc in zip(head_lanes, accs):
        o_ref[:, lanes] = (acc / acc[:, V_DIM:V_DIM + 1]).astype(BF16)


def _attn_prompt(q, k, v, *, bsz, t_len, tq):
    width = q.shape[1]
    nq = t_len // tq
    return pl.pallas_call(
        functools.partial(_attn_prompt_kernel, tq=tq, n_heads=N_HEADS),
        grid=(bsz, nq),
        in_specs=[
            pl.BlockSpec((tq, width), lambda b, i: (b * nq + i, 0)),
            pl.BlockSpec((t_len, width), lambda b, i: (b, 0)),
            pl.BlockSpec((t_len, width), lambda b, i: (b, 0)),
        ],
        out_specs=pl.BlockSpec((tq, width), lambda b, i: (b * nq + i, 0)),
        out_shape=jax.ShapeDtypeStruct(q.shape, BF16),
        compiler_params=_cparams("parallel", "arbitrary"),
        name="attn_prompt",
    )(q, k, v)


def _qlat_kernel(q_ref, w_ref, o_ref, *, n_heads):
    for h in range(n_heads):
        o_ref[h] = jnp.dot(q_ref[:, h * HEAD_SLOT:(h + 1) * HEAD_SLOT], w_ref[h],
                           preferred_element_type=F32).astype(BF16)


def _qlat(q_slots, w_q2lat):
    n = q_slots.shape[0]
    n_heads, _, kvr = w_q2lat.shape
    return pl.pallas_call(
        functools.partial(_qlat_kernel, n_heads=n_heads),
        out_shape=jax.ShapeDtypeStruct((n_heads, n, kvr), BF16),
        compiler_params=pltpu.CompilerParams(vmem_limit_bytes=VMEM_LIMIT),
        name="q_latent",
    )(q_slots, w_q2lat)


def _attn_sample_kernel(pt_ref, ql_ref, qp_ref, cn_ref, kn_ref, ckv_hbm, kpe_hbm, o_ref,
                        cbuf_ref, pbuf_ref, kbf_ref, pbf_ref, m_ref, l_ref, acc_ref, sem,
                        *, layer, pages_per_step, pages_per_chunk, page, t_new, t_pad):
    g = pages_per_step
    bsz, n_pages = pt_ref.shape
    steps_per_seq = n_pages // g
    total = bsz * steps_per_seq

    def page_copies(i, slot):
        b = i // steps_per_seq
        first = (i % steps_per_seq) * g
        copies = []
        for j in range(g):
            pid = pt_ref[b, first + j]
            copies.append(pltpu.make_async_copy(
                ckv_hbm.at[layer, pid], cbuf_ref.at[slot, pl.ds(j * page, page)], sem.at[slot]))
            copies.append(pltpu.make_async_copy(
                kpe_hbm.at[layer, pid], pbuf_ref.at[slot, j], sem.at[slot]))
        return copies

    for cp in page_copies(0, 0):
        cp.start()

    def step(i, carry):
        slot = i % 2
        b = i // steps_per_seq
        s_idx = i % steps_per_seq

        @pl.when(i + 1 < total)
        def _():
            for cp in page_copies(i + 1, 1 - slot):
                cp.start()

        pltpu.make_async_copy(cbuf_ref.at[slot], cbuf_ref.at[slot], sem.at[slot]).wait()
        pltpu.make_async_copy(pbuf_ref.at[slot], pbuf_ref.at[slot], sem.at[slot]).wait()
        _attn_sample_step(b, s_idx, slot, steps_per_seq, ql_ref, qp_ref, cn_ref, kn_ref, o_ref, cbuf_ref,
                          pbuf_ref, kbf_ref, pbf_ref, m_ref, l_ref, acc_ref, g=g,
                          pages_per_chunk=pages_per_chunk, page=page, t_new=t_new, t_pad=t_pad)
        return carry

    lax.fori_loop(0, total, step, 0)


def _attn_sample_step(b, s_idx, slot, steps_per_seq, ql_ref, qp_ref, cn_ref, kn_ref, o_ref, cbuf_ref, pbuf_ref,
                      kbf_ref, pbf_ref, m_ref, l_ref, acc_ref, *, g, pages_per_chunk, page, t_new, t_pad):
    ql = ql_ref[b]
    qp = qp_ref[b]
    rows = ql.shape[0]
    contract_last = (((1,), (1,)), ((), ()))

    @pl.when(s_idx == 0)
    def _():
        cn = cn_ref[b].astype(BF16)
        kn = kn_ref[b].astype(BF16)
        s = (lax.dot_general(ql, cn, contract_last, preferred_element_type=F32)
             + lax.dot_general(qp, kn, contract_last, preferred_element_type=F32))
        t_of_row = lax.broadcasted_iota(jnp.int32, (rows, t_pad), 0) % t_pad
        key = lax.broadcasted_iota(jnp.int32, (rows, t_pad), 1)
        ok = (key <= t_of_row) & (key < t_new)
        s = jnp.where(ok, s, NEG_BIG)
        m = jnp.max(s, axis=-1, keepdims=True)
        p = jnp.where(ok, jnp.exp2(s - m), 0.0)
        m_ref[...] = m
        l_ref[...] = jnp.sum(p, axis=-1, keepdims=True)
        acc_ref[...] = jnp.dot(p.astype(BF16), cn, preferred_element_type=F32)

    kbf_ref[...] = cbuf_ref[slot].astype(BF16)
    for j in range(g):
        pbf_ref[:, j * page:(j + 1) * page] = pbuf_ref[slot, j].astype(BF16)

    ck = pages_per_chunk * page
    parts = []
    for c in range(g // pages_per_chunk):
        kb = kbf_ref[c * ck:(c + 1) * ck, :]
        s = (lax.dot_general(ql, kb, contract_last, preferred_element_type=F32)
             + jnp.dot(qp, pbf_ref[:, c * ck:(c + 1) * ck], preferred_element_type=F32))
        mc = jnp.max(s, axis=-1, keepdims=True)
        p = jnp.exp2(s - mc)
        parts.append((mc, jnp.sum(p, axis=-1, keepdims=True),
                      jnp.dot(p.astype(BF16), kb, preferred_element_type=F32)))
    m_old = m_ref[...]
    m_new = m_old
    for mc, _, _ in parts:
        m_new = jnp.maximum(m_new, mc)
    alpha = jnp.exp2(m_old - m_new)
    l = alpha * l_ref[...]
    acc = alpha * acc_ref[...]
    for mc, lc, ac in parts:
        w = jnp.exp2(mc - m_new)
        l = l + w * lc
        acc = acc + w * ac
    m_ref[...] = m_new
    l_ref[...] = l
    acc_ref[...] = acc

    @pl.when(s_idx == steps_per_seq - 1)
    def _():
        o_ref[b] = acc_ref[...] / l_ref[...]


def _attn_sample(page_table, q_lat, q_pe, ckv_new, kpe_new, cache_ckv, cache_kpe_t, layer, *,
                 t_new, t_pad, pages_per_step, pages_per_chunk):
    bsz, rows, kvr = q_lat.shape
    rope = q_pe.shape[2]
    page = cache_ckv.shape[2]
    g = pages_per_step
    assert page_table.shape[1] % g == 0 and g % pages_per_chunk == 0
    vmem = pl.BlockSpec(memory_space=pltpu.VMEM)
    hbm = pl.BlockSpec(memory_space=pl.ANY)
    return pl.pallas_call(
        functools.partial(_attn_sample_kernel, layer=layer, pages_per_step=g, pages_per_chunk=pages_per_chunk,
                          page=page, t_new=t_new, t_pad=t_pad),
        in_specs=[pl.BlockSpec(memory_space=pltpu.SMEM), vmem, vmem, vmem, vmem, hbm, hbm],
        out_specs=vmem,
        out_shape=jax.ShapeDtypeStruct((bsz, rows, kvr), F32),
        scratch_shapes=[
            pltpu.VMEM((2, g * page, kvr), F32),
            pltpu.VMEM((2, g, rope, page), F32),
            pltpu.VMEM((g * page, kvr), BF16),
            pltpu.VMEM((rope, g * page), BF16),
            pltpu.VMEM((rows, 1), F32),
            pltpu.VMEM((rows, 1), F32),
            pltpu.VMEM((rows, kvr), F32),
            pltpu.SemaphoreType.DMA((2,)),
        ],
        compiler_params=pltpu.CompilerParams(vmem_limit_bytes=VMEM_LIMIT),
        name="attn_sample",
    )(page_table, q_lat, q_pe, ckv_new, kpe_new, cache_ckv, cache_kpe_t)


def _olat_kernel(o_ref, w_ref, out_ref, *, n_heads):
    for h in range(n_heads):
        out_ref[:, h * HEAD_SLOT:(h + 1) * HEAD_SLOT] = jnp.dot(
            o_ref[h].astype(BF16), w_ref[h], preferred_element_type=F32).astype(BF16)


def _olat(o_lat_heads, w_uv_pad):
    n_heads, n, _ = o_lat_heads.shape
    return pl.pallas_call(
        functools.partial(_olat_kernel, n_heads=n_heads),
        out_shape=jax.ShapeDtypeStruct((n, n_heads * HEAD_SLOT), BF16),
        compiler_params=pltpu.CompilerParams(vmem_limit_bytes=VMEM_LIMIT),
        name="o_latent",
    )(o_lat_heads, w_uv_pad)


def _out_proj_kernel(x_ref, c_ref, o_ref, wc_ref, wo_ref, g_ref, x1_ref, h_ref):
    mix = (jnp.dot(c_ref[...], wc_ref[...], preferred_element_type=F32)
           + jnp.dot(o_ref[...], wo_ref[...], preferred_element_type=F32))
    x1 = x_ref[...] + mix
    x1_ref[...] = x1
    h_ref[...] = _rms(x1, g_ref[...]).astype(BF16)


def _out_proj(x, c, o_slots, w_c, w_o, g, *, tm):
    n, d = x.shape
    row = lambda i: (i, 0)
    full = lambda i: (0, 0)
    return pl.pallas_call(
        _out_proj_kernel,
        grid=(n // tm,),
        in_specs=[
            pl.BlockSpec((tm, d), row),
            pl.BlockSpec((tm, c.shape[1]), row),
            pl.BlockSpec((tm, o_slots.shape[1]), row),
            pl.BlockSpec(w_c.shape, full),
            pl.BlockSpec(w_o.shape, full),
            pl.BlockSpec((1, d), full),
        ],
        out_specs=[pl.BlockSpec((tm, d), row), pl.BlockSpec((tm, d), row)],
        out_shape=[jax.ShapeDtypeStruct((n, d), F32), jax.ShapeDtypeStruct((n, d), BF16)],
        compiler_params=_cparams("parallel"),
        name="out_proj",
    )(x, c, o_slots, w_c, w_o, g)


def _swiglu_rows(x, wg, wu, wd):
    a = jnp.dot(x, wg, preferred_element_type=F32)
    b = jnp.dot(x, wu, preferred_element_type=F32)
    mid = (a * jax.nn.sigmoid(a) * b).astype(BF16)
    return jnp.dot(mid, wd, preferred_element_type=F32)


def _ffn_kernel(x_ref, res_ref, wg_hbm, wu_hbm, wd_hbm, o_ref, wg_ref, wu_ref, wd_ref, sem, *, row_splits):
    @pl.when(pl.program_id(0) == 0)
    def _():
        copies = [pltpu.make_async_copy(src, dst, sem.at[j]) for j, (src, dst) in enumerate(
            ((wg_hbm, wg_ref), (wu_hbm, wu_ref), (wd_hbm, wd_ref)))]
        for cp in copies:
            cp.start()
        for cp in copies:
            cp.wait()

    hm = o_ref.shape[0] // row_splits
    for r in range(row_splits):
        rows = slice(r * hm, (r + 1) * hm)
        o_ref[rows, :] = res_ref[rows, :] + _swiglu_rows(x_ref[rows, :], wg_ref[...], wu_ref[...], wd_ref[...])


def _ffn(x, res, wg, wu, wd, *, tm):
    n, d = x.shape
    d_ff = wg.shape[1]
    row = lambda i: (i, 0)
    hbm = pl.BlockSpec(memory_space=pl.ANY)
    return pl.pallas_call(
        functools.partial(_ffn_kernel, row_splits=2 if tm % 32 == 0 else 1),
        grid=(n // tm,),
        in_specs=[pl.BlockSpec((tm, d), row), pl.BlockSpec((tm, d), row), hbm, hbm, hbm],
        out_specs=pl.BlockSpec((tm, d), row),
        out_shape=jax.ShapeDtypeStruct((n, d), F32),
        scratch_shapes=[pltpu.VMEM((d, d_ff), BF16), pltpu.VMEM((d, d_ff), BF16), pltpu.VMEM((d_ff, d), BF16),
                        pltpu.SemaphoreType.DMA((3,))],
        compiler_params=_cparams("arbitrary"),
        name="ffn_dense",
    )(x, res, wg, wu, wd)


def _swiglu_kernel(te_ref, nv_ref, tr_ref, x_ref, wg_ref, wu_ref, wd_ref, o_ref, xb_ref, *, row_groups):
    m = pl.program_id(0)
    f = pl.program_id(1)
    tm = o_ref.shape[0]
    hm = tm // row_groups

    @pl.when(f == 0)
    def _():
        o_ref[...] = jnp.zeros_like(o_ref)

    valid = m < nv_ref[0]
    groups_needed = (tr_ref[jnp.minimum(m, nv_ref[0] - 1)] + hm - 1) // hm

    @pl.when(valid & (f == 0))
    def _():
        xb_ref[...] = x_ref[...].astype(BF16)

    @pl.when(valid & (groups_needed == row_groups))
    def _():
        wg = wg_ref[...].astype(BF16)
        wu = wu_ref[...].astype(BF16)
        wd = wd_ref[...].astype(BF16)
        for r in range(row_groups):
            rows = slice(r * hm, (r + 1) * hm)
            o_ref[rows, :] += _swiglu_rows(xb_ref[rows, :], wg, wu, wd)

    @pl.when(valid & (groups_needed < row_groups))
    def _():
        wg = wg_ref[...].astype(BF16)
        wu = wu_ref[...].astype(BF16)
        wd = wd_ref[...].astype(BF16)

        def group(r, carry):
            rows = pl.ds(pl.multiple_of(r * hm, hm), hm)
            o_ref[rows, :] += _swiglu_rows(xb_ref[rows, :], wg, wu, wd)
            return carry

        lax.fori_loop(0, groups_needed, group, 0)


def _swiglu(tile_expert, n_valid, tile_rows, x, wg, wu, wd, *, tm, tf):
    p, d = x.shape
    d_ff = wg.shape[2]
    n_f = d_ff // tf
    n_tiles = p // tm

    def mm(m, nv):
        return jnp.minimum(m, nv[0] - 1)

    def ff(m, f, nv):
        return jnp.where(m < nv[0], f, n_f - 1)

    return pl.pallas_call(
        functools.partial(_swiglu_kernel, row_groups=2 if tm % 32 == 0 else 1),
        grid_spec=pltpu.PrefetchScalarGridSpec(
            num_scalar_prefetch=3,
            grid=(n_tiles, n_f),
            in_specs=[
                pl.BlockSpec((tm, d), lambda m, f, te, nv, tr: (mm(m, nv), 0)),
                pl.BlockSpec((None, d, tf), lambda m, f, te, nv, tr: (te[mm(m, nv)], 0, ff(m, f, nv))),
                pl.BlockSpec((None, d, tf), lambda m, f, te, nv, tr: (te[mm(m, nv)], 0, ff(m, f, nv))),
                pl.BlockSpec((None, tf, d), lambda m, f, te, nv, tr: (te[mm(m, nv)], ff(m, f, nv), 0)),
            ],
            out_specs=pl.BlockSpec((tm, d), lambda m, f, te, nv, tr: (m, 0)),
            scratch_shapes=[pltpu.VMEM((tm, d), BF16)],
        ),
        out_shape=jax.ShapeDtypeStruct((p, d), F32),
        compiler_params=_cparams("arbitrary", "arbitrary"),
        name="swiglu_experts",
    )(tile_expert, n_valid, tile_rows, x, wg, wu, wd)


def _row_copy(src, src_row, dst, dst_row, sem):
    return pltpu.make_async_copy(src.at[pl.ds(src_row, 1)], dst.at[pl.ds(dst_row, 1)], sem)


def _dispatch_kernel(pos_ref, h_ref, xs_in_ref, xs_ref, sem, *, tt):
    del xs_in_ref

    def body(r, carry):
        for k in range(TOP_K):
            _row_copy(h_ref, r, xs_ref, pos_ref[0, TOP_K * r + k], sem).start(priority=k % 2)
        return carry

    lax.fori_loop(0, tt, body, 0, unroll=8)
    for k in range(TOP_K):
        pltpu.make_async_copy(h_ref, xs_ref.at[pl.ds(0, tt)], sem).wait()


def _dispatch(pos_tiles, h, xs):
    n, d = h.shape
    n_tiles = pos_tiles.shape[0]
    tt = n // n_tiles
    return pl.pallas_call(
        functools.partial(_dispatch_kernel, tt=tt),
        grid=(n_tiles,),
        in_specs=[
            pl.BlockSpec((None, 1, TOP_K * tt), lambda i: (i, 0, 0), memory_space=pltpu.SMEM),
            pl.BlockSpec((tt, d), lambda i: (i, 0)),
            pl.BlockSpec(memory_space=pl.ANY),
        ],
        out_specs=pl.BlockSpec(memory_space=pl.ANY),
        out_shape=jax.ShapeDtypeStruct(xs.shape, xs.dtype),
        scratch_shapes=[pltpu.SemaphoreType.DMA],
        input_output_aliases={2: 0},
        compiler_params=_cparams("arbitrary"),
        name="moe_dispatch",
    )(pos_tiles, h, xs)


def _pool_kernel(*refs, tm, pos0, windows, has_halo, n_experts, seqs):
    if has_halo:
        (prev_ref, halo_ref, x_ref, g1_ref, wp_ref, sc_ref, g2_ref, wr_ref,
         x3_ref, h_ref, hn_ref, gate_ref, idx_ref, ext_ref, lv_ref) = refs
    else:
        (prev_ref, x_ref, g1_ref, wp_ref, sc_ref, g2_ref, wr_ref,
         x3_ref, h_ref, hn_ref, gate_ref, idx_ref, ext_ref, lv_ref) = refs
    for sq in range(seqs):
        _pool_sequence(sq, prev_ref, halo_ref if has_halo else None, x_ref, g1_ref, wp_ref, sc_ref, g2_ref,
                       wr_ref, x3_ref, h_ref, hn_ref, gate_ref, idx_ref, ext_ref, lv_ref,
                       tm=tm, pos0=pos0, windows=windows, n_experts=n_experts)


def _pool_sequence(sq, prev_ref, halo_ref, x_ref, g1_ref, wp_ref, sc_ref, g2_ref, wr_ref,
                   x3_ref, h_ref, hn_ref, gate_ref, idx_ref, ext_ref, lv_ref, *, tm, pos0, windows, n_experts):
    t = pl.program_id(1)
    x = x_ref[sq]
    hn = _rms(x, g1_ref[...])
    hn_ref[sq] = hn
    top = SUBLANES
    base = top + POOL_HALO
    end = base + tm
    d = x.shape[1]
    ext_ref[0:top, :] = jnp.zeros((top, d), F32)
    if halo_ref is not None:
        @pl.when(t == 0)
        def _():
            ext_ref[top:base, :] = prev_ref[sq]

        @pl.when(t > 0)
        def _():
            ext_ref[top:base, :] = _rms(halo_ref[sq], g1_ref[...])
    else:
        ext_ref[top:base, :] = prev_ref[sq]
    ext_ref[base:end, :] = hn

    gc = d // len(windows)

    def level_rows(k, lo_row, hi_row, lanes):
        if k == 0:
            return ext_ref[lo_row:hi_row, lanes]
        return lv_ref[k - 1, lo_row:hi_row, lanes]

    for k in range(1, len(windows)):
        lanes = slice((k - 1) * gc, d)
        shift = 1 << (k - 1)
        lv_ref[k - 1, 0:top, lanes] = jnp.zeros((top, d - (k - 1) * gc), F32)
        lv_ref[k - 1, top:end, lanes] = (level_rows(k - 1, top, end, lanes)
                                         + level_rows(k - 1, top - shift, end - shift, lanes))

    pos = pos0 + t * tm + lax.broadcasted_iota(jnp.int32, (tm, 1), 0)
    for gi, w in enumerate(windows):
        lanes = slice(gi * gc, (gi + 1) * gc)
        if gi + 1 < len(windows):
            s = level_rows(gi + 1, base, end, lanes)
        else:
            s = level_rows(gi, base, end, lanes) + level_rows(gi, base - w // 2, end - w // 2, lanes)
        cnt = jnp.minimum(pos + 1, w).astype(F32)
        diff = s / cnt - hn[:, lanes]
        y = jnp.dot(diff.astype(BF16), wp_ref[gi], preferred_element_type=F32)
        x3_ref[sq, :, lanes] = x[:, lanes] + y * sc_ref[:, lanes]

    x3 = x3_ref[sq]
    h2 = _rms(x3, g2_ref[...])
    h_ref[sq] = h2
    logits = jnp.dot(h2, wr_ref[...], preferred_element_type=F32, precision=lax.Precision.HIGHEST)
    lane = lax.broadcasted_iota(jnp.int32, logits.shape, 1)
    logits = jnp.where(lane < n_experts, logits, -jnp.inf)
    m1 = jnp.max(logits, axis=-1, keepdims=True)
    i1 = jnp.min(jnp.where(logits == m1, lane, LANES), axis=-1, keepdims=True)
    rest = jnp.where(lane == i1, -jnp.inf, logits)
    m2 = jnp.max(rest, axis=-1, keepdims=True)
    i2 = jnp.min(jnp.where(rest == m2, lane, LANES), axis=-1, keepdims=True)
    e = jnp.exp(m2 - m1)
    g1 = 1.0 / (1.0 + e)
    g2 = e / (1.0 + e)
    gate_ref[sq] = jnp.where(lane == 0, g1, jnp.where(lane == 1, g2, 0.0))
    idx_ref[sq] = jnp.where(lane == 0, i1, jnp.where(lane == 1, i2, 0))


def _pool_layer(prev_pad, x, g1, w_pool, scale, g2, w_router_pad, *, tm, pos0, n_experts):
    bsz, t_len, d = x.shape
    assert all(w == 2 ** (g + 1) for g, w in enumerate(POOL_WINDOWS)) and POOL_HALO >= max(POOL_WINDOWS) - 1
    n_t = t_len // tm
    has_halo = n_t > 1
    hb = tm // POOL_HALO if has_halo else 1
    n_rows = SUBLANES + POOL_HALO + tm
    seqs = _seqs_per_step(bsz, n_t)
    cur = lambda b, t: (b, t, 0)
    full2 = lambda b, t: (0, 0)
    in_specs = [pl.BlockSpec((seqs, POOL_HALO, d), lambda b, t: (b, 0, 0))]
    args = [prev_pad]
    if has_halo:
        in_specs.append(pl.BlockSpec((seqs, POOL_HALO, d), lambda b, t: (b, jnp.maximum(t * hb - 1, 0), 0)))
        args.append(x)
    in_specs += [
        pl.BlockSpec((seqs, tm, d), cur),
        pl.BlockSpec((1, d), full2),
        pl.BlockSpec(w_pool.shape, lambda b, t: (0, 0, 0)),
        pl.BlockSpec((1, d), full2),
        pl.BlockSpec((1, d), full2),
        pl.BlockSpec(w_router_pad.shape, full2),
    ]
    args += [x, g1, w_pool, scale, g2, w_router_pad]
    return pl.pallas_call(
        functools.partial(_pool_kernel, tm=tm, pos0=pos0, windows=POOL_WINDOWS, has_halo=has_halo,
                          n_experts=n_experts, seqs=seqs),
        grid=(bsz // seqs, n_t),
        in_specs=in_specs,
        out_specs=[
            pl.BlockSpec((seqs, tm, d), cur),
            pl.BlockSpec((seqs, tm, d), cur),
            pl.BlockSpec((seqs, tm, d), cur),
            pl.BlockSpec((seqs, tm, LANES), cur),
            pl.BlockSpec((seqs, tm, LANES), cur),
        ],
        out_shape=[
            jax.ShapeDtypeStruct((bsz, t_len, d), F32),
            jax.ShapeDtypeStruct((bsz, t_len, d), F32),
            jax.ShapeDtypeStruct((bsz, t_len, d), F32),
            jax.ShapeDtypeStruct((bsz, t_len, LANES), F32),
            jax.ShapeDtypeStruct((bsz, t_len, LANES), jnp.int32),
        ],
        scratch_shapes=[pltpu.VMEM((n_rows, d), F32), pltpu.VMEM((len(POOL_WINDOWS) - 1, n_rows, d), F32)],
        compiler_params=_cparams("parallel", "parallel"),
        name="pool_layer",
    )(*args)


def _combine_kernel(pos_cur_ref, pos_nxt_ref, x_ref, gate_ref, g_ref, ys_ref, y_ref, buf_ref, sem,
                    *, tt, apply_norm):
    i = pl.program_id(0)
    n = pl.num_programs(0)
    slot = i % 2

    def fetch(pos_ref, s):
        def body(r, carry):
            for k in range(TOP_K):
                _row_copy(ys_ref, pos_ref[0, TOP_K * r + k], buf_ref.at[s, k], r,
                          sem.at[s]).start(priority=k % 2)
            return carry
        lax.fori_loop(0, tt, body, 0, unroll=8)

    @pl.when(i == 0)
    def _():
        fetch(pos_cur_ref, 0)

    @pl.when(i + 1 < n)
    def _():
        fetch(pos_nxt_ref, 1 - slot)

    for k in range(TOP_K):
        pltpu.make_async_copy(ys_ref.at[pl.ds(0, tt)], buf_ref.at[slot, k], sem.at[slot]).wait()
    moe = gate_ref[:, 0:1] * buf_ref[slot, 0]
    for k in range(1, TOP_K):
        moe = moe + gate_ref[:, k:k + 1] * buf_ref[slot, k]
    y = x_ref[...] + moe
    y_ref[...] = _rms(y, g_ref[...]) if apply_norm else y


def _combine(pos_tiles, x, gates, g, ys, *, apply_norm):
    n, d = x.shape
    n_tiles = pos_tiles.shape[0]
    tt = n // n_tiles
    pos_spec = lambda fn: pl.BlockSpec((None, 1, TOP_K * tt), fn, memory_space=pltpu.SMEM)
    return pl.pallas_call(
        functools.partial(_combine_kernel, tt=tt, apply_norm=apply_norm),
        grid=(n_tiles,),
        in_specs=[
            pos_spec(lambda i: (i, 0, 0)),
            pos_spec(lambda i: (jnp.minimum(i + 1, n_tiles - 1), 0, 0)),
            pl.BlockSpec((tt, d), lambda i: (i, 0)),
            pl.BlockSpec((tt, LANES), lambda i: (i, 0)),
            pl.BlockSpec((1, d), lambda i: (0, 0)),
            pl.BlockSpec(memory_space=pl.ANY),
        ],
        out_specs=pl.BlockSpec((tt, d), lambda i: (i, 0)),
        out_shape=jax.ShapeDtypeStruct((n, d), F32),
        scratch_shapes=[pltpu.VMEM((2, TOP_K, tt, d), F32), pltpu.SemaphoreType.DMA((2,))],
        compiler_params=_cparams("arbitrary"),
        name="moe_combine",
    )(pos_tiles, pos_tiles, x, gates, g, ys)


def _final_kernel(x_ref, g_ref, y_ref):
    y_ref[...] = _rms(x_ref[...], g_ref[...])


def _final(x, g, *, tm):
    n, d = x.shape
    row = lambda i: (i, 0)
    return pl.pallas_call(
        _final_kernel,
        grid=(n // tm,),
        in_specs=[pl.BlockSpec((tm, d), row), pl.BlockSpec((1, d), lambda i: (0, 0))],
        out_specs=pl.BlockSpec((tm, d), row),
        out_shape=jax.ShapeDtypeStruct((n, d), F32),
        compiler_params=_cparams("parallel"),
        name="final_norm",
    )(x, g)


def _swap_halves(w):
    half = w.shape[-1] // 2
    return jnp.concatenate([w[..., half:], w[..., :half]], axis=-1)


def _slot(parts, total=HEAD_SLOT):
    used = sum(p.shape[-1] for p in parts)
    pad = jnp.zeros(parts[0].shape[:-1] + (total - used,), parts[0].dtype)
    return jnp.concatenate(list(parts) + [pad], axis=-1)


def _prep_even_weights(w_in, w_uq, w_ukv, w_out, *, dc, qr, kvr):
    d = w_in.shape[0]
    body = w_in[:, :2 * dc + qr + kvr]
    w_kpe = w_in[:, 2 * dc + qr + kvr:]
    z_nope = jnp.zeros((d, NOPE_DIM), w_in.dtype)
    w_in_ext = jnp.concatenate(
        [body, _slot([z_nope, w_kpe]), _slot([z_nope, _swap_halves(w_kpe)])], axis=-1).astype(BF16)

    uq = w_uq.reshape(qr, N_HEADS, NOPE_DIM + ROPE_DIM)
    uq_nope, uq_pe = uq[..., :NOPE_DIM], uq[..., NOPE_DIM:]
    qa = _slot([uq_nope, uq_pe]).reshape(qr, N_HEADS * HEAD_SLOT)
    qb = _slot([jnp.zeros_like(uq_nope), _swap_halves(uq_pe)]).reshape(qr, N_HEADS * HEAD_SLOT)
    w_uq_ext = jnp.concatenate([qa, qb], axis=-1).astype(BF16)

    ukv = w_ukv.reshape(kvr, N_HEADS, NOPE_DIM + V_DIM)
    uk, uv = ukv[..., :NOPE_DIM], ukv[..., NOPE_DIM:]
    w_kv_ext = jnp.concatenate([_slot([uk]).reshape(kvr, N_HEADS * HEAD_SLOT),
                                _slot([uv]).reshape(kvr, N_HEADS * HEAD_SLOT)], axis=-1).astype(BF16)
    w_q2lat = jnp.concatenate(
        [jnp.transpose(uk, (1, 2, 0)),
         jnp.zeros((N_HEADS, HEAD_SLOT - NOPE_DIM, kvr), w_ukv.dtype)], axis=1).astype(BF16)
    w_uv_pad = _slot([jnp.transpose(uv, (1, 0, 2))]).astype(BF16)

    w_c = w_out[:dc].astype(BF16)
    wo = w_out[dc:].reshape(N_HEADS, V_DIM, d)
    w_o = jnp.concatenate([wo, jnp.zeros((N_HEADS, HEAD_SLOT - V_DIM, d), w_out.dtype)],
                          axis=1).reshape(N_HEADS * HEAD_SLOT, d).astype(BF16)
    return w_in_ext, w_uq_ext, w_kv_ext, w_q2lat, w_uv_pad, w_c, w_o


def _rope_tables(pos):
    half = ROPE_DIM // 2
    inv_freq = jnp.power(ROPE_THETA, -jnp.arange(half, dtype=F32) / half)
    ang = pos.astype(F32)[:, None] * inv_freq[None, :]
    cos, sin = jnp.cos(ang), jnp.sin(ang)
    n = pos.shape[0]
    z_nope = jnp.zeros((n, NOPE_DIM), F32)
    cos_k = _slot([z_nope, cos, cos])
    sin_k = _slot([z_nope, -sin, sin])
    cos_q = _slot([jnp.ones((n, NOPE_DIM), F32), cos, cos]) * (ATTN_SCALE * LOG2E)
    sin_q = sin_k * (ATTN_SCALE * LOG2E)
    return cos_k, sin_k, cos_q, sin_q


def _even_layer(x, conv_prev, pos, attend, P, layer, i):
    bsz, t_len, d = x.shape
    n = bsz * t_len
    dc = P['conv_w'].shape[2]
    qr = P['q_norm'].shape[1]
    kvr = P['kv_norm'].shape[1]
    k_taps = P['conv_w'].shape[1]
    tm = _row_tile(n, 512) if t_len >= 512 else n
    w_in_ext, w_uq_ext, w_kv_ext, w_q2lat, w_uv_pad, w_c, w_o = _prep_even_weights(
        P['w_in'][i], P['w_uq'][i], P['w_ukv'][i], P['w_out'][i], dc=dc, qr=qr, kvr=kvr)
    cos_k, sin_k, cos_q, sin_q = _rope_tables(pos)
    if t_len < tm:
        reps = tm // t_len
        cos_k, sin_k, cos_q, sin_q = (jnp.tile(a, (reps, 1)) for a in (cos_k, sin_k, cos_q, sin_q))

    xf = x.reshape(n, d)
    u, qn, ckv, kpe_slot = _in_proj(
        xf, P['norm_mix'][layer][None], w_in_ext, P['q_norm'][i][None], P['kv_norm'][i][None],
        cos_k, sin_k, dc=dc, qr=qr, kvr=kvr, tm=tm)

    u3 = u.reshape(bsz, t_len, dc)
    prev_pad = jnp.pad(conv_prev.astype(F32), ((0, 0), (CONV_HALO - (k_taps - 1), 0), (0, 0)))
    c = _conv_branch(prev_pad, u3, P['conv_w'][i], P['conv_b'][i][None], P['conv_ln_g'][i][None],
                     P['conv_ln_b'][i][None], tc=min(t_len, 512))
    conv_state = jnp.concatenate([conv_prev.astype(F32), u3], axis=1)[:, -(k_taps - 1):]

    q_slots = _q_proj(qn, w_uq_ext, cos_q, sin_q, tm=tm)
    o_slots = attend(q_slots, ckv, kpe_slot, w_kv_ext, w_q2lat, w_uv_pad, tm)

    x1, h_ffn = _out_proj(xf, c.reshape(n, dc), o_slots, w_c, w_o, P['norm_ffn'][layer][None], tm=tm)

    x2 = _ffn(h_ffn, x1, P['ffn_w_gate'][i].astype(BF16), P['ffn_w_up'][i].astype(BF16),
              P['ffn_w_down'][i].astype(BF16), tm=tm)
    kpe = kpe_slot[:, NOPE_DIM:NOPE_DIM + ROPE_DIM]
    return (x2.reshape(bsz, t_len, d), conv_state, ckv.reshape(bsz, t_len, kvr),
            kpe.reshape(bsz, t_len, ROPE_DIM))


def _make_attend_prompt(bsz, t_len):
    def attend(q_slots, ckv, kpe_slot, w_kv_ext, w_q2lat, w_uv_pad, tm):
        k_slots, v_slots = _kv_proj(ckv, kpe_slot, w_kv_ext, tm=tm)
        return _attn_prompt(q_slots, k_slots, v_slots, bsz=bsz, t_len=t_len, tq=min(t_len, 512))
    return attend


def _make_attend_sample(bsz, t_len, cache_ckv, cache_kpe_t, page_table, i):
    t_pad = 8

    def attend(q_slots, ckv, kpe_slot, w_kv_ext, w_q2lat, w_uv_pad, tm):
        n = bsz * t_len
        kvr = ckv.shape[1]
        q_lat = _qlat(q_slots, w_q2lat)
        q_lat = q_lat.reshape(N_HEADS, bsz, t_len, kvr)
        q_lat = jnp.pad(q_lat, ((0, 0), (0, 0), (0, t_pad - t_len), (0, 0)))
        q_lat = jnp.transpose(q_lat, (1, 0, 2, 3)).reshape(bsz, N_HEADS * t_pad, kvr)
        q_pe = q_slots.reshape(bsz, t_len, N_HEADS, HEAD_SLOT)[..., NOPE_DIM:NOPE_DIM + ROPE_DIM]
        q_pe = jnp.pad(jnp.transpose(q_pe, (0, 2, 1, 3)), ((0, 0), (0, 0), (0, t_pad - t_len), (0, 0)))
        q_pe = q_pe.reshape(bsz, N_HEADS * t_pad, ROPE_DIM)
        pad_t = ((0, 0), (0, t_pad - t_len), (0, 0))
        ckv_new = jnp.pad(ckv.reshape(bsz, t_len, kvr), pad_t)
        kpe_new = jnp.pad(kpe_slot[:, NOPE_DIM:NOPE_DIM + ROPE_DIM].reshape(bsz, t_len, ROPE_DIM), pad_t)
        o_lat = _attn_sample(page_table, q_lat, q_pe, ckv_new, kpe_new, cache_ckv, cache_kpe_t, i,
                             t_new=t_len, t_pad=t_pad, pages_per_step=64,
                             pages_per_chunk=64)
        o_lat = o_lat.reshape(bsz, N_HEADS, t_pad, kvr)[:, :, :t_len]
        o_lat = jnp.transpose(o_lat, (1, 0, 2, 3)).reshape(N_HEADS, n, kvr)
        return _olat(o_lat, w_uv_pad)
    return attend


def _odd_layer_mixer(x, pool_prev, pos0, P, layer, i):
    bsz, t_len, d = x.shape
    n_experts = P['router_w'].shape[2]
    ctx = max(POOL_WINDOWS) - 1
    prev_pad = jnp.pad(pool_prev.astype(F32), ((0, 0), (POOL_HALO - ctx, 0), (0, 0)))
    w_router_pad = jnp.pad(P['router_w'][i], ((0, 0), (0, LANES - n_experts)))
    x3, h_moe, hn, gates, ids = _pool_layer(
        prev_pad, x, P['norm_mix'][layer][None], P['pool_w'][i].astype(BF16), P['pool_scale'][i][None],
        P['norm_ffn'][layer][None], w_router_pad, tm=min(t_len, 512), pos0=pos0, n_experts=n_experts)
    pool_state = jnp.concatenate([pool_prev.astype(F32), hn], axis=1)[:, -ctx:]
    n = bsz * t_len
    return (x3.reshape(n, d), h_moe.reshape(n, d), pool_state, gates.reshape(n, LANES),
            ids.reshape(n, LANES)[:, :TOP_K])


def _moe_plan(ids, n_experts, tm):
    n = ids.shape[0]
    flat = ids.reshape(-1)
    experts = jnp.arange(n_experts, dtype=jnp.int32)
    onehot = (flat[:, None] == experts[None, :]).astype(jnp.int32)
    csum = jnp.cumsum(onehot, axis=0)
    counts = csum[-1]
    rank = jnp.sum(csum * onehot, axis=1) - 1
    padded = ((counts + tm - 1) // tm) * tm
    ends = jnp.cumsum(padded)
    starts = ends - padded
    pos = jnp.sum(starts[None, :] * onehot, axis=1) + rank
    n_tiles = (n * TOP_K + tm - 1) // tm + n_experts
    n_valid = (ends[-1] // tm).astype(jnp.int32)
    tile_start = jnp.arange(n_tiles, dtype=jnp.int32) * tm
    tile_expert = jnp.sum((ends[None, :] <= tile_start[:, None]).astype(jnp.int32), axis=1)
    tile_expert = jnp.minimum(tile_expert, n_experts - 1)
    group_end = starts + counts
    tile_onehot = (tile_expert[:, None] == experts[None, :]).astype(jnp.int32)
    tile_rows = jnp.clip(jnp.sum(group_end[None, :] * tile_onehot, axis=1) - tile_start, 0, tm)
    return pos.reshape(n, TOP_K), tile_expert, n_valid.reshape(1), tile_rows.astype(jnp.int32), n_tiles


def _moe_layer(groups, P, i, g_final, *, tm, tt):
    d = groups[0][0].shape[1]
    n_experts = P['router_w'].shape[2]
    ids = jnp.concatenate([g[3] for g in groups], axis=0)
    pos, tile_expert, n_valid, tile_rows, n_tiles = _moe_plan(ids, n_experts, tm)
    xs = jnp.zeros((n_tiles * tm, d), F32)
    pos_tiles = []
    lo = 0
    for x3, h, _, _ in groups:
        n = x3.shape[0]
        t = min(tt, n)
        pt = pos[lo:lo + n].reshape(n // t, 1, TOP_K * t)
        xs = _dispatch(pt, h, xs)
        pos_tiles.append(pt)
        lo += n
    ys = _swiglu(tile_expert, n_valid, tile_rows, xs, P['moe_w_gate'][i], P['moe_w_up'][i],
                 P['moe_w_down'][i], tm=tm, tf=512)
    gain = g_final if g_final is not None else jnp.ones((1, d), F32)
    return [_combine(pt, x3, gates, gain, ys, apply_norm=g_final is not None)
            for pt, (x3, _, gates, _) in zip(pos_tiles, groups)]


def kernel(x_prompt, x_sample, cache_ckv, cache_kpe, page_table, state_conv, state_pool, norm_mix, norm_ffn,
           norm_final, w_in, conv_w, conv_b, conv_ln_g, conv_ln_b, q_norm, w_uq, kv_norm, w_ukv, w_out,
           ffn_w_gate, ffn_w_up, ffn_w_down, pool_w, pool_scale, router_w, moe_w_gate, moe_w_up, moe_w_down):
    P = {'norm_mix': norm_mix, 'norm_ffn': norm_ffn, 'norm_final': norm_final, 'w_in': w_in,
         'conv_w': conv_w, 'conv_b': conv_b, 'conv_ln_g': conv_ln_g, 'conv_ln_b': conv_ln_b,
         'q_norm': q_norm, 'w_uq': w_uq, 'kv_norm': kv_norm, 'w_ukv': w_ukv, 'w_out': w_out,
         'ffn_w_gate': ffn_w_gate, 'ffn_w_up': ffn_w_up, 'ffn_w_down': ffn_w_down,
         'pool_w': pool_w, 'pool_scale': pool_scale, 'router_w': router_w,
         'moe_w_gate': moe_w_gate, 'moe_w_up': moe_w_up, 'moe_w_down': moe_w_down}
    depth = norm_mix.shape[0]
    d = x_prompt.shape[2]
    bp, tp, _ = x_prompt.shape
    bs, ts, _ = x_sample.shape
    past = page_table.shape[1] * cache_ckv.shape[2]
    k_taps = conv_w.shape[1]
    ctx = max(POOL_WINDOWS) - 1
    pos_p = jnp.arange(tp, dtype=jnp.int32)
    pos_s = past + jnp.arange(ts, dtype=jnp.int32)
    cache_kpe_t = jnp.swapaxes(cache_kpe, 2, 3)

    xp, xs = x_prompt, x_sample
    outs_p = {'ckv': [], 'kpe': [], 'conv': [], 'pool': []}
    outs_s = {'ckv': [], 'kpe': [], 'conv': [], 'pool': []}
    for layer in range(depth):
        i = layer // 2
        last = layer == depth - 1
        if layer % 2 == 0:
            conv0 = jnp.zeros((bp, k_taps - 1, conv_w.shape[2]), F32)
            xp, cst, ckv, kpe = _even_layer(xp, conv0, pos_p, _make_attend_prompt(bp, tp), P, layer, i)
            outs_p['conv'].append(cst); outs_p['ckv'].append(ckv); outs_p['kpe'].append(kpe)
            xs, cst, ckv, kpe = _even_layer(
                xs, state_conv[i], pos_s, _make_attend_sample(bs, ts, cache_ckv, cache_kpe_t, page_table, i),
                P, layer, i)
            outs_s['conv'].append(cst); outs_s['ckv'].append(ckv); outs_s['kpe'].append(kpe)
            if last:
                xp = _final(xp.reshape(bp * tp, d), norm_final[None], tm=_row_tile(bp * tp, 512)).reshape(xp.shape)
                xs = _final(xs.reshape(bs * ts, d), norm_final[None], tm=_row_tile(bs * ts, 512)).reshape(xs.shape)
        else:
            pool0 = jnp.zeros((bp, ctx, d), F32)
            x3p, hp, pst_p, gp, ip = _odd_layer_mixer(xp, pool0, 0, P, layer, i)
            x3s, hs, pst_s, gs, is_ = _odd_layer_mixer(xs, state_pool[i], past, P, layer, i)
            outs_p['pool'].append(pst_p); outs_s['pool'].append(pst_s)
            yp, ysm = _moe_layer([(x3p, hp, gp, ip), (x3s, hs, gs, is_)], P, i,
                                 norm_final[None] if last else None, tm=1024, tt=512)
            xp, xs = yp.reshape(xp.shape), ysm.reshape(xs.shape)

    def stack(lst, shape_if_empty):
        return jnp.stack(lst) if lst else jnp.zeros(shape_if_empty, F32)

    return (xp, xs,
            jnp.stack(outs_p['ckv']), jnp.stack(outs_p['kpe']), jnp.stack(outs_p['conv']),
            stack(outs_p['pool'], (0, bp, ctx, d)),
            jnp.stack(outs_s['ckv']), jnp.stack(outs_s['kpe']), jnp.stack(outs_s['conv']),
            stack(outs_s['pool'], (0, bs, ctx, d)))
```

```python
import functools

import jax
import jax.numpy as jnp
from jax import lax
from jax.experimental import pallas as pl
from jax.experimental.pallas import tpu as pltpu

F32 = jnp.float32
BF16 = jnp.bfloat16

N_HEADS = 8
NOPE_DIM = 64
ROPE_DIM = 32
V_DIM = 64
ROPE_THETA = 10000.0
ATTN_SCALE = (NOPE_DIM + ROPE_DIM) ** -0.5
LOG2E = 1.4426950408889634
POOL_WINDOWS = (2, 4, 8, 16)
TOP_K = 2
EPS = 1e-6

HEAD_SLOT = 128
LANES = 128
SUBLANES = 8
NEG_BIG = -1e30
VMEM_LIMIT = 56 * 1024 * 1024

PROJ_TILE_ROWS = 1024
FFN_TILE_ROWS = 512
CONV_HALO = 32
POOL_HALO = 16


def _cparams(*semantics):
    return pltpu.CompilerParams(dimension_semantics=semantics, vmem_limit_bytes=VMEM_LIMIT)


def _rms(x, g):
    return x * lax.rsqrt(jnp.mean(x * x, axis=-1, keepdims=True) + EPS) * g


def _row_tile(n, want):
    if n <= want:
        return n
    t = want
    while t >= 8:
        if n % t == 0 and t % 8 == 0:
            return t
        t -= 8
    return n


def _in_proj_kernel(x_ref, g_ref, w_ref, qg_ref, kvg_ref, cos_ref, sin_ref,
                    u_ref, qn_ref, ckv_ref, kpe_ref, *, dc, qr, kvr):
    h = _rms(x_ref[...], g_ref[...])
    proj = jnp.dot(h.astype(BF16), w_ref[...], preferred_element_type=F32)
    a = proj[:, :dc]
    gate = proj[:, dc:2 * dc]
    u_ref[...] = a * jax.nn.sigmoid(gate)
    o = 2 * dc
    qn_ref[...] = _rms(proj[:, o:o + qr], qg_ref[...]).astype(BF16)
    o += qr
    ckv_ref[...] = _rms(proj[:, o:o + kvr], kvg_ref[...])
    o += kvr
    kpe_ref[...] = (proj[:, o:o + HEAD_SLOT] * cos_ref[...]
                    + proj[:, o + HEAD_SLOT:o + 2 * HEAD_SLOT] * sin_ref[...])


def _in_proj(x, g, w_ext, qg, kvg, cos_k, sin_k, *, dc, qr, kvr, tm):
    n, d = x.shape
    n_tab = cos_k.shape[0] // tm
    row = lambda i: (i, 0)
    full = lambda i: (0, 0)
    tab = lambda i: (i % n_tab, 0)
    return pl.pallas_call(
        functools.partial(_in_proj_kernel, dc=dc, qr=qr, kvr=kvr),
        grid=(n // tm,),
        in_specs=[
            pl.BlockSpec((tm, d), row),
            pl.BlockSpec((1, d), full),
            pl.BlockSpec(w_ext.shape, full),
            pl.BlockSpec((1, qr), full),
            pl.BlockSpec((1, kvr), full),
            pl.BlockSpec((tm, HEAD_SLOT), tab),
            pl.BlockSpec((tm, HEAD_SLOT), tab),
        ],
        out_specs=[
            pl.BlockSpec((tm, dc), row),
            pl.BlockSpec((tm, qr), row),
            pl.BlockSpec((tm, kvr), row),
            pl.BlockSpec((tm, HEAD_SLOT), row),
        ],
        out_shape=[
            jax.ShapeDtypeStruct((n, dc), F32),
            jax.ShapeDtypeStruct((n, qr), BF16),
            jax.ShapeDtypeStruct((n, kvr), F32),
            jax.ShapeDtypeStruct((n, HEAD_SLOT), F32),
        ],
        compiler_params=_cparams("parallel"),
        name="in_proj",
    )(x, g, w_ext, qg, kvg, cos_k, sin_k)


def _conv_kernel(*refs, tc, k_taps, rows_per_chunk, has_halo, seqs):
    if has_halo:
        prev_ref, halo_ref, u_ref, w_ref, b_ref, lg_ref, lb_ref, c_ref, ext_ref, sh_ref = refs
    else:
        prev_ref, u_ref, w_ref, b_ref, lg_ref, lb_ref, c_ref, ext_ref, sh_ref = refs
    n_rows = CONV_HALO + tc
    first = CONV_HALO - (k_taps - 1)

    def window(start, n):
        q, s = divmod(start, SUBLANES)
        if s == 0:
            return ext_ref[start:start + n, :]
        return sh_ref[s - 1, q * SUBLANES:q * SUBLANES + n, :]

    for sq in range(seqs):
        if has_halo:
            t = pl.program_id(1)

            @pl.when(t == 0)
            def _():
                ext_ref[0:CONV_HALO, :] = prev_ref[sq]

            @pl.when(t > 0)
            def _():
                ext_ref[0:CONV_HALO, :] = halo_ref[sq]
        else:
            ext_ref[0:CONV_HALO, :] = prev_ref[sq]
        ext_ref[CONV_HALO:n_rows, :] = u_ref[sq]

        for s in range(1, SUBLANES):
            sh_ref[s - 1, 0:n_rows - SUBLANES, :] = ext_ref[s:s + n_rows - SUBLANES, :]

        for c0 in range(0, tc, rows_per_chunk):
            rc = min(rows_per_chunk, tc - c0)
            acc = w_ref[0:1, :] * window(first + c0, rc)
            for k in range(1, k_taps):
                acc = acc + w_ref[k:k + 1, :] * window(first + c0 + k, rc)
            acc = acc + b_ref[...]
            mu = jnp.mean(acc, axis=-1, keepdims=True)
            xc = acc - mu
            y = xc * lax.rsqrt(jnp.mean(xc * xc, axis=-1, keepdims=True) + EPS)
            y = y * lg_ref[...] + lb_ref[...]
            c_ref[sq, c0:c0 + rc, :] = (y * jax.nn.sigmoid(y)).astype(BF16)


def _seqs_per_step(bsz, n_t, want=8):
    if n_t > 1:
        return 1
    g = min(want, bsz)
    while bsz % g:
        g -= 1
    return g


def _conv_branch(prev_pad, u, w, b, lg, lb, *, tc):
    bsz, t_len, c = u.shape
    k_taps = w.shape[0]
    n_t = t_len // tc
    has_halo = n_t > 1
    hb = tc // CONV_HALO if has_halo else 1
    seqs = _seqs_per_step(bsz, n_t)
    in_specs = [pl.BlockSpec((seqs, CONV_HALO, c), lambda bi, ti: (bi, 0, 0))]
    args = [prev_pad]
    if has_halo:
        in_specs.append(pl.BlockSpec((seqs, CONV_HALO, c),
                                     lambda bi, ti: (bi, jnp.maximum(ti * hb - 1, 0), 0)))
        args.append(u)
    in_specs += [
        pl.BlockSpec((seqs, tc, c), lambda bi, ti: (bi, ti, 0)),
        pl.BlockSpec((k_taps, c), lambda bi, ti: (0, 0)),
        pl.BlockSpec((1, c), lambda bi, ti: (0, 0)),
        pl.BlockSpec((1, c), lambda bi, ti: (0, 0)),
        pl.BlockSpec((1, c), lambda bi, ti: (0, 0)),
    ]
    args += [u, w, b, lg, lb]
    return pl.pallas_call(
        functools.partial(_conv_kernel, tc=tc, k_taps=k_taps, rows_per_chunk=64, has_halo=has_halo, seqs=seqs),
        grid=(bsz // seqs, n_t),
        in_specs=in_specs,
        out_specs=pl.BlockSpec((seqs, tc, c), lambda bi, ti: (bi, ti, 0)),
        out_shape=jax.ShapeDtypeStruct((bsz, t_len, c), BF16),
        scratch_shapes=[pltpu.VMEM((CONV_HALO + tc, c), F32),
                        pltpu.VMEM((SUBLANES - 1, CONV_HALO + tc, c), F32)],
        compiler_params=_cparams("parallel", "parallel"),
        name="conv_branch",
    )(*args)


def _q_proj_kernel(qn_ref, w_ref, cos_ref, sin_ref, q_ref, *, n_heads):
    proj = jnp.dot(qn_ref[...], w_ref[...], preferred_element_type=F32)
    half = n_heads * HEAD_SLOT
    cos = cos_ref[...]
    sin = sin_ref[...]
    for h in range(n_heads):
        a = proj[:, h * HEAD_SLOT:(h + 1) * HEAD_SLOT]
        b = proj[:, half + h * HEAD_SLOT:half + (h + 1) * HEAD_SLOT]
        q_ref[:, h * HEAD_SLOT:(h + 1) * HEAD_SLOT] = (a * cos + b * sin).astype(BF16)


def _q_proj(qn, w_ext, cos_q, sin_q, *, tm):
    n, qr = qn.shape
    n_tab = cos_q.shape[0] // tm
    width = N_HEADS * HEAD_SLOT
    return pl.pallas_call(
        functools.partial(_q_proj_kernel, n_heads=N_HEADS),
        grid=(n // tm,),
        in_specs=[
            pl.BlockSpec((tm, qr), lambda i: (i, 0)),
            pl.BlockSpec(w_ext.shape, lambda i: (0, 0)),
            pl.BlockSpec((tm, HEAD_SLOT), lambda i: (i % n_tab, 0)),
            pl.BlockSpec((tm, HEAD_SLOT), lambda i: (i % n_tab, 0)),
        ],
        out_specs=pl.BlockSpec((tm, width), lambda i: (i, 0)),
        out_shape=jax.ShapeDtypeStruct((n, width), BF16),
        compiler_params=_cparams("parallel"),
        name="q_proj",
    )(qn, w_ext, cos_q, sin_q)


def _kv_proj_kernel(ckv_ref, kpe_ref, w_ref, k_ref, v_ref, *, n_heads):
    proj = jnp.dot(ckv_ref[...].astype(BF16), w_ref[...], preferred_element_type=F32)
    half = n_heads * HEAD_SLOT
    kpe = kpe_ref[...]
    for h in range(n_heads):
        k_ref[:, h * HEAD_SLOT:(h + 1) * HEAD_SLOT] = (
            proj[:, h * HEAD_SLOT:(h + 1) * HEAD_SLOT] + kpe).astype(BF16)
    v = proj[:, half:]
    lane = lax.broadcasted_iota(jnp.int32, v.shape, 1)
    v_ref[...] = jnp.where(lane % HEAD_SLOT == V_DIM, 1.0, v).astype(BF16)


def _kv_proj(ckv, kpe_slot, w_ext, *, tm):
    n, kvr = ckv.shape
    width = N_HEADS * HEAD_SLOT
    return pl.pallas_call(
        functools.partial(_kv_proj_kernel, n_heads=N_HEADS),
        grid=(n // tm,),
        in_specs=[
            pl.BlockSpec((tm, kvr), lambda i: (i, 0)),
            pl.BlockSpec((tm, HEAD_SLOT), lambda i: (i, 0)),
            pl.BlockSpec(w_ext.shape, lambda i: (0, 0)),
        ],
        out_specs=[pl.BlockSpec((tm, width), lambda i: (i, 0)),
                   pl.BlockSpec((tm, width), lambda i: (i, 0))],
        out_shape=[jax.ShapeDtypeStruct((n, width), BF16),
                   jax.ShapeDtypeStruct((n, width), BF16)],
        compiler_params=_cparams("parallel"),
        name="kv_proj",
    )(ckv, kpe_slot, w_ext)


def _attn_prompt_kernel(q_ref, k_ref, v_ref, o_ref, *, tq, n_heads):
    qi = pl.program_id(1)
    row = lax.broadcasted_iota(jnp.int32, (tq, tq), 0)
    col = lax.broadcasted_iota(jnp.int32, (tq, tq), 1)
    causal = col <= row

    head_lanes = [slice(h * HEAD_SLOT, (h + 1) * HEAD_SLOT) for h in range(n_heads)]

    def step(kt, carry, masked):
        ms, accs = carry
        start = pl.multiple_of(kt * tq, tq)
        new_ms, new_accs = [], []
        for h, lanes in enumerate(head_lanes):
            k = k_ref[pl.ds(start, tq), lanes]
            v = v_ref[pl.ds(start, tq), lanes]
            s = lax.dot_general(q_ref[:, lanes], k, (((1,), (1,)), ((), ())), preferred_element_type=F32)
            if masked:
                s = jnp.where(causal, s, NEG_BIG)
            m_new = jnp.maximum(ms[h], jnp.max(s, axis=-1, keepdims=True))
            p = jnp.exp2(s - m_new)
            new_accs.append(jnp.exp2(ms[h] - m_new) * accs[h]
                            + jnp.dot(p.astype(BF16), v, preferred_element_type=F32))
            new_ms.append(m_new)
        return tuple(new_ms), tuple(new_accs)

    init = (tuple(jnp.full((tq, 1), NEG_BIG, F32) for _ in head_lanes),
            tuple(jnp.zeros((tq, HEAD_SLOT), F32) for _ in head_lanes))
    carry = lax.fori_loop(0, qi, functools.partial(step, masked=False), init)
    _, accs = step(qi, carry, True)
    for lanes, acc in zip(head_lanes, accs):
        o_ref[:, lanes] = (acc / acc[:, V_DIM:V_DIM + 1]).astype(BF16)


def _attn_prompt(q, k, v, *, bsz, t_len, tq):
    width = q.shape[1]
    nq = t_len // tq
    return pl.pallas_call(
        functools.partial(_attn_prompt_kernel, tq=tq, n_heads=N_HEADS),
        grid=(bsz, nq),
        in_specs=[
            pl.BlockSpec((tq, width), lambda b, i: (b * nq + i, 0)),
            pl.BlockSpec((t_len, width), lambda b, i: (b, 0)),
            pl.BlockSpec((t_len, width), lambda b, i: (b, 0)),
        ],
        out_specs=pl.BlockSpec((tq, width), lambda b, i: (b * nq + i, 0)),
        out_shape=jax.ShapeDtypeStruct(q.shape, BF16),
        compiler_params=_cparams("parallel", "arbitrary"),
        name="attn_prompt",
    )(q, k, v)


def _qlat_kernel(q_ref, w_ref, o_ref, *, n_heads):
    for h in range(n_heads):
        o_ref[h] = jnp.dot(q_ref[:, h * HEAD_SLOT:(h + 1) * HEAD_SLOT], w_ref[h],
                           preferred_element_type=F32).astype(BF16)


def _qlat(q_slots, w_q2lat):
    n = q_slots.shape[0]
    n_heads, _, kvr = w_q2lat.shape
    return pl.pallas_call(
        functools.partial(_qlat_kernel, n_heads=n_heads),
        out_shape=jax.ShapeDtypeStruct((n_heads, n, kvr), BF16),
        compiler_params=pltpu.CompilerParams(vmem_limit_bytes=VMEM_LIMIT),
        name="q_latent",
    )(q_slots, w_q2lat)


def _attn_sample_kernel(pt_ref, ql_ref, qp_ref, cn_ref, kn_ref, ckv_hbm, kpe_hbm, o_ref,
                        cbuf_ref, pbuf_ref, kbf_ref, pbf_ref, m_ref, l_ref, acc_ref, sem,
                        *, layer, pages_per_step, pages_per_chunk, page, t_new, t_pad):
    g = pages_per_step
    bsz, n_pages = pt_ref.shape
    steps_per_seq = n_pages // g
    total = bsz * steps_per_seq

    def page_copies(i, slot):
        b = i // steps_per_seq
        first = (i % steps_per_seq) * g
        copies = []
        for j in range(g):
            pid = pt_ref[b, first + j]
            copies.append(pltpu.make_async_copy(
                ckv_hbm.at[layer, pid], cbuf_ref.at[slot, pl.ds(j * page, page)], sem.at[slot]))
            copies.append(pltpu.make_async_copy(
                kpe_hbm.at[layer, pid], pbuf_ref.at[slot, j], sem.at[slot]))
        return copies

    for cp in page_copies(0, 0):
        cp.start()

    def step(i, carry):
        slot = i % 2
        b = i // steps_per_seq
        s_idx = i % steps_per_seq

        @pl.when(i + 1 < total)
        def _():
            for cp in page_copies(i + 1, 1 - slot):
                cp.start()

        pltpu.make_async_copy(cbuf_ref.at[slot], cbuf_ref.at[slot], sem.at[slot]).wait()
        pltpu.make_async_copy(pbuf_ref.at[slot], pbuf_ref.at[slot], sem.at[slot]).wait()
        _attn_sample_step(b, s_idx, slot, steps_per_seq, ql_ref, qp_ref, cn_ref, kn_ref, o_ref, cbuf_ref,
                          pbuf_ref, kbf_ref, pbf_ref, m_ref, l_ref, acc_ref, g=g,
                          pages_per_chunk=pages_per_chunk, page=page, t_new=t_new, t_pad=t_pad)
        return carry

    lax.fori_loop(0, total, step, 0)


def _attn_sample_step(b, s_idx, slot, steps_per_seq, ql_ref, qp_ref, cn_ref, kn_ref, o_ref, cbuf_ref, pbuf_ref,
                      kbf_ref, pbf_ref, m_ref, l_ref, acc_ref, *, g, pages_per_chunk, page, t_new, t_pad):
    ql = ql_ref[b]
    qp = qp_ref[b]
    rows = ql.shape[0]
    contract_last = (((1,), (1,)), ((), ()))

    @pl.when(s_idx == 0)
    def _():
        cn = cn_ref[b].astype(BF16)
        kn = kn_ref[b].astype(BF16)
        s = (lax.dot_general(ql, cn, contract_last, preferred_element_type=F32)
             + lax.dot_general(qp, kn, contract_last, preferred_element_type=F32))
        t_of_row = lax.broadcasted_iota(jnp.int32, (rows, t_pad), 0) % t_pad
        key = lax.broadcasted_iota(jnp.int32, (rows, t_pad), 1)
        ok = (key <= t_of_row) & (key < t_new)
        s = jnp.where(ok, s, NEG_BIG)
        m = jnp.max(s, axis=-1, keepdims=True)
        p = jnp.where(ok, jnp.exp2(s - m), 0.0)
        m_ref[...] = m
        l_ref[...] = jnp.sum(p, axis=-1, keepdims=True)
        acc_ref[...] = jnp.dot(p.astype(BF16), cn, preferred_element_type=F32)

    kbf_ref[...] = cbuf_ref[slot].astype(BF16)
    for j in range(g):
        pbf_ref[:, j * page:(j + 1) * page] = pbuf_ref[slot, j].astype(BF16)

    ck = pages_per_chunk * page
    parts = []
    for c in range(g // pages_per_chunk):
        kb = kbf_ref[c * ck:(c + 1) * ck, :]
        s = (lax.dot_general(ql, kb, contract_last, preferred_element_type=F32)
             + jnp.dot(qp, pbf_ref[:, c * ck:(c + 1) * ck], preferred_element_type=F32))
        mc = jnp.max(s, axis=-1, keepdims=True)
        p = jnp.exp2(s - mc)
        parts.append((mc, jnp.sum(p, axis=-1, keepdims=True),
                      jnp.dot(p.astype(BF16), kb, preferred_element_type=F32)))
    m_old = m_ref[...]
    m_new = m_old
    for mc, _, _ in parts:
        m_new = jnp.maximum(m_new, mc)
    alpha = jnp.exp2(m_old - m_new)
    l = alpha * l_ref[...]
    acc = alpha * acc_ref[...]
    for mc, lc, ac in parts:
        w = jnp.exp2(mc - m_new)
        l = l + w * lc
        acc = acc + w * ac
    m_ref[...] = m_new
    l_ref[...] = l
    acc_ref[...] = acc

    @pl.when(s_idx == steps_per_seq - 1)
    def _():
        o_ref[b] = acc_ref[...] / l_ref[...]


def _attn_sample(page_table, q_lat, q_pe, ckv_new, kpe_new, cache_ckv, cache_kpe_t, layer, *,
                 t_new, t_pad, pages_per_step, pages_per_chunk):
    bsz, rows, kvr = q_lat.shape
    rope = q_pe.shape[2]
    page = cache_ckv.shape[2]
    g = pages_per_step
    assert page_table.shape[1] % g == 0 and g % pages_per_chunk == 0
    vmem = pl.BlockSpec(memory_space=pltpu.VMEM)
    hbm = pl.BlockSpec(memory_space=pl.ANY)
    return pl.pallas_call(
        functools.partial(_attn_sample_kernel, layer=layer, pages_per_step=g, pages_per_chunk=pages_per_chunk,
                          page=page, t_new=t_new, t_pad=t_pad),
        in_specs=[pl.BlockSpec(memory_space=pltpu.SMEM), vmem, vmem, vmem, vmem, hbm, hbm],
        out_specs=vmem,
        out_shape=jax.ShapeDtypeStruct((bsz, rows, kvr), F32),
        scratch_shapes=[
            pltpu.VMEM((2, g * page, kvr), F32),
            pltpu.VMEM((2, g, rope, page), F32),
            pltpu.VMEM((g * page, kvr), BF16),
            pltpu.VMEM((rope, g * page), BF16),
            pltpu.VMEM((rows, 1), F32),
            pltpu.VMEM((rows, 1), F32),
            pltpu.VMEM((rows, kvr), F32),
            pltpu.SemaphoreType.DMA((2,)),
        ],
        compiler_params=pltpu.CompilerParams(vmem_limit_bytes=VMEM_LIMIT),
        name="attn_sample",
    )(page_table, q_lat, q_pe, ckv_new, kpe_new, cache_ckv, cache_kpe_t)


def _olat_kernel(o_ref, w_ref, out_ref, *, n_heads):
    for h in range(n_heads):
        out_ref[:, h * HEAD_SLOT:(h + 1) * HEAD_SLOT] = jnp.dot(
            o_ref[h].astype(BF16), w_ref[h], preferred_element_type=F32).astype(BF16)


def _olat(o_lat_heads, w_uv_pad):
    n_heads, n, _ = o_lat_heads.shape
    return pl.pallas_call(
        functools.partial(_olat_kernel, n_heads=n_heads),
        out_shape=jax.ShapeDtypeStruct((n, n_heads * HEAD_SLOT), BF16),
        compiler_params=pltpu.CompilerParams(vmem_limit_bytes=VMEM_LIMIT),
        name="o_latent",
    )(o_lat_heads, w_uv_pad)


def _out_proj_kernel(x_ref, c_ref, o_ref, wc_ref, wo_ref, g_ref, x1_ref, h_ref):
    mix = (jnp.dot(c_ref[...], wc_ref[...], preferred_element_type=F32)
           + jnp.dot(o_ref[...], wo_ref[...], preferred_element_type=F32))
    x1 = x_ref[...] + mix
    x1_ref[...] = x1
    h_ref[...] = _rms(x1, g_ref[...]).astype(BF16)


def _out_proj(x, c, o_slots, w_c, w_o, g, *, tm):
    n, d = x.shape
    row = lambda i: (i, 0)
    full = lambda i: (0, 0)
    return pl.pallas_call(
        _out_proj_kernel,
        grid=(n // tm,),
        in_specs=[
            pl.BlockSpec((tm, d), row),
            pl.BlockSpec((tm, c.shape[1]), row),
            pl.BlockSpec((tm, o_slots.shape[1]), row),
            pl.BlockSpec(w_c.shape, full),
            pl.BlockSpec(w_o.shape, full),
            pl.BlockSpec((1, d), full),
        ],
        out_specs=[pl.BlockSpec((tm, d), row), pl.BlockSpec((tm, d), row)],
        out_shape=[jax.ShapeDtypeStruct((n, d), F32), jax.ShapeDtypeStruct((n, d), BF16)],
        compiler_params=_cparams("parallel"),
        name="out_proj",
    )(x, c, o_slots, w_c, w_o, g)


def _swiglu_rows(x, wg, wu, wd):
    a = jnp.dot(x, wg, preferred_element_type=F32)
    b = jnp.dot(x, wu, preferred_element_type=F32)
    mid = (a * jax.nn.sigmoid(a) * b).astype(BF16)
    return jnp.dot(mid, wd, preferred_element_type=F32)


def _ffn_kernel(x_ref, res_ref, wg_hbm, wu_hbm, wd_hbm, o_ref, wg_ref, wu_ref, wd_ref, sem, *, row_splits):
    @pl.when(pl.program_id(0) == 0)
    def _():
        copies = [pltpu.make_async_copy(src, dst, sem.at[j]) for j, (src, dst) in enumerate(
            ((wg_hbm, wg_ref), (wu_hbm, wu_ref), (wd_hbm, wd_ref)))]
        for cp in copies:
            cp.start()
        for cp in copies:
            cp.wait()

    hm = o_ref.shape[0] // row_splits
    for r in range(row_splits):
        rows = slice(r * hm, (r + 1) * hm)
        o_ref[rows, :] = res_ref[rows, :] + _swiglu_rows(x_ref[rows, :], wg_ref[...], wu_ref[...], wd_ref[...])


def _ffn(x, res, wg, wu, wd, *, tm):
    n, d = x.shape
    d_ff = wg.shape[1]
    row = lambda i: (i, 0)
    hbm = pl.BlockSpec(memory_space=pl.ANY)
    return pl.pallas_call(
        functools.partial(_ffn_kernel, row_splits=2 if tm % 32 == 0 else 1),
        grid=(n // tm,),
        in_specs=[pl.BlockSpec((tm, d), row), pl.BlockSpec((tm, d), row), hbm, hbm, hbm],
        out_specs=pl.BlockSpec((tm, d), row),
        out_shape=jax.ShapeDtypeStruct((n, d), F32),
        scratch_shapes=[pltpu.VMEM((d, d_ff), BF16), pltpu.VMEM((d, d_ff), BF16), pltpu.VMEM((d_ff, d), BF16),
                        pltpu.SemaphoreType.DMA((3,))],
        compiler_params=_cparams("arbitrary"),
        name="ffn_dense",
    )(x, res, wg, wu, wd)


def _swiglu_kernel(te_ref, nv_ref, tr_ref, x_ref, wg_ref, wu_ref, wd_ref, o_ref, xb_ref, *, row_groups,
                   full_splits):
    m = pl.program_id(0)
    f = pl.program_id(1)
    tm = o_ref.shape[0]
    hm = tm // row_groups

    @pl.when(f == 0)
    def _():
        o_ref[...] = jnp.zeros_like(o_ref)

    valid = m < nv_ref[0]
    groups_needed = (tr_ref[jnp.minimum(m, nv_ref[0] - 1)] + hm - 1) // hm

    @pl.when(valid & (f == 0))
    def _():
        xb_ref[...] = x_ref[...].astype(BF16)

    @pl.when(valid & (groups_needed == row_groups))
    def _():
        wg = wg_ref[...].astype(BF16)
        wu = wu_ref[...].astype(BF16)
        wd = wd_ref[...].astype(BF16)
        fm = tm // full_splits
        for r in range(full_splits):
            rows = slice(r * fm, (r + 1) * fm)
            o_ref[rows, :] += _swiglu_rows(xb_ref[rows, :], wg, wu, wd)

    @pl.when(valid & (groups_needed < row_groups))
    def _():
        wg = wg_ref[...].astype(BF16)
        wu = wu_ref[...].astype(BF16)
        wd = wd_ref[...].astype(BF16)

        def group(r, carry):
            rows = pl.ds(pl.multiple_of(r * hm, hm), hm)
            o_ref[rows, :] += _swiglu_rows(xb_ref[rows, :], wg, wu, wd)
            return carry

        lax.fori_loop(0, groups_needed, group, 0)


def _swiglu(tile_expert, n_valid, tile_rows, x, wg, wu, wd, *, tm, tf):
    p, d = x.shape
    d_ff = wg.shape[2]
    n_f = d_ff // tf
    n_tiles = p // tm

    def mm(m, nv):
        return jnp.minimum(m, nv[0] - 1)

    def ff(m, f, nv):
        return jnp.where(m < nv[0], f, n_f - 1)

    return pl.pallas_call(
        functools.partial(_swiglu_kernel, row_groups=2 if tm % 32 == 0 else 1, full_splits=1),
        grid_spec=pltpu.PrefetchScalarGridSpec(
            num_scalar_prefetch=3,
            grid=(n_tiles, n_f),
            in_specs=[
                pl.BlockSpec((tm, d), lambda m, f, te, nv, tr: (mm(m, nv), 0)),
                pl.BlockSpec((None, d, tf), lambda m, f, te, nv, tr: (te[mm(m, nv)], 0, ff(m, f, nv))),
                pl.BlockSpec((None, d, tf), lambda m, f, te, nv, tr: (te[mm(m, nv)], 0, ff(m, f, nv))),
                pl.BlockSpec((None, tf, d), lambda m, f, te, nv, tr: (te[mm(m, nv)], ff(m, f, nv), 0)),
            ],
            out_specs=pl.BlockSpec((tm, d), lambda m, f, te, nv, tr: (m, 0)),
            scratch_shapes=[pltpu.VMEM((tm, d), BF16)],
        ),
        out_shape=jax.ShapeDtypeStruct((p, d), F32),
        compiler_params=_cparams("arbitrary", "arbitrary"),
        name="swiglu_experts",
    )(tile_expert, n_valid, tile_rows, x, wg, wu, wd)


def _row_copy(src, src_row, dst, dst_row, sem):
    return pltpu.make_async_copy(src.at[pl.ds(src_row, 1)], dst.at[pl.ds(dst_row, 1)], sem)


def _dispatch_kernel(pos_ref, h_ref, xs_in_ref, xs_ref, sem, *, tt):
    del xs_in_ref

    def body(r, carry):
        for k in range(TOP_K):
            _row_copy(h_ref, r, xs_ref, pos_ref[0, TOP_K * r + k], sem).start(priority=k % 2)
        return carry

    lax.fori_loop(0, tt, body, 0, unroll=8)
    for k in range(TOP_K):
        pltpu.make_async_copy(h_ref, xs_ref.at[pl.ds(0, tt)], sem).wait()


def _dispatch(pos_tiles, h, xs):
    n, d = h.shape
    n_tiles = pos_tiles.shape[0]
    tt = n // n_tiles
    return pl.pallas_call(
        functools.partial(_dispatch_kernel, tt=tt),
        grid=(n_tiles,),
        in_specs=[
            pl.BlockSpec((None, 1, TOP_K * tt), lambda i: (i, 0, 0), memory_space=pltpu.SMEM),
            pl.BlockSpec((tt, d), lambda i: (i, 0)),
            pl.BlockSpec(memory_space=pl.ANY),
        ],
        out_specs=pl.BlockSpec(memory_space=pl.ANY),
        out_shape=jax.ShapeDtypeStruct(xs.shape, xs.dtype),
        scratch_shapes=[pltpu.SemaphoreType.DMA],
        input_output_aliases={2: 0},
        compiler_params=_cparams("arbitrary"),
        name="moe_dispatch",
    )(pos_tiles, h, xs)


def _pool_kernel(*refs, tm, pos0, windows, has_halo, n_experts, seqs):
    if has_halo:
        (prev_ref, halo_ref, x_ref, g1_ref, wp_ref, sc_ref, g2_ref, wr_ref,
         x3_ref, h_ref, hn_ref, gate_ref, idx_ref, ext_ref, lv_ref) = refs
    else:
        (prev_ref, x_ref, g1_ref, wp_ref, sc_ref, g2_ref, wr_ref,
         x3_ref, h_ref, hn_ref, gate_ref, idx_ref, ext_ref, lv_ref) = refs
    for sq in range(seqs):
        _pool_sequence(sq, prev_ref, halo_ref if has_halo else None, x_ref, g1_ref, wp_ref, sc_ref, g2_ref,
                       wr_ref, x3_ref, h_ref, hn_ref, gate_ref, idx_ref, ext_ref, lv_ref,
                       tm=tm, pos0=pos0, windows=windows, n_experts=n_experts)


def _pool_sequence(sq, prev_ref, halo_ref, x_ref, g1_ref, wp_ref, sc_ref, g2_ref, wr_ref,
                   x3_ref, h_ref, hn_ref, gate_ref, idx_ref, ext_ref, lv_ref, *, tm, pos0, windows, n_experts):
    t = pl.program_id(1)
    x = x_ref[sq]
    hn = _rms(x, g1_ref[...])
    hn_ref[sq] = hn
    top = SUBLANES
    base = top + POOL_HALO
    end = base + tm
    d = x.shape[1]
    ext_ref[0:top, :] = jnp.zeros((top, d), F32)
    if halo_ref is not None:
        @pl.when(t == 0)
        def _():
            ext_ref[top:base, :] = prev_ref[sq]

        @pl.when(t > 0)
        def _():
            ext_ref[top:base, :] = _rms(halo_ref[sq], g1_ref[...])
    else:
        ext_ref[top:base, :] = prev_ref[sq]
    ext_ref[base:end, :] = hn

    gc = d // len(windows)

    def level_rows(k, lo_row, hi_row, lanes):
        if k == 0:
            return ext_ref[lo_row:hi_row, lanes]
        return lv_ref[k - 1, lo_row:hi_row, lanes]

    for k in range(1, len(windows)):
        lanes = slice((k - 1) * gc, d)
        shift = 1 << (k - 1)
        lv_ref[k - 1, 0:top, lanes] = jnp.zeros((top, d - (k - 1) * gc), F32)
        lv_ref[k - 1, top:end, lanes] = (level_rows(k - 1, top, end, lanes)
                                         + level_rows(k - 1, top - shift, end - shift, lanes))

    pos = pos0 + t * tm + lax.broadcasted_iota(jnp.int32, (tm, 1), 0)
    for gi, w in enumerate(windows):
        lanes = slice(gi * gc, (gi + 1) * gc)
        if gi + 1 < len(windows):
            s = level_rows(gi + 1, base, end, lanes)
        else:
            s = level_rows(gi, base, end, lanes) + level_rows(gi, base - w // 2, end - w // 2, lanes)
        cnt = jnp.minimum(pos + 1, w).astype(F32)
        diff = s / cnt - hn[:, lanes]
        y = jnp.dot(diff.astype(BF16), wp_ref[gi], preferred_element_type=F32)
        x3_ref[sq, :, lanes] = x[:, lanes] + y * sc_ref[:, lanes]

    x3 = x3_ref[sq]
    h2 = _rms(x3, g2_ref[...])
    h_ref[sq] = h2
    logits = jnp.dot(h2, wr_ref[...], preferred_element_type=F32, precision=lax.Precision.HIGHEST)
    lane = lax.broadcasted_iota(jnp.int32, logits.shape, 1)
    logits = jnp.where(lane < n_experts, logits, -jnp.inf)
    m1 = jnp.max(logits, axis=-1, keepdims=True)
    i1 = jnp.min(jnp.where(logits == m1, lane, LANES), axis=-1, keepdims=True)
    rest = jnp.where(lane == i1, -jnp.inf, logits)
    m2 = jnp.max(rest, axis=-1, keepdims=True)
    i2 = jnp.min(jnp.where(rest == m2, lane, LANES), axis=-1, keepdims=True)
    e = jnp.exp(m2 - m1)
    g1 = 1.0 / (1.0 + e)
    g2 = e / (1.0 + e)
    gate_ref[sq] = jnp.where(lane == 0, g1, jnp.where(lane == 1, g2, 0.0))
    idx_ref[sq] = jnp.where(lane == 0, i1, jnp.where(lane == 1, i2, 0))


def _pool_layer(prev_pad, x, g1, w_pool, scale, g2, w_router_pad, *, tm, pos0, n_experts):
    bsz, t_len, d = x.shape
    assert all(w == 2 ** (g + 1) for g, w in enumerate(POOL_WINDOWS)) and POOL_HALO >= max(POOL_WINDOWS) - 1
    n_t = t_len // tm
    has_halo = n_t > 1
    hb = tm // POOL_HALO if has_halo else 1
    n_rows = SUBLANES + POOL_HALO + tm
    seqs = _seqs_per_step(bsz, n_t)
    cur = lambda b, t: (b, t, 0)
    full2 = lambda b, t: (0, 0)
    in_specs = [pl.BlockSpec((seqs, POOL_HALO, d), lambda b, t: (b, 0, 0))]
    args = [prev_pad]
    if has_halo:
        in_specs.append(pl.BlockSpec((seqs, POOL_HALO, d), lambda b, t: (b, jnp.maximum(t * hb - 1, 0), 0)))
        args.append(x)
    in_specs += [
        pl.BlockSpec((seqs, tm, d), cur),
        pl.BlockSpec((1, d), full2),
        pl.BlockSpec(w_pool.shape, lambda b, t: (0, 0, 0)),
        pl.BlockSpec((1, d), full2),
        pl.BlockSpec((1, d), full2),
        pl.BlockSpec(w_router_pad.shape, full2),
    ]
    args += [x, g1, w_pool, scale, g2, w_router_pad]
    return pl.pallas_call(
        functools.partial(_pool_kernel, tm=tm, pos0=pos0, windows=POOL_WINDOWS, has_halo=has_halo,
                          n_experts=n_experts, seqs=seqs),
        grid=(bsz // seqs, n_t),
        in_specs=in_specs,
        out_specs=[
            pl.BlockSpec((seqs, tm, d), cur),
            pl.BlockSpec((seqs, tm, d), cur),
            pl.BlockSpec((seqs, tm, d), cur),
            pl.BlockSpec((seqs, tm, LANES), cur),
            pl.BlockSpec((seqs, tm, LANES), cur),
        ],
        out_shape=[
            jax.ShapeDtypeStruct((bsz, t_len, d), F32),
            jax.ShapeDtypeStruct((bsz, t_len, d), F32),
            jax.ShapeDtypeStruct((bsz, t_len, d), F32),
            jax.ShapeDtypeStruct((bsz, t_len, LANES), F32),
            jax.ShapeDtypeStruct((bsz, t_len, LANES), jnp.int32),
        ],
        scratch_shapes=[pltpu.VMEM((n_rows, d), F32), pltpu.VMEM((len(POOL_WINDOWS) - 1, n_rows, d), F32)],
        compiler_params=_cparams("parallel", "parallel"),
        name="pool_layer",
    )(*args)


def _combine_kernel(pos_cur_ref, pos_nxt_ref, x_ref, gate_ref, g_ref, ys_ref, y_ref, buf_ref, sem,
                    *, tt, apply_norm):
    i = pl.program_id(0)
    n = pl.num_programs(0)
    slot = i % 2

    def fetch(pos_ref, s):
        def body(r, carry):
            for k in range(TOP_K):
                _row_copy(ys_ref, pos_ref[0, TOP_K * r + k], buf_ref.at[s, k], r,
                          sem.at[s]).start(priority=k % 2)
            return carry
        lax.fori_loop(0, tt, body, 0, unroll=8)

    @pl.when(i == 0)
    def _():
        fetch(pos_cur_ref, 0)

    @pl.when(i + 1 < n)
    def _():
        fetch(pos_nxt_ref, 1 - slot)

    for k in range(TOP_K):
        pltpu.make_async_copy(ys_ref.at[pl.ds(0, tt)], buf_ref.at[slot, k], sem.at[slot]).wait()
    moe = gate_ref[:, 0:1] * buf_ref[slot, 0]
    for k in range(1, TOP_K):
        moe = moe + gate_ref[:, k:k + 1] * buf_ref[slot, k]
    y = x_ref[...] + moe
    y_ref[...] = _rms(y, g_ref[...]) if apply_norm else y


def _combine(pos_tiles, x, gates, g, ys, *, apply_norm):
    n, d = x.shape
    n_tiles = pos_tiles.shape[0]
    tt = n // n_tiles
    pos_spec = lambda fn: pl.BlockSpec((None, 1, TOP_K * tt), fn, memory_space=pltpu.SMEM)
    return pl.pallas_call(
        functools.partial(_combine_kernel, tt=tt, apply_norm=apply_norm),
        grid=(n_tiles,),
        in_specs=[
            pos_spec(lambda i: (i, 0, 0)),
            pos_spec(lambda i: (jnp.minimum(i + 1, n_tiles - 1), 0, 0)),
            pl.BlockSpec((tt, d), lambda i: (i, 0)),
            pl.BlockSpec((tt, LANES), lambda i: (i, 0)),
            pl.BlockSpec((1, d), lambda i: (0, 0)),
            pl.BlockSpec(memory_space=pl.ANY),
        ],
        out_specs=pl.BlockSpec((tt, d), lambda i: (i, 0)),
        out_shape=jax.ShapeDtypeStruct((n, d), F32),
        scratch_shapes=[pltpu.VMEM((2, TOP_K, tt, d), F32), pltpu.SemaphoreType.DMA((2,))],
        compiler_params=_cparams("arbitrary"),
        name="moe_combine",
    )(pos_tiles, pos_tiles, x, gates, g, ys)


def _final_kernel(x_ref, g_ref, y_ref):
    y_ref[...] = _rms(x_ref[...], g_ref[...])


def _final(x, g, *, tm):
    n, d = x.shape
    row = lambda i: (i, 0)
    return pl.pallas_call(
        _final_kernel,
        grid=(n // tm,),
        in_specs=[pl.BlockSpec((tm, d), row), pl.BlockSpec((1, d), lambda i: (0, 0))],
        out_specs=pl.BlockSpec((tm, d), row),
        out_shape=jax.ShapeDtypeStruct((n, d), F32),
        compiler_params=_cparams("parallel"),
        name="final_norm",
    )(x, g)


def _swap_halves(w):
    half = w.shape[-1] // 2
    return jnp.concatenate([w[..., half:], w[..., :half]], axis=-1)


def _slot(parts, total=HEAD_SLOT):
    used = sum(p.shape[-1] for p in parts)
    pad = jnp.zeros(parts[0].shape[:-1] + (total - used,), parts[0].dtype)
    return jnp.concatenate(list(parts) + [pad], axis=-1)


def _prep_even_weights(w_in, w_uq, w_ukv, w_out, *, dc, qr, kvr):
    d = w_in.shape[0]
    body = w_in[:, :2 * dc + qr + kvr]
    w_kpe = w_in[:, 2 * dc + qr + kvr:]
    z_nope = jnp.zeros((d, NOPE_DIM), w_in.dtype)
    w_in_ext = jnp.concatenate(
        [body, _slot([z_nope, w_kpe]), _slot([z_nope, _swap_halves(w_kpe)])], axis=-1).astype(BF16)

    uq = w_uq.reshape(qr, N_HEADS, NOPE_DIM + ROPE_DIM)
    uq_nope, uq_pe = uq[..., :NOPE_DIM], uq[..., NOPE_DIM:]
    qa = _slot([uq_nope, uq_pe]).reshape(qr, N_HEADS * HEAD_SLOT)
    qb = _slot([jnp.zeros_like(uq_nope), _swap_halves(uq_pe)]).reshape(qr, N_HEADS * HEAD_SLOT)
    w_uq_ext = jnp.concatenate([qa, qb], axis=-1).astype(BF16)

    ukv = w_ukv.reshape(kvr, N_HEADS, NOPE_DIM + V_DIM)
    uk, uv = ukv[..., :NOPE_DIM], ukv[..., NOPE_DIM:]
    w_kv_ext = jnp.concatenate([_slot([uk]).reshape(kvr, N_HEADS * HEAD_SLOT),
                                _slot([uv]).reshape(kvr, N_HEADS * HEAD_SLOT)], axis=-1).astype(BF16)
    w_q2lat = jnp.concatenate(
        [jnp.transpose(uk, (1, 2, 0)),
         jnp.zeros((N_HEADS, HEAD_SLOT - NOPE_DIM, kvr), w_ukv.dtype)], axis=1).astype(BF16)
    w_uv_pad = _slot([jnp.transpose(uv, (1, 0, 2))]).astype(BF16)

    w_c = w_out[:dc].astype(BF16)
    wo = w_out[dc:].reshape(N_HEADS, V_DIM, d)
    w_o = jnp.concatenate([wo, jnp.zeros((N_HEADS, HEAD_SLOT - V_DIM, d), w_out.dtype)],
                          axis=1).reshape(N_HEADS * HEAD_SLOT, d).astype(BF16)
    return w_in_ext, w_uq_ext, w_kv_ext, w_q2lat, w_uv_pad, w_c, w_o


def _rope_tables(pos):
    half = ROPE_DIM // 2
    inv_freq = jnp.power(ROPE_THETA, -jnp.arange(half, dtype=F32) / half)
    ang = pos.astype(F32)[:, None] * inv_freq[None, :]
    cos, sin = jnp.cos(ang), jnp.sin(ang)
    n = pos.shape[0]
    z_nope = jnp.zeros((n, NOPE_DIM), F32)
    cos_k = _slot([z_nope, cos, cos])
    sin_k = _slot([z_nope, -sin, sin])
    cos_q = _slot([jnp.ones((n, NOPE_DIM), F32), cos, cos]) * (ATTN_SCALE * LOG2E)
    sin_q = sin_k * (ATTN_SCALE * LOG2E)
    return cos_k, sin_k, cos_q, sin_q


def _even_layer(x, conv_prev, pos, attend, P, layer, i):
    bsz, t_len, d = x.shape
    n = bsz * t_len
    dc = P['conv_w'].shape[2]
    qr = P['q_norm'].shape[1]
    kvr = P['kv_norm'].shape[1]
    k_taps = P['conv_w'].shape[1]
    tm = next((t for t in (PROJ_TILE_ROWS, FFN_TILE_ROWS) if t_len % t == 0), n)
    w_in_ext, w_uq_ext, w_kv_ext, w_q2lat, w_uv_pad, w_c, w_o = _prep_even_weights(
        P['w_in'][i], P['w_uq'][i], P['w_ukv'][i], P['w_out'][i], dc=dc, qr=qr, kvr=kvr)
    cos_k, sin_k, cos_q, sin_q = _rope_tables(pos)
    if t_len < tm:
        reps = tm // t_len
        cos_k, sin_k, cos_q, sin_q = (jnp.tile(a, (reps, 1)) for a in (cos_k, sin_k, cos_q, sin_q))

    xf = x.reshape(n, d)
    u, qn, ckv, kpe_slot = _in_proj(
        xf, P['norm_mix'][layer][None], w_in_ext, P['q_norm'][i][None], P['kv_norm'][i][None],
        cos_k, sin_k, dc=dc, qr=qr, kvr=kvr, tm=tm)

    u3 = u.reshape(bsz, t_len, dc)
    prev_pad = jnp.pad(conv_prev.astype(F32), ((0, 0), (CONV_HALO - (k_taps - 1), 0), (0, 0)))
    c = _conv_branch(prev_pad, u3, P['conv_w'][i], P['conv_b'][i][None], P['conv_ln_g'][i][None],
                     P['conv_ln_b'][i][None], tc=min(t_len, 512))
    conv_state = jnp.concatenate([conv_prev.astype(F32), u3], axis=1)[:, -(k_taps - 1):]

    q_slots = _q_proj(qn, w_uq_ext, cos_q, sin_q, tm=tm)
    o_slots = attend(q_slots, ckv, kpe_slot, w_kv_ext, w_q2lat, w_uv_pad, tm)

    x1, h_ffn = _out_proj(xf, c.reshape(n, dc), o_slots, w_c, w_o, P['norm_ffn'][layer][None], tm=tm)

    x2 = _ffn(h_ffn, x1, P['ffn_w_gate'][i].astype(BF16), P['ffn_w_up'][i].astype(BF16),
              P['ffn_w_down'][i].astype(BF16), tm=_row_tile(n, FFN_TILE_ROWS))
    kpe = kpe_slot[:, NOPE_DIM:NOPE_DIM + ROPE_DIM]
    return (x2.reshape(bsz, t_len, d), conv_state, ckv.reshape(bsz, t_len, kvr),
            kpe.reshape(bsz, t_len, ROPE_DIM))


def _make_attend_prompt(bsz, t_len):
    def attend(q_slots, ckv, kpe_slot, w_kv_ext, w_q2lat, w_uv_pad, tm):
        k_slots, v_slots = _kv_proj(ckv, kpe_slot, w_kv_ext, tm=tm)
        return _attn_prompt(q_slots, k_slots, v_slots, bsz=bsz, t_len=t_len, tq=min(t_len, 512))
    return attend


def _make_attend_sample(bsz, t_len, cache_ckv, cache_kpe_t, page_table, i):
    t_pad = 8

    def attend(q_slots, ckv, kpe_slot, w_kv_ext, w_q2lat, w_uv_pad, tm):
        n = bsz * t_len
        kvr = ckv.shape[1]
        q_lat = _qlat(q_slots, w_q2lat)
        q_lat = q_lat.reshape(N_HEADS, bsz, t_len, kvr)
        q_lat = jnp.pad(q_lat, ((0, 0), (0, 0), (0, t_pad - t_len), (0, 0)))
        q_lat = jnp.transpose(q_lat, (1, 0, 2, 3)).reshape(bsz, N_HEADS * t_pad, kvr)
        q_pe = q_slots.reshape(bsz, t_len, N_HEADS, HEAD_SLOT)[..., NOPE_DIM:NOPE_DIM + ROPE_DIM]
        q_pe = jnp.pad(jnp.transpose(q_pe, (0, 2, 1, 3)), ((0, 0), (0, 0), (0, t_pad - t_len), (0, 0)))
        q_pe = q_pe.reshape(bsz, N_HEADS * t_pad, ROPE_DIM)
        pad_t = ((0, 0), (0, t_pad - t_len), (0, 0))
        ckv_new = jnp.pad(ckv.reshape(bsz, t_len, kvr), pad_t)
        kpe_new = jnp.pad(kpe_slot[:, NOPE_DIM:NOPE_DIM + ROPE_DIM].reshape(bsz, t_len, ROPE_DIM), pad_t)
        o_lat = _attn_sample(page_table, q_lat, q_pe, ckv_new, kpe_new, cache_ckv, cache_kpe_t, i,
                             t_new=t_len, t_pad=t_pad, pages_per_step=64,
                             pages_per_chunk=64)
        o_lat = o_lat.reshape(bsz, N_HEADS, t_pad, kvr)[:, :, :t_len]
        o_lat = jnp.transpose(o_lat, (1, 0, 2, 3)).reshape(N_HEADS, n, kvr)
        return _olat(o_lat, w_uv_pad)
    return attend


def _odd_layer_mixer(x, pool_prev, pos0, P, layer, i):
    bsz, t_len, d = x.shape
    n_experts = P['router_w'].shape[2]
    ctx = max(POOL_WINDOWS) - 1
    prev_pad = jnp.pad(pool_prev.astype(F32), ((0, 0), (POOL_HALO - ctx, 0), (0, 0)))
    w_router_pad = jnp.pad(P['router_w'][i], ((0, 0), (0, LANES - n_experts)))
    x3, h_moe, hn, gates, ids = _pool_layer(
        prev_pad, x, P['norm_mix'][layer][None], P['pool_w'][i].astype(BF16), P['pool_scale'][i][None],
        P['norm_ffn'][layer][None], w_router_pad, tm=min(t_len, 512), pos0=pos0, n_experts=n_experts)
    pool_state = jnp.concatenate([pool_prev.astype(F32), hn], axis=1)[:, -ctx:]
    n = bsz * t_len
    return (x3.reshape(n, d), h_moe.reshape(n, d), pool_state, gates.reshape(n, LANES),
            ids.reshape(n, LANES)[:, :TOP_K])


def _moe_plan(ids, n_experts, tm):
    n = ids.shape[0]
    flat = ids.reshape(-1)
    experts = jnp.arange(n_experts, dtype=jnp.int32)
    onehot = (flat[:, None] == experts[None, :]).astype(jnp.int32)
    csum = jnp.cumsum(onehot, axis=0)
    counts = csum[-1]
    rank = jnp.sum(csum * onehot, axis=1) - 1
    padded = ((counts + tm - 1) // tm) * tm
    ends = jnp.cumsum(padded)
    starts = ends - padded
    pos = jnp.sum(starts[None, :] * onehot, axis=1) + rank
    n_tiles = (n * TOP_K + tm - 1) // tm + n_experts
    n_valid = (ends[-1] // tm).astype(jnp.int32)
    tile_start = jnp.arange(n_tiles, dtype=jnp.int32) * tm
    tile_expert = jnp.sum((ends[None, :] <= tile_start[:, None]).astype(jnp.int32), axis=1)
    tile_expert = jnp.minimum(tile_expert, n_experts - 1)
    group_end = starts + counts
    tile_onehot = (tile_expert[:, None] == experts[None, :]).astype(jnp.int32)
    tile_rows = jnp.clip(jnp.sum(group_end[None, :] * tile_onehot, axis=1) - tile_start, 0, tm)
    return pos.reshape(n, TOP_K), tile_expert, n_valid.reshape(1), tile_rows.astype(jnp.int32), n_tiles


def _moe_layer(groups, P, i, g_final, *, tm, tt):
    d = groups[0][0].shape[1]
    n_experts = P['router_w'].shape[2]
    ids = jnp.concatenate([g[3] for g in groups], axis=0)
    pos, tile_expert, n_valid, tile_rows, n_tiles = _moe_plan(ids, n_experts, tm)
    xs = jnp.zeros((n_tiles * tm, d), F32)
    pos_tiles = []
    lo = 0
    for x3, h, _, _ in groups:
        n = x3.shape[0]
        t = min(tt, n)
        pt = pos[lo:lo + n].reshape(n // t, 1, TOP_K * t)
        xs = _dispatch(pt, h, xs)
        pos_tiles.append(pt)
        lo += n
    ys = _swiglu(tile_expert, n_valid, tile_rows, xs, P['moe_w_gate'][i], P['moe_w_up'][i],
                 P['moe_w_down'][i], tm=tm, tf=512)
    gain = g_final if g_final is not None else jnp.ones((1, d), F32)
    return [_combine(pt, x3, gates, gain, ys, apply_norm=g_final is not None)
            for pt, (x3, _, gates, _) in zip(pos_tiles, groups)]


def kernel(x_prompt, x_sample, cache_ckv, cache_kpe, page_table, state_conv, state_pool, norm_mix, norm_ffn,
           norm_final, w_in, conv_w, conv_b, conv_ln_g, conv_ln_b, q_norm, w_uq, kv_norm, w_ukv, w_out,
           ffn_w_gate, ffn_w_up, ffn_w_down, pool_w, pool_scale, router_w, moe_w_gate, moe_w_up, moe_w_down):
    P = {'norm_mix': norm_mix, 'norm_ffn': norm_ffn, 'norm_final': norm_final, 'w_in': w_in,
         'conv_w': conv_w, 'conv_b': conv_b, 'conv_ln_g': conv_ln_g, 'conv_ln_b': conv_ln_b,
         'q_norm': q_norm, 'w_uq': w_uq, 'kv_norm': kv_norm, 'w_ukv': w_ukv, 'w_out': w_out,
         'ffn_w_gate': ffn_w_gate, 'ffn_w_up': ffn_w_up, 'ffn_w_down': ffn_w_down,
         'pool_w': pool_w, 'pool_scale': pool_scale, 'router_w': router_w,
         'moe_w_gate': moe_w_gate, 'moe_w_up': moe_w_up, 'moe_w_down': moe_w_down}
    depth = norm_mix.shape[0]
    d = x_prompt.shape[2]
    bp, tp, _ = x_prompt.shape
    bs, ts, _ = x_sample.shape
    past = page_table.shape[1] * cache_ckv.shape[2]
    k_taps = conv_w.shape[1]
    ctx = max(POOL_WINDOWS) - 1
    pos_p = jnp.arange(tp, dtype=jnp.int32)
    pos_s = past + jnp.arange(ts, dtype=jnp.int32)
    cache_kpe_t = jnp.swapaxes(cache_kpe, 2, 3)

    xp, xs = x_prompt, x_sample
    outs_p = {'ckv': [], 'kpe': [], 'conv': [], 'pool': []}
    outs_s = {'ckv': [], 'kpe': [], 'conv': [], 'pool': []}
    for layer in range(depth):
        i = layer // 2
        last = layer == depth - 1
        if layer % 2 == 0:
            conv0 = jnp.zeros((bp, k_taps - 1, conv_w.shape[2]), F32)
            xp, cst, ckv, kpe = _even_layer(xp, conv0, pos_p, _make_attend_prompt(bp, tp), P, layer, i)
            outs_p['conv'].append(cst); outs_p['ckv'].append(ckv); outs_p['kpe'].append(kpe)
            xs, cst, ckv, kpe = _even_layer(
                xs, state_conv[i], pos_s, _make_attend_sample(bs, ts, cache_ckv, cache_kpe_t, page_table, i),
                P, layer, i)
            outs_s['conv'].append(cst); outs_s['ckv'].append(ckv); outs_s['kpe'].append(kpe)
            if last:
                xp = _final(xp.reshape(bp * tp, d), norm_final[None], tm=_row_tile(bp * tp, 512)).reshape(xp.shape)
                xs = _final(xs.reshape(bs * ts, d), norm_final[None], tm=_row_tile(bs * ts, 512)).reshape(xs.shape)
        else:
            pool0 = jnp.zeros((bp, ctx, d), F32)
            x3p, hp, pst_p, gp, ip = _odd_layer_mixer(xp, pool0, 0, P, layer, i)
            x3s, hs, pst_s, gs, is_ = _odd_layer_mixer(xs, state_pool[i], past, P, layer, i)
            outs_p['pool'].append(pst_p); outs_s['pool'].append(pst_s)
            yp, ysm = _moe_layer([(x3p, hp, gp, ip), (x3s, hs, gs, is_)], P, i,
                                 norm_final[None] if last else None, tm=1024, tt=512)
            xp, xs = yp.reshape(xp.shape), ysm.reshape(xs.shape)

    def stack(lst, shape_if_empty):
        return jnp.stack(lst) if lst else jnp.zeros(shape_if_empty, F32)

    return (xp, xs,
            jnp.stack(outs_p['ckv']), jnp.stack(outs_p['kpe']), jnp.stack(outs_p['conv']),
            stack(outs_p['pool'], (0, bp, ctx, d)),
            jnp.stack(outs_s['ckv']), jnp.stack(outs_s['kpe']), jnp.stack(outs_s['conv']),
            stack(outs_s['pool'], (0, bs, ctx, d)))
```

```python
import functools

import jax
import jax.numpy as jnp
from jax import lax
from jax.experimental import pallas as pl
from jax.experimental.pallas import tpu as pltpu

F32 = jnp.float32
BF16 = jnp.bfloat16

N_HEADS = 8
NOPE_DIM = 64
ROPE_DIM = 32
V_DIM = 64
ROPE_THETA = 10000.0
ATTN_SCALE = (NOPE_DIM + ROPE_DIM) ** -0.5
LOG2E = 1.4426950408889634
POOL_WINDOWS = (2, 4, 8, 16)
TOP_K = 2
EPS = 1e-6

HEAD_SLOT = 128
LANES = 128
SUBLANES = 8
NEG_BIG = -1e30
VMEM_LIMIT = 56 * 1024 * 1024

PROJ_TILE_ROWS = 1024
FFN_TILE_ROWS = 512
CONV_HALO = 32
POOL_HALO = 16


def _cparams(*semantics):
    return pltpu.CompilerParams(dimension_semantics=semantics, vmem_limit_bytes=VMEM_LIMIT)


def _rms(x, g):
    return x * lax.rsqrt(jnp.mean(x * x, axis=-1, keepdims=True) + EPS) * g


def _row_tile(n, want):
    if n <= want:
        return n
    t = want
    while t >= 8:
        if n % t == 0 and t % 8 == 0:
            return t
        t -= 8
    return n


def _in_proj_kernel(x_ref, g_ref, w_ref, qg_ref, kvg_ref, cos_ref, sin_ref,
                    u_ref, qn_ref, ckv_ref, kpe_ref, *, dc, qr, kvr):
    h = _rms(x_ref[...], g_ref[...])
    proj = jnp.dot(h.astype(BF16), w_ref[...], preferred_element_type=F32)
    a = proj[:, :dc]
    gate = proj[:, dc:2 * dc]
    u_ref[...] = a * jax.nn.sigmoid(gate)
    o = 2 * dc
    qn_ref[...] = _rms(proj[:, o:o + qr], qg_ref[...]).astype(BF16)
    o += qr
    ckv_ref[...] = _rms(proj[:, o:o + kvr], kvg_ref[...])
    o += kvr
    kpe_ref[...] = (proj[:, o:o + HEAD_SLOT] * cos_ref[...]
                    + proj[:, o + HEAD_SLOT:o + 2 * HEAD_SLOT] * sin_ref[...])


def _in_proj(x, g, w_ext, qg, kvg, cos_k, sin_k, *, dc, qr, kvr, tm):
    n, d = x.shape
    n_tab = cos_k.shape[0] // tm
    row = lambda i: (i, 0)
    full = lambda i: (0, 0)
    tab = lambda i: (i % n_tab, 0)
    return pl.pallas_call(
        functools.partial(_in_proj_kernel, dc=dc, qr=qr, kvr=kvr),
        grid=(n // tm,),
        in_specs=[
            pl.BlockSpec((tm, d), row),
            pl.BlockSpec((1, d), full),
            pl.BlockSpec(w_ext.shape, full),
            pl.BlockSpec((1, qr), full),
            pl.BlockSpec((1, kvr), full),
            pl.BlockSpec((tm, HEAD_SLOT), tab),
            pl.BlockSpec((tm, HEAD_SLOT), tab),
        ],
        out_specs=[
            pl.BlockSpec((tm, dc), row),
            pl.BlockSpec((tm, qr), row),
            pl.BlockSpec((tm, kvr), row),
            pl.BlockSpec((tm, HEAD_SLOT), row),
        ],
        out_shape=[
            jax.ShapeDtypeStruct((n, dc), F32),
            jax.ShapeDtypeStruct((n, qr), BF16),
            jax.ShapeDtypeStruct((n, kvr), F32),
            jax.ShapeDtypeStruct((n, HEAD_SLOT), F32),
        ],
        compiler_params=_cparams("parallel"),
        name="in_proj",
    )(x, g, w_ext, qg, kvg, cos_k, sin_k)


def _conv_kernel(*refs, tc, k_taps, rows_per_chunk, has_halo, seqs):
    if has_halo:
        prev_ref, halo_ref, u_ref, w_ref, b_ref, lg_ref, lb_ref, c_ref, ext_ref, sh_ref = refs
    else:
        prev_ref, u_ref, w_ref, b_ref, lg_ref, lb_ref, c_ref, ext_ref, sh_ref = refs
    n_rows = CONV_HALO + tc
    first = CONV_HALO - (k_taps - 1)

    def window(start, n):
        q, s = divmod(start, SUBLANES)
        if s == 0:
            return ext_ref[start:start + n, :]
        return sh_ref[s - 1, q * SUBLANES:q * SUBLANES + n, :]

    for sq in range(seqs):
        if has_halo:
            t = pl.program_id(1)

            @pl.when(t == 0)
            def _():
                ext_ref[0:CONV_HALO, :] = prev_ref[sq]

            @pl.when(t > 0)
            def _():
                ext_ref[0:CONV_HALO, :] = halo_ref[sq]
        else:
            ext_ref[0:CONV_HALO, :] = prev_ref[sq]
        ext_ref[CONV_HALO:n_rows, :] = u_ref[sq]

        for s in range(1, SUBLANES):
            sh_ref[s - 1, 0:n_rows - SUBLANES, :] = ext_ref[s:s + n_rows - SUBLANES, :]

        for c0 in range(0, tc, rows_per_chunk):
            rc = min(rows_per_chunk, tc - c0)
            acc = w_ref[0:1, :] * window(first + c0, rc)
            for k in range(1, k_taps):
                acc = acc + w_ref[k:k + 1, :] * window(first + c0 + k, rc)
            acc = acc + b_ref[...]
            mu = jnp.mean(acc, axis=-1, keepdims=True)
            xc = acc - mu
            y = xc * lax.rsqrt(jnp.mean(xc * xc, axis=-1, keepdims=True) + EPS)
            y = y * lg_ref[...] + lb_ref[...]
            c_ref[sq, c0:c0 + rc, :] = (y * jax.nn.sigmoid(y)).astype(BF16)


def _seqs_per_step(bsz, n_t, want=8):
    if n_t > 1:
        return 1
    g = min(want, bsz)
    while bsz % g:
        g -= 1
    return g


def _conv_branch(prev_pad, u, w, b, lg, lb, *, tc):
    bsz, t_len, c = u.shape
    k_taps = w.shape[0]
    n_t = t_len // tc
    has_halo = n_t > 1
    hb = tc // CONV_HALO if has_halo else 1
    seqs = _seqs_per_step(bsz, n_t)
    in_specs = [pl.BlockSpec((seqs, CONV_HALO, c), lambda bi, ti: (bi, 0, 0))]
    args = [prev_pad]
    if has_halo:
        in_specs.append(pl.BlockSpec((seqs, CONV_HALO, c),
                                     lambda bi, ti: (bi, jnp.maximum(ti * hb - 1, 0), 0)))
        args.append(u)
    in_specs += [
        pl.BlockSpec((seqs, tc, c), lambda bi, ti: (bi, ti, 0)),
        pl.BlockSpec((k_taps, c), lambda bi, ti: (0, 0)),
        pl.BlockSpec((1, c), lambda bi, ti: (0, 0)),
        pl.BlockSpec((1, c), lambda bi, ti: (0, 0)),
        pl.BlockSpec((1, c), lambda bi, ti: (0, 0)),
    ]
    args += [u, w, b, lg, lb]
    return pl.pallas_call(
        functools.partial(_conv_kernel, tc=tc, k_taps=k_taps, rows_per_chunk=64, has_halo=has_halo, seqs=seqs),
        grid=(bsz // seqs, n_t),
        in_specs=in_specs,
        out_specs=pl.BlockSpec((seqs, tc, c), lambda bi, ti: (bi, ti, 0)),
        out_shape=jax.ShapeDtypeStruct((bsz, t_len, c), BF16),
        scratch_shapes=[pltpu.VMEM((CONV_HALO + tc, c), F32),
                        pltpu.VMEM((SUBLANES - 1, CONV_HALO + tc, c), F32)],
        compiler_params=_cparams("parallel", "parallel"),
        name="conv_branch",
    )(*args)


def _q_proj_kernel(qn_ref, w_ref, cos_ref, sin_ref, q_ref, *, n_heads):
    proj = jnp.dot(qn_ref[...], w_ref[...], preferred_element_type=F32)
    half = n_heads * HEAD_SLOT
    cos = cos_ref[...]
    sin = sin_ref[...]
    for h in range(n_heads):
        a = proj[:, h * HEAD_SLOT:(h + 1) * HEAD_SLOT]
        b = proj[:, half + h * HEAD_SLOT:half + (h + 1) * HEAD_SLOT]
        q_ref[:, h * HEAD_SLOT:(h + 1) * HEAD_SLOT] = (a * cos + b * sin).astype(BF16)


def _q_proj(qn, w_ext, cos_q, sin_q, *, tm):
    n, qr = qn.shape
    n_tab = cos_q.shape[0] // tm
    width = N_HEADS * HEAD_SLOT
    return pl.pallas_call(
        functools.partial(_q_proj_kernel, n_heads=N_HEADS),
        grid=(n // tm,),
        in_specs=[
            pl.BlockSpec((tm, qr), lambda i: (i, 0)),
            pl.BlockSpec(w_ext.shape, lambda i: (0, 0)),
            pl.BlockSpec((tm, HEAD_SLOT), lambda i: (i % n_tab, 0)),
            pl.BlockSpec((tm, HEAD_SLOT), lambda i: (i % n_tab, 0)),
        ],
        out_specs=pl.BlockSpec((tm, width), lambda i: (i, 0)),
        out_shape=jax.ShapeDtypeStruct((n, width), BF16),
        compiler_params=_cparams("parallel"),
        name="q_proj",
    )(qn, w_ext, cos_q, sin_q)


def _kv_proj_kernel(ckv_ref, kpe_ref, w_ref, k_ref, v_ref, *, n_heads):
    proj = jnp.dot(ckv_ref[...].astype(BF16), w_ref[...], preferred_element_type=F32)
    half = n_heads * HEAD_SLOT
    kpe = kpe_ref[...]
    for h in range(n_heads):
        k_ref[:, h * HEAD_SLOT:(h + 1) * HEAD_SLOT] = (
            proj[:, h * HEAD_SLOT:(h + 1) * HEAD_SLOT] + kpe).astype(BF16)
    v = proj[:, half:]
    lane = lax.broadcasted_iota(jnp.int32, v.shape, 1)
    v_ref[...] = jnp.where(lane % HEAD_SLOT == V_DIM, 1.0, v).astype(BF16)


def _kv_proj(ckv, kpe_slot, w_ext, *, tm):
    n, kvr = ckv.shape
    width = N_HEADS * HEAD_SLOT
    return pl.pallas_call(
        functools.partial(_kv_proj_kernel, n_heads=N_HEADS),
        grid=(n // tm,),
        in_specs=[
            pl.BlockSpec((tm, kvr), lambda i: (i, 0)),
            pl.BlockSpec((tm, HEAD_SLOT), lambda i: (i, 0)),
            pl.BlockSpec(w_ext.shape, lambda i: (0, 0)),
        ],
        out_specs=[pl.BlockSpec((tm, width), lambda i: (i, 0)),
                   pl.BlockSpec((tm, width), lambda i: (i, 0))],
        out_shape=[jax.ShapeDtypeStruct((n, width), BF16),
                   jax.ShapeDtypeStruct((n, width), BF16)],
        compiler_params=_cparams("parallel"),
        name="kv_proj",
    )(ckv, kpe_slot, w_ext)


def _attn_prompt_kernel(q_ref, k_ref, v_ref, o_ref, *, tq, n_heads):
    qi = pl.program_id(1)
    row = lax.broadcasted_iota(jnp.int32, (tq, tq), 0)
    col = lax.broadcasted_iota(jnp.int32, (tq, tq), 1)
    causal = col <= row

    head_lanes = [slice(h * HEAD_SLOT, (h + 1) * HEAD_SLOT) for h in range(n_heads)]

    def step(kt, carry, masked):
        ms, accs = carry
        start = pl.multiple_of(kt * tq, tq)
        new_ms, new_accs = [], []
        for h, lanes in enumerate(head_lanes):
            k = k_ref[pl.ds(start, tq), lanes]
            v = v_ref[pl.ds(start, tq), lanes]
            s = lax.dot_general(q_ref[:, lanes], k, (((1,), (1,)), ((), ())), preferred_element_type=F32)
            if masked:
                s = jnp.where(causal, s, NEG_BIG)
            m_new = jnp.maximum(ms[h], jnp.max(s, axis=-1, keepdims=True))
            p = jnp.exp2(s - m_new)
            new_accs.append(jnp.exp2(ms[h] - m_new) * accs[h]
                            + jnp.dot(p.astype(BF16), v, preferred_element_type=F32))
            new_ms.append(m_new)
        return tuple(new_ms), tuple(new_accs)

    init = (tuple(jnp.full((tq, 1), NEG_BIG, F32) for _ in head_lanes),
            tuple(jnp.zeros((tq, HEAD_SLOT), F32) for _ in head_lanes))
    carry = lax.fori_loop(0, qi, functools.partial(step, masked=False), init)
    _, accs = step(qi, carry, True)
    for lanes, acc in zip(head_lanes, accs):
        o_ref[:, lanes] = (acc / acc[:, V_DIM:V_DIM + 1]).astype(BF16)


def _attn_prompt(q, k, v, *, bsz, t_len, tq):
    width = q.shape[1]
    nq = t_len // tq
    return pl.pallas_call(
        functools.partial(_attn_prompt_kernel, tq=tq, n_heads=N_HEADS),
        grid=(bsz, nq),
        in_specs=[
            pl.BlockSpec((tq, width), lambda b, i: (b * nq + i, 0)),
            pl.BlockSpec((t_len, width), lambda b, i: (b, 0)),
            pl.BlockSpec((t_len, width), lambda b, i: (b, 0)),
        ],
        out_specs=pl.BlockSpec((tq, width), lambda b, i: (b * nq + i, 0)),
        out_shape=jax.ShapeDtypeStruct(q.shape, BF16),
        compiler_params=_cparams("parallel", "arbitrary"),
        name="attn_prompt",
    )(q, k, v)


def _qlat_kernel(q_ref, w_ref, o_ref, *, n_heads):
    for h in range(n_heads):
        o_ref[h] = jnp.dot(q_ref[:, h * HEAD_SLOT:(h + 1) * HEAD_SLOT], w_ref[h],
                           preferred_element_type=F32).astype(BF16)


def _qlat(q_slots, w_q2lat):
    n = q_slots.shape[0]
    n_heads, _, kvr = w_q2lat.shape
    return pl.pallas_call(
        functools.partial(_qlat_kernel, n_heads=n_heads),
        out_shape=jax.ShapeDtypeStruct((n_heads, n, kvr), BF16),
        compiler_params=pltpu.CompilerParams(vmem_limit_bytes=VMEM_LIMIT),
        name="q_latent",
    )(q_slots, w_q2lat)


def _attn_sample_kernel(pt_ref, ql_ref, qp_ref, cn_ref, kn_ref, ckv_hbm, kpe_hbm, o_ref,
                        cbuf_ref, pbuf_ref, kbf_ref, pbf_ref, m_ref, l_ref, acc_ref, sem,
                        *, layer, pages_per_step, pages_per_chunk, page, t_new, t_pad):
    g = pages_per_step
    bsz, n_pages = pt_ref.shape
    steps_per_seq = n_pages // g
    total = bsz * steps_per_seq

    def page_copies(i, slot):
        b = i // steps_per_seq
        first = (i % steps_per_seq) * g
        copies = []
        for j in range(g):
            pid = pt_ref[b, first + j]
            copies.append(pltpu.make_async_copy(
                ckv_hbm.at[layer, pid], cbuf_ref.at[slot, pl.ds(j * page, page)], sem.at[slot]))
            copies.append(pltpu.make_async_copy(
                kpe_hbm.at[layer, pid], pbuf_ref.at[slot, j], sem.at[slot]))
        return copies

    for cp in page_copies(0, 0):
        cp.start()

    def step(i, carry):
        slot = i % 2
        b = i // steps_per_seq
        s_idx = i % steps_per_seq

        @pl.when(i + 1 < total)
        def _():
            for cp in page_copies(i + 1, 1 - slot):
                cp.start()

        pltpu.make_async_copy(cbuf_ref.at[slot], cbuf_ref.at[slot], sem.at[slot]).wait()
        pltpu.make_async_copy(pbuf_ref.at[slot], pbuf_ref.at[slot], sem.at[slot]).wait()
        _attn_sample_step(b, s_idx, slot, steps_per_seq, ql_ref, qp_ref, cn_ref, kn_ref, o_ref, cbuf_ref,
                          pbuf_ref, kbf_ref, pbf_ref, m_ref, l_ref, acc_ref, g=g,
                          pages_per_chunk=pages_per_chunk, page=page, t_new=t_new, t_pad=t_pad)
        return carry

    lax.fori_loop(0, total, step, 0)


def _attn_sample_step(b, s_idx, slot, steps_per_seq, ql_ref, qp_ref, cn_ref, kn_ref, o_ref, cbuf_ref, pbuf_ref,
                      kbf_ref, pbf_ref, m_ref, l_ref, acc_ref, *, g, pages_per_chunk, page, t_new, t_pad):
    ql = ql_ref[b]
    qp = qp_ref[b]
    rows = ql.shape[0]
    contract_last = (((1,), (1,)), ((), ()))

    @pl.when(s_idx == 0)
    def _():
        cn = cn_ref[b].astype(BF16)
        kn = kn_ref[b].astype(BF16)
        s = (lax.dot_general(ql, cn, contract_last, preferred_element_type=F32)
             + lax.dot_general(qp, kn, contract_last, preferred_element_type=F32))
        t_of_row = lax.broadcasted_iota(jnp.int32, (rows, t_pad), 0) % t_pad
        key = lax.broadcasted_iota(jnp.int32, (rows, t_pad), 1)
        ok = (key <= t_of_row) & (key < t_new)
        s = jnp.where(ok, s, NEG_BIG)
        m = jnp.max(s, axis=-1, keepdims=True)
        p = jnp.where(ok, jnp.exp2(s - m), 0.0)
        m_ref[...] = m
        l_ref[...] = jnp.sum(p, axis=-1, keepdims=True)
        acc_ref[...] = jnp.dot(p.astype(BF16), cn, preferred_element_type=F32)

    kbf_ref[...] = cbuf_ref[slot].astype(BF16)
    for j in range(g):
        pbf_ref[:, j * page:(j + 1) * page] = pbuf_ref[slot, j].astype(BF16)

    ck = pages_per_chunk * page
    parts = []
    for c in range(g // pages_per_chunk):
        kb = kbf_ref[c * ck:(c + 1) * ck, :]
        s = (lax.dot_general(ql, kb, contract_last, preferred_element_type=F32)
             + jnp.dot(qp, pbf_ref[:, c * ck:(c + 1) * ck], preferred_element_type=F32))
        mc = jnp.max(s, axis=-1, keepdims=True)
        p = jnp.exp2(s - mc)
        parts.append((mc, jnp.sum(p, axis=-1, keepdims=True),
                      jnp.dot(p.astype(BF16), kb, preferred_element_type=F32)))
    m_old = m_ref[...]
    m_new = m_old
    for mc, _, _ in parts:
        m_new = jnp.maximum(m_new, mc)
    alpha = jnp.exp2(m_old - m_new)
    l = alpha * l_ref[...]
    acc = alpha * acc_ref[...]
    for mc, lc, ac in parts:
        w = jnp.exp2(mc - m_new)
        l = l + w * lc
        acc = acc + w * ac
    m_ref[...] = m_new
    l_ref[...] = l
    acc_ref[...] = acc

    @pl.when(s_idx == steps_per_seq - 1)
    def _():
        o_ref[b] = acc_ref[...] / l_ref[...]


def _attn_sample(page_table, q_lat, q_pe, ckv_new, kpe_new, cache_ckv, cache_kpe_t, layer, *,
                 t_new, t_pad, pages_per_step, pages_per_chunk):
    bsz, rows, kvr = q_lat.shape
    rope = q_pe.shape[2]
    page = cache_ckv.shape[2]
    g = pages_per_step
    assert page_table.shape[1] % g == 0 and g % pages_per_chunk == 0
    vmem = pl.BlockSpec(memory_space=pltpu.VMEM)
    hbm = pl.BlockSpec(memory_space=pl.ANY)
    return pl.pallas_call(
        functools.partial(_attn_sample_kernel, layer=layer, pages_per_step=g, pages_per_chunk=pages_per_chunk,
                          page=page, t_new=t_new, t_pad=t_pad),
        in_specs=[pl.BlockSpec(memory_space=pltpu.SMEM), vmem, vmem, vmem, vmem, hbm, hbm],
        out_specs=vmem,
        out_shape=jax.ShapeDtypeStruct((bsz, rows, kvr), F32),
        scratch_shapes=[
            pltpu.VMEM((2, g * page, kvr), F32),
            pltpu.VMEM((2, g, rope, page), F32),
            pltpu.VMEM((g * page, kvr), BF16),
            pltpu.VMEM((rope, g * page), BF16),
            pltpu.VMEM((rows, 1), F32),
            pltpu.VMEM((rows, 1), F32),
            pltpu.VMEM((rows, kvr), F32),
            pltpu.SemaphoreType.DMA((2,)),
        ],
        compiler_params=pltpu.CompilerParams(vmem_limit_bytes=VMEM_LIMIT),
        name="attn_sample",
    )(page_table, q_lat, q_pe, ckv_new, kpe_new, cache_ckv, cache_kpe_t)


def _olat_kernel(o_ref, w_ref, out_ref, *, n_heads):
    for h in range(n_heads):
        out_ref[:, h * HEAD_SLOT:(h + 1) * HEAD_SLOT] = jnp.dot(
            o_ref[h].astype(BF16), w_ref[h], preferred_element_type=F32).astype(BF16)


def _olat(o_lat_heads, w_uv_pad):
    n_heads, n, _ = o_lat_heads.shape
    return pl.pallas_call(
        functools.partial(_olat_kernel, n_heads=n_heads),
        out_shape=jax.ShapeDtypeStruct((n, n_heads * HEAD_SLOT), BF16),
        compiler_params=pltpu.CompilerParams(vmem_limit_bytes=VMEM_LIMIT),
        name="o_latent",
    )(o_lat_heads, w_uv_pad)


def _out_proj_kernel(x_ref, c_ref, o_ref, wc_ref, wo_ref, g_ref, x1_ref, h_ref):
    mix = (jnp.dot(c_ref[...], wc_ref[...], preferred_element_type=F32)
           + jnp.dot(o_ref[...], wo_ref[...], preferred_element_type=F32))
    x1 = x_ref[...] + mix
    x1_ref[...] = x1
    h_ref[...] = _rms(x1, g_ref[...]).astype(BF16)


def _out_proj(x, c, o_slots, w_c, w_o, g, *, tm):
    n, d = x.shape
    row = lambda i: (i, 0)
    full = lambda i: (0, 0)
    return pl.pallas_call(
        _out_proj_kernel,
        grid=(n // tm,),
        in_specs=[
            pl.BlockSpec((tm, d), row),
            pl.BlockSpec((tm, c.shape[1]), row),
            pl.BlockSpec((tm, o_slots.shape[1]), row),
            pl.BlockSpec(w_c.shape, full),
            pl.BlockSpec(w_o.shape, full),
            pl.BlockSpec((1, d), full),
        ],
        out_specs=[pl.BlockSpec((tm, d), row), pl.BlockSpec((tm, d), row)],
        out_shape=[jax.ShapeDtypeStruct((n, d), F32), jax.ShapeDtypeStruct((n, d), BF16)],
        compiler_params=_cparams("parallel"),
        name="out_proj",
    )(x, c, o_slots, w_c, w_o, g)


def _swiglu_rows(x, wg, wu, wd):
    a = jnp.dot(x, wg, preferred_element_type=F32)
    b = jnp.dot(x, wu, preferred_element_type=F32)
    mid = (a * jax.nn.sigmoid(a) * b).astype(BF16)
    return jnp.dot(mid, wd, preferred_element_type=F32)


def _ffn_kernel(x_ref, res_ref, wg_hbm, wu_hbm, wd_hbm, o_ref, wg_ref, wu_ref, wd_ref, sem, *, row_splits):
    @pl.when(pl.program_id(0) == 0)
    def _():
        copies = [pltpu.make_async_copy(src, dst, sem.at[j]) for j, (src, dst) in enumerate(
            ((wg_hbm, wg_ref), (wu_hbm, wu_ref), (wd_hbm, wd_ref)))]
        for cp in copies:
            cp.start()
        for cp in copies:
            cp.wait()

    hm = o_ref.shape[0] // row_splits
    for r in range(row_splits):
        rows = slice(r * hm, (r + 1) * hm)
        o_ref[rows, :] = res_ref[rows, :] + _swiglu_rows(x_ref[rows, :], wg_ref[...], wu_ref[...], wd_ref[...])


def _ffn(x, res, wg, wu, wd, *, tm):
    n, d = x.shape
    d_ff = wg.shape[1]
    row = lambda i: (i, 0)
    hbm = pl.BlockSpec(memory_space=pl.ANY)
    return pl.pallas_call(
        functools.partial(_ffn_kernel, row_splits=1),
        grid=(n // tm,),
        in_specs=[pl.BlockSpec((tm, d), row), pl.BlockSpec((tm, d), row), hbm, hbm, hbm],
        out_specs=pl.BlockSpec((tm, d), row),
        out_shape=jax.ShapeDtypeStruct((n, d), F32),
        scratch_shapes=[pltpu.VMEM((d, d_ff), BF16), pltpu.VMEM((d, d_ff), BF16), pltpu.VMEM((d_ff, d), BF16),
                        pltpu.SemaphoreType.DMA((3,))],
        compiler_params=_cparams("arbitrary"),
        name="ffn_dense",
    )(x, res, wg, wu, wd)


def _swiglu_kernel(te_ref, nv_ref, tr_ref, x_ref, wg_ref, wu_ref, wd_ref, o_ref, xb_ref, *, row_groups,
                   full_splits):
    m = pl.program_id(0)
    f = pl.program_id(1)
    tm = o_ref.shape[0]
    hm = tm // row_groups

    @pl.when(f == 0)
    def _():
        o_ref[...] = jnp.zeros_like(o_ref)

    valid = m < nv_ref[0]
    groups_needed = (tr_ref[jnp.minimum(m, nv_ref[0] - 1)] + hm - 1) // hm

    @pl.when(valid & (f == 0))
    def _():
        xb_ref[...] = x_ref[...].astype(BF16)

    @pl.when(valid & (groups_needed == row_groups))
    def _():
        wg = wg_ref[...].astype(BF16)
        wu = wu_ref[...].astype(BF16)
        wd = wd_ref[...].astype(BF16)
        fm = tm // full_splits
        for r in range(full_splits):
            rows = slice(r * fm, (r + 1) * fm)
            o_ref[rows, :] += _swiglu_rows(xb_ref[rows, :], wg, wu, wd)

    @pl.when(valid & (groups_needed < row_groups))
    def _():
        wg = wg_ref[...].astype(BF16)
        wu = wu_ref[...].astype(BF16)
        wd = wd_ref[...].astype(BF16)

        def group(r, carry):
            rows = pl.ds(pl.multiple_of(r * hm, hm), hm)
            o_ref[rows, :] += _swiglu_rows(xb_ref[rows, :], wg, wu, wd)
            return carry

        lax.fori_loop(0, groups_needed, group, 0)


def _swiglu(tile_expert, n_valid, tile_rows, x, wg, wu, wd, *, tm, tf):
    p, d = x.shape
    d_ff = wg.shape[2]
    n_f = d_ff // tf
    n_tiles = p // tm

    def mm(m, nv):
        return jnp.minimum(m, nv[0] - 1)

    def ff(m, f, nv):
        return jnp.where(m < nv[0], f, n_f - 1)

    return pl.pallas_call(
        functools.partial(_swiglu_kernel, row_groups=2 if tm % 32 == 0 else 1, full_splits=1),
        grid_spec=pltpu.PrefetchScalarGridSpec(
            num_scalar_prefetch=3,
            grid=(n_tiles, n_f),
            in_specs=[
                pl.BlockSpec((tm, d), lambda m, f, te, nv, tr: (mm(m, nv), 0)),
                pl.BlockSpec((None, d, tf), lambda m, f, te, nv, tr: (te[mm(m, nv)], 0, ff(m, f, nv))),
                pl.BlockSpec((None, d, tf), lambda m, f, te, nv, tr: (te[mm(m, nv)], 0, ff(m, f, nv))),
                pl.BlockSpec((None, tf, d), lambda m, f, te, nv, tr: (te[mm(m, nv)], ff(m, f, nv), 0)),
            ],
            out_specs=pl.BlockSpec((tm, d), lambda m, f, te, nv, tr: (m, 0)),
            scratch_shapes=[pltpu.VMEM((tm, d), BF16)],
        ),
        out_shape=jax.ShapeDtypeStruct((p, d), F32),
        compiler_params=_cparams("arbitrary", "arbitrary"),
        name="swiglu_experts",
    )(tile_expert, n_valid, tile_rows, x, wg, wu, wd)


def _row_copy(src, src_row, dst, dst_row, sem):
    return pltpu.make_async_copy(src.at[pl.ds(src_row, 1)], dst.at[pl.ds(dst_row, 1)], sem)


def _dispatch_kernel(pos_ref, h_ref, xs_in_ref, xs_ref, sem, *, tt):
    del xs_in_ref

    def body(r, carry):
        for k in range(TOP_K):
            _row_copy(h_ref, r, xs_ref, pos_ref[0, TOP_K * r + k], sem).start(priority=k % 2)
        return carry

    lax.fori_loop(0, tt, body, 0, unroll=8)
    for k in range(TOP_K):
        pltpu.make_async_copy(h_ref, xs_ref.at[pl.ds(0, tt)], sem).wait()


def _dispatch(pos_tiles, h, xs):
    n, d = h.shape
    n_tiles = pos_tiles.shape[0]
    tt = n // n_tiles
    return pl.pallas_call(
        functools.partial(_dispatch_kernel, tt=tt),
        grid=(n_tiles,),
        in_specs=[
            pl.BlockSpec((None, 1, TOP_K * tt), lambda i: (i, 0, 0), memory_space=pltpu.SMEM),
            pl.BlockSpec((tt, d), lambda i: (i, 0)),
            pl.BlockSpec(memory_space=pl.ANY),
        ],
        out_specs=pl.BlockSpec(memory_space=pl.ANY),
        out_shape=jax.ShapeDtypeStruct(xs.shape, xs.dtype),
        scratch_shapes=[pltpu.SemaphoreType.DMA],
        input_output_aliases={2: 0},
        compiler_params=_cparams("arbitrary"),
        name="moe_dispatch",
    )(pos_tiles, h, xs)


def _pool_kernel(*refs, tm, pos0, windows, has_halo, n_experts, seqs):
    if has_halo:
        (prev_ref, halo_ref, x_ref, g1_ref, wp_ref, sc_ref, g2_ref, wr_ref,
         x3_ref, h_ref, hn_ref, gate_ref, idx_ref, ext_ref, lv_ref) = refs
    else:
        (prev_ref, x_ref, g1_ref, wp_ref, sc_ref, g2_ref, wr_ref,
         x3_ref, h_ref, hn_ref, gate_ref, idx_ref, ext_ref, lv_ref) = refs
    for sq in range(seqs):
        _pool_sequence(sq, prev_ref, halo_ref if has_halo else None, x_ref, g1_ref, wp_ref, sc_ref, g2_ref,
                       wr_ref, x3_ref, h_ref, hn_ref, gate_ref, idx_ref, ext_ref, lv_ref,
                       tm=tm, pos0=pos0, windows=windows, n_experts=n_experts)


def _pool_sequence(sq, prev_ref, halo_ref, x_ref, g1_ref, wp_ref, sc_ref, g2_ref, wr_ref,
                   x3_ref, h_ref, hn_ref, gate_ref, idx_ref, ext_ref, lv_ref, *, tm, pos0, windows, n_experts):
    t = pl.program_id(1)
    x = x_ref[sq]
    hn = _rms(x, g1_ref[...])
    hn_ref[sq] = hn
    top = SUBLANES
    base = top + POOL_HALO
    end = base + tm
    d = x.shape[1]
    ext_ref[0:top, :] = jnp.zeros((top, d), F32)
    if halo_ref is not None:
        @pl.when(t == 0)
        def _():
            ext_ref[top:base, :] = prev_ref[sq]

        @pl.when(t > 0)
        def _():
            ext_ref[top:base, :] = _rms(halo_ref[sq], g1_ref[...])
    else:
        ext_ref[top:base, :] = prev_ref[sq]
    ext_ref[base:end, :] = hn

    gc = d // len(windows)

    def level_rows(k, lo_row, hi_row, lanes):
        if k == 0:
            return ext_ref[lo_row:hi_row, lanes]
        return lv_ref[k - 1, lo_row:hi_row, lanes]

    for k in range(1, len(windows)):
        lanes = slice((k - 1) * gc, d)
        shift = 1 << (k - 1)
        lv_ref[k - 1, 0:top, lanes] = jnp.zeros((top, d - (k - 1) * gc), F32)
        lv_ref[k - 1, top:end, lanes] = (level_rows(k - 1, top, end, lanes)
                                         + level_rows(k - 1, top - shift, end - shift, lanes))

    pos = pos0 + t * tm + lax.broadcasted_iota(jnp.int32, (tm, 1), 0)
    for gi, w in enumerate(windows):
        lanes = slice(gi * gc, (gi + 1) * gc)
        if gi + 1 < len(windows):
            s = level_rows(gi + 1, base, end, lanes)
        else:
            s = level_rows(gi, base, end, lanes) + level_rows(gi, base - w // 2, end - w // 2, lanes)
        cnt = jnp.minimum(pos + 1, w).astype(F32)
        diff = s / cnt - hn[:, lanes]
        y = jnp.dot(diff.astype(BF16), wp_ref[gi], preferred_element_type=F32)
        x3_ref[sq, :, lanes] = x[:, lanes] + y * sc_ref[:, lanes]

    x3 = x3_ref[sq]
    h2 = _rms(x3, g2_ref[...])
    h_ref[sq] = h2
    logits = jnp.dot(h2, wr_ref[...], preferred_element_type=F32, precision=lax.Precision.HIGHEST)
    lane = lax.broadcasted_iota(jnp.int32, logits.shape, 1)
    logits = jnp.where(lane < n_experts, logits, -jnp.inf)
    m1 = jnp.max(logits, axis=-1, keepdims=True)
    i1 = jnp.min(jnp.where(logits == m1, lane, LANES), axis=-1, keepdims=True)
    rest = jnp.where(lane == i1, -jnp.inf, logits)
    m2 = jnp.max(rest, axis=-1, keepdims=True)
    i2 = jnp.min(jnp.where(rest == m2, lane, LANES), axis=-1, keepdims=True)
    e = jnp.exp(m2 - m1)
    g1 = 1.0 / (1.0 + e)
    g2 = e / (1.0 + e)
    gate_ref[sq] = jnp.where(lane == 0, g1, jnp.where(lane == 1, g2, 0.0))
    idx_ref[sq] = jnp.where(lane == 0, i1, jnp.where(lane == 1, i2, 0))


def _pool_layer(prev_pad, x, g1, w_pool, scale, g2, w_router_pad, *, tm, pos0, n_experts):
    bsz, t_len, d = x.shape
    assert all(w == 2 ** (g + 1) for g, w in enumerate(POOL_WINDOWS)) and POOL_HALO >= max(POOL_WINDOWS) - 1
    n_t = t_len // tm
    has_halo = n_t > 1
    hb = tm // POOL_HALO if has_halo else 1
    n_rows = SUBLANES + POOL_HALO + tm
    seqs = _seqs_per_step(bsz, n_t)
    cur = lambda b, t: (b, t, 0)
    full2 = lambda b, t: (0, 0)
    in_specs = [pl.BlockSpec((seqs, POOL_HALO, d), lambda b, t: (b, 0, 0))]
    args = [prev_pad]
    if has_halo:
        in_specs.append(pl.BlockSpec((seqs, POOL_HALO, d), lambda b, t: (b, jnp.maximum(t * hb - 1, 0), 0)))
        args.append(x)
    in_specs += [
        pl.BlockSpec((seqs, tm, d), cur),
        pl.BlockSpec((1, d), full2),
        pl.BlockSpec(w_pool.shape, lambda b, t: (0, 0, 0)),
        pl.BlockSpec((1, d), full2),
        pl.BlockSpec((1, d), full2),
        pl.BlockSpec(w_router_pad.shape, full2),
    ]
    args += [x, g1, w_pool, scale, g2, w_router_pad]
    return pl.pallas_call(
        functools.partial(_pool_kernel, tm=tm, pos0=pos0, windows=POOL_WINDOWS, has_halo=has_halo,
                          n_experts=n_experts, seqs=seqs),
        grid=(bsz // seqs, n_t),
        in_specs=in_specs,
        out_specs=[
            pl.BlockSpec((seqs, tm, d), cur),
            pl.BlockSpec((seqs, tm, d), cur),
            pl.BlockSpec((seqs, tm, d), cur),
            pl.BlockSpec((seqs, tm, LANES), cur),
            pl.BlockSpec((seqs, tm, LANES), cur),
        ],
        out_shape=[
            jax.ShapeDtypeStruct((bsz, t_len, d), F32),
            jax.ShapeDtypeStruct((bsz, t_len, d), F32),
            jax.ShapeDtypeStruct((bsz, t_len, d), F32),
            jax.ShapeDtypeStruct((bsz, t_len, LANES), F32),
            jax.ShapeDtypeStruct((bsz, t_len, LANES), jnp.int32),
        ],
        scratch_shapes=[pltpu.VMEM((n_rows, d), F32), pltpu.VMEM((len(POOL_WINDOWS) - 1, n_rows, d), F32)],
        compiler_params=_cparams("parallel", "parallel"),
        name="pool_layer",
    )(*args)


def _combine_kernel(pos_cur_ref, pos_nxt_ref, x_ref, gate_ref, g_ref, ys_ref, y_ref, buf_ref, sem,
                    *, tt, apply_norm):
    i = pl.program_id(0)
    n = pl.num_programs(0)
    slot = i % 2

    def fetch(pos_ref, s):
        def body(r, carry):
            for k in range(TOP_K):
                _row_copy(ys_ref, pos_ref[0, TOP_K * r + k], buf_ref.at[s, k], r,
                          sem.at[s]).start(priority=k % 2)
            return carry
        lax.fori_loop(0, tt, body, 0, unroll=8)

    @pl.when(i == 0)
    def _():
        fetch(pos_cur_ref, 0)

    @pl.when(i + 1 < n)
    def _():
        fetch(pos_nxt_ref, 1 - slot)

    for k in range(TOP_K):
        pltpu.make_async_copy(ys_ref.at[pl.ds(0, tt)], buf_ref.at[slot, k], sem.at[slot]).wait()
    moe = gate_ref[:, 0:1] * buf_ref[slot, 0]
    for k in range(1, TOP_K):
        moe = moe + gate_ref[:, k:k + 1] * buf_ref[slot, k]
    y = x_ref[...] + moe
    y_ref[...] = _rms(y, g_ref[...]) if apply_norm else y


def _combine(pos_tiles, x, gates, g, ys, *, apply_norm):
    n, d = x.shape
    n_tiles = pos_tiles.shape[0]
    tt = n // n_tiles
    pos_spec = lambda fn: pl.BlockSpec((None, 1, TOP_K * tt), fn, memory_space=pltpu.SMEM)
    return pl.pallas_call(
        functools.partial(_combine_kernel, tt=tt, apply_norm=apply_norm),
        grid=(n_tiles,),
        in_specs=[
            pos_spec(lambda i: (i, 0, 0)),
            pos_spec(lambda i: (jnp.minimum(i + 1, n_tiles - 1), 0, 0)),
            pl.BlockSpec((tt, d), lambda i: (i, 0)),
            pl.BlockSpec((tt, LANES), lambda i: (i, 0)),
            pl.BlockSpec((1, d), lambda i: (0, 0)),
            pl.BlockSpec(memory_space=pl.ANY),
        ],
        out_specs=pl.BlockSpec((tt, d), lambda i: (i, 0)),
        out_shape=jax.ShapeDtypeStruct((n, d), F32),
        scratch_shapes=[pltpu.VMEM((2, TOP_K, tt, d), F32), pltpu.SemaphoreType.DMA((2,))],
        compiler_params=_cparams("arbitrary"),
        name="moe_combine",
    )(pos_tiles, pos_tiles, x, gates, g, ys)


def _final_kernel(x_ref, g_ref, y_ref):
    y_ref[...] = _rms(x_ref[...], g_ref[...])


def _final(x, g, *, tm):
    n, d = x.shape
    row = lambda i: (i, 0)
    return pl.pallas_call(
        _final_kernel,
        grid=(n // tm,),
        in_specs=[pl.BlockSpec((tm, d), row), pl.BlockSpec((1, d), lambda i: (0, 0))],
        out_specs=pl.BlockSpec((tm, d), row),
        out_shape=jax.ShapeDtypeStruct((n, d), F32),
        compiler_params=_cparams("parallel"),
        name="final_norm",
    )(x, g)


def _swap_halves(w):
    half = w.shape[-1] // 2
    return jnp.concatenate([w[..., half:], w[..., :half]], axis=-1)


def _slot(parts, total=HEAD_SLOT):
    used = sum(p.shape[-1] for p in parts)
    pad = jnp.zeros(parts[0].shape[:-1] + (total - used,), parts[0].dtype)
    return jnp.concatenate(list(parts) + [pad], axis=-1)


def _prep_even_weights(w_in, w_uq, w_ukv, w_out, *, dc, qr, kvr):
    d = w_in.shape[0]
    body = w_in[:, :2 * dc + qr + kvr]
    w_kpe = w_in[:, 2 * dc + qr + kvr:]
    z_nope = jnp.zeros((d, NOPE_DIM), w_in.dtype)
    w_in_ext = jnp.concatenate(
        [body, _slot([z_nope, w_kpe]), _slot([z_nope, _swap_halves(w_kpe)])], axis=-1).astype(BF16)

    uq = w_uq.reshape(qr, N_HEADS, NOPE_DIM + ROPE_DIM)
    uq_nope, uq_pe = uq[..., :NOPE_DIM], uq[..., NOPE_DIM:]
    qa = _slot([uq_nope, uq_pe]).reshape(qr, N_HEADS * HEAD_SLOT)
    qb = _slot([jnp.zeros_like(uq_nope), _swap_halves(uq_pe)]).reshape(qr, N_HEADS * HEAD_SLOT)
    w_uq_ext = jnp.concatenate([qa, qb], axis=-1).astype(BF16)

    ukv = w_ukv.reshape(kvr, N_HEADS, NOPE_DIM + V_DIM)
    uk, uv = ukv[..., :NOPE_DIM], ukv[..., NOPE_DIM:]
    w_kv_ext = jnp.concatenate([_slot([uk]).reshape(kvr, N_HEADS * HEAD_SLOT),
                                _slot([uv]).reshape(kvr, N_HEADS * HEAD_SLOT)], axis=-1).astype(BF16)
    w_q2lat = jnp.concatenate(
        [jnp.transpose(uk, (1, 2, 0)),
         jnp.zeros((N_HEADS, HEAD_SLOT - NOPE_DIM, kvr), w_ukv.dtype)], axis=1).astype(BF16)
    w_uv_pad = _slot([jnp.transpose(uv, (1, 0, 2))]).astype(BF16)

    w_c = w_out[:dc].astype(BF16)
    wo = w_out[dc:].reshape(N_HEADS, V_DIM, d)
    w_o = jnp.concatenate([wo, jnp.zeros((N_HEADS, HEAD_SLOT - V_DIM, d), w_out.dtype)],
                          axis=1).reshape(N_HEADS * HEAD_SLOT, d).astype(BF16)
    return w_in_ext, w_uq_ext, w_kv_ext, w_q2lat, w_uv_pad, w_c, w_o


def _rope_tables(pos):
    half = ROPE_DIM // 2
    inv_freq = jnp.power(ROPE_THETA, -jnp.arange(half, dtype=F32) / half)
    ang = pos.astype(F32)[:, None] * inv_freq[None, :]
    cos, sin = jnp.cos(ang), jnp.sin(ang)
    n = pos.shape[0]
    z_nope = jnp.zeros((n, NOPE_DIM), F32)
    cos_k = _slot([z_nope, cos, cos])
    sin_k = _slot([z_nope, -sin, sin])
    cos_q = _slot([jnp.ones((n, NOPE_DIM), F32), cos, cos]) * (ATTN_SCALE * LOG2E)
    sin_q = sin_k * (ATTN_SCALE * LOG2E)
    return cos_k, sin_k, cos_q, sin_q


def _even_layer(x, conv_prev, pos, attend, P, layer, i):
    bsz, t_len, d = x.shape
    n = bsz * t_len
    dc = P['conv_w'].shape[2]
    qr = P['q_norm'].shape[1]
    kvr = P['kv_norm'].shape[1]
    k_taps = P['conv_w'].shape[1]
    tm = next((t for t in (PROJ_TILE_ROWS, FFN_TILE_ROWS) if t_len % t == 0), n)
    w_in_ext, w_uq_ext, w_kv_ext, w_q2lat, w_uv_pad, w_c, w_o = _prep_even_weights(
        P['w_in'][i], P['w_uq'][i], P['w_ukv'][i], P['w_out'][i], dc=dc, qr=qr, kvr=kvr)
    cos_k, sin_k, cos_q, sin_q = _rope_tables(pos)
    if t_len < tm:
        reps = tm // t_len
        cos_k, sin_k, cos_q, sin_q = (jnp.tile(a, (reps, 1)) for a in (cos_k, sin_k, cos_q, sin_q))

    xf = x.reshape(n, d)
    u, qn, ckv, kpe_slot = _in_proj(
        xf, P['norm_mix'][layer][None], w_in_ext, P['q_norm'][i][None], P['kv_norm'][i][None],
        cos_k, sin_k, dc=dc, qr=qr, kvr=kvr, tm=tm)

    u3 = u.reshape(bsz, t_len, dc)
    prev_pad = jnp.pad(conv_prev.astype(F32), ((0, 0), (CONV_HALO - (k_taps - 1), 0), (0, 0)))
    c = _conv_branch(prev_pad, u3, P['conv_w'][i], P['conv_b'][i][None], P['conv_ln_g'][i][None],
                     P['conv_ln_b'][i][None], tc=min(t_len, 512))
    conv_state = jnp.concatenate([conv_prev.astype(F32), u3], axis=1)[:, -(k_taps - 1):]

    q_slots = _q_proj(qn, w_uq_ext, cos_q, sin_q, tm=tm)
    o_slots = attend(q_slots, ckv, kpe_slot, w_kv_ext, w_q2lat, w_uv_pad, tm)

    x1, h_ffn = _out_proj(xf, c.reshape(n, dc), o_slots, w_c, w_o, P['norm_ffn'][layer][None], tm=tm)

    x2 = _ffn(h_ffn, x1, P['ffn_w_gate'][i].astype(BF16), P['ffn_w_up'][i].astype(BF16),
              P['ffn_w_down'][i].astype(BF16), tm=_row_tile(n, FFN_TILE_ROWS))
    kpe = kpe_slot[:, NOPE_DIM:NOPE_DIM + ROPE_DIM]
    return (x2.reshape(bsz, t_len, d), conv_state, ckv.reshape(bsz, t_len, kvr),
            kpe.reshape(bsz, t_len, ROPE_DIM))


def _make_attend_prompt(bsz, t_len):
    def attend(q_slots, ckv, kpe_slot, w_kv_ext, w_q2lat, w_uv_pad, tm):
        k_slots, v_slots = _kv_proj(ckv, kpe_slot, w_kv_ext, tm=tm)
        return _attn_prompt(q_slots, k_slots, v_slots, bsz=bsz, t_len=t_len, tq=min(t_len, 512))
    return attend


def _make_attend_sample(bsz, t_len, cache_ckv, cache_kpe_t, page_table, i):
    t_pad = 8

    def attend(q_slots, ckv, kpe_slot, w_kv_ext, w_q2lat, w_uv_pad, tm):
        n = bsz * t_len
        kvr = ckv.shape[1]
        q_lat = _qlat(q_slots, w_q2lat)
        q_lat = q_lat.reshape(N_HEADS, bsz, t_len, kvr)
        q_lat = jnp.pad(q_lat, ((0, 0), (0, 0), (0, t_pad - t_len), (0, 0)))
        q_lat = jnp.transpose(q_lat, (1, 0, 2, 3)).reshape(bsz, N_HEADS * t_pad, kvr)
        q_pe = q_slots.reshape(bsz, t_len, N_HEADS, HEAD_SLOT)[..., NOPE_DIM:NOPE_DIM + ROPE_DIM]
        q_pe = jnp.pad(jnp.transpose(q_pe, (0, 2, 1, 3)), ((0, 0), (0, 0), (0, t_pad - t_len), (0, 0)))
        q_pe = q_pe.reshape(bsz, N_HEADS * t_pad, ROPE_DIM)
        pad_t = ((0, 0), (0, t_pad - t_len), (0, 0))
        ckv_new = jnp.pad(ckv.reshape(bsz, t_len, kvr), pad_t)
        kpe_new = jnp.pad(kpe_slot[:, NOPE_DIM:NOPE_DIM + ROPE_DIM].reshape(bsz, t_len, ROPE_DIM), pad_t)
        o_lat = _attn_sample(page_table, q_lat, q_pe, ckv_new, kpe_new, cache_ckv, cache_kpe_t, i,
                             t_new=t_len, t_pad=t_pad, pages_per_step=64,
                             pages_per_chunk=64)
        o_lat = o_lat.reshape(bsz, N_HEADS, t_pad, kvr)[:, :, :t_len]
        o_lat = jnp.transpose(o_lat, (1, 0, 2, 3)).reshape(N_HEADS, n, kvr)
        return _olat(o_lat, w_uv_pad)
    return attend


def _odd_layer_mixer(x, pool_prev, pos0, P, layer, i):
    bsz, t_len, d = x.shape
    n_experts = P['router_w'].shape[2]
    ctx = max(POOL_WINDOWS) - 1
    prev_pad = jnp.pad(pool_prev.astype(F32), ((0, 0), (POOL_HALO - ctx, 0), (0, 0)))
    w_router_pad = jnp.pad(P['router_w'][i], ((0, 0), (0, LANES - n_experts)))
    x3, h_moe, hn, gates, ids = _pool_layer(
        prev_pad, x, P['norm_mix'][layer][None], P['pool_w'][i].astype(BF16), P['pool_scale'][i][None],
        P['norm_ffn'][layer][None], w_router_pad, tm=min(t_len, 512), pos0=pos0, n_experts=n_experts)
    pool_state = jnp.concatenate([pool_prev.astype(F32), hn], axis=1)[:, -ctx:]
    n = bsz * t_len
    return (x3.reshape(n, d), h_moe.reshape(n, d), pool_state, gates.reshape(n, LANES),
            ids.reshape(n, LANES)[:, :TOP_K])


def _moe_plan(ids, n_experts, tm):
    n = ids.shape[0]
    flat = ids.reshape(-1)
    experts = jnp.arange(n_experts, dtype=jnp.int32)
    onehot = (flat[:, None] == experts[None, :]).astype(jnp.int32)
    csum = jnp.cumsum(onehot, axis=0)
    counts = csum[-1]
    rank = jnp.sum(csum * onehot, axis=1) - 1
    padded = ((counts + tm - 1) // tm) * tm
    ends = jnp.cumsum(padded)
    starts = ends - padded
    pos = jnp.sum(starts[None, :] * onehot, axis=1) + rank
    n_tiles = (n * TOP_K + tm - 1) // tm + n_experts
    n_valid = (ends[-1] // tm).astype(jnp.int32)
    tile_start = jnp.arange(n_tiles, dtype=jnp.int32) * tm
    tile_expert = jnp.sum((ends[None, :] <= tile_start[:, None]).astype(jnp.int32), axis=1)
    tile_expert = jnp.minimum(tile_expert, n_experts - 1)
    group_end = starts + counts
    tile_onehot = (tile_expert[:, None] == experts[None, :]).astype(jnp.int32)
    tile_rows = jnp.clip(jnp.sum(group_end[None, :] * tile_onehot, axis=1) - tile_start, 0, tm)
    return pos.reshape(n, TOP_K), tile_expert, n_valid.reshape(1), tile_rows.astype(jnp.int32), n_tiles


def _moe_layer(groups, P, i, g_final, *, tm, tt):
    d = groups[0][0].shape[1]
    n_experts = P['router_w'].shape[2]
    ids = jnp.concatenate([g[3] for g in groups], axis=0)
    pos, tile_expert, n_valid, tile_rows, n_tiles = _moe_plan(ids, n_experts, tm)
    xs = jnp.zeros((n_tiles * tm, d), F32)
    pos_tiles = []
    lo = 0
    for x3, h, _, _ in groups:
        n = x3.shape[0]
        t = min(tt, n)
        pt = pos[lo:lo + n].reshape(n // t, 1, TOP_K * t)
        xs = _dispatch(pt, h, xs)
        pos_tiles.append(pt)
        lo += n
    ys = _swiglu(tile_expert, n_valid, tile_rows, xs, P['moe_w_gate'][i], P['moe_w_up'][i],
                 P['moe_w_down'][i], tm=tm, tf=512)
    gain = g_final if g_final is not None else jnp.ones((1, d), F32)
    return [_combine(pt, x3, gates, gain, ys, apply_norm=g_final is not None)
            for pt, (x3, _, gates, _) in zip(pos_tiles, groups)]


def kernel(x_prompt, x_sample, cache_ckv, cache_kpe, page_table, state_conv, state_pool, norm_mix, norm_ffn,
           norm_final, w_in, conv_w, conv_b, conv_ln_g, conv_ln_b, q_norm, w_uq, kv_norm, w_ukv, w_out,
           ffn_w_gate, ffn_w_up, ffn_w_down, pool_w, pool_scale, router_w, moe_w_gate, moe_w_up, moe_w_down):
    P = {'norm_mix': norm_mix, 'norm_ffn': norm_ffn, 'norm_final': norm_final, 'w_in': w_in,
         'conv_w': conv_w, 'conv_b': conv_b, 'conv_ln_g': conv_ln_g, 'conv_ln_b': conv_ln_b,
         'q_norm': q_norm, 'w_uq': w_uq, 'kv_norm': kv_norm, 'w_ukv': w_ukv, 'w_out': w_out,
         'ffn_w_gate': ffn_w_gate, 'ffn_w_up': ffn_w_up, 'ffn_w_down': ffn_w_down,
         'pool_w': pool_w, 'pool_scale': pool_scale, 'router_w': router_w,
         'moe_w_gate': moe_w_gate, 'moe_w_up': moe_w_up, 'moe_w_down': moe_w_down}
    depth = norm_mix.shape[0]
    d = x_prompt.shape[2]
    bp, tp, _ = x_prompt.shape
    bs, ts, _ = x_sample.shape
    past = page_table.shape[1] * cache_ckv.shape[2]
    k_taps = conv_w.shape[1]
    ctx = max(POOL_WINDOWS) - 1
    pos_p = jnp.arange(tp, dtype=jnp.int32)
    pos_s = past + jnp.arange(ts, dtype=jnp.int32)
    cache_kpe_t = jnp.swapaxes(cache_kpe, 2, 3)

    xp, xs = x_prompt, x_sample
    outs_p = {'ckv': [], 'kpe': [], 'conv': [], 'pool': []}
    outs_s = {'ckv': [], 'kpe': [], 'conv': [], 'pool': []}
    for layer in range(depth):
        i = layer // 2
        last = layer == depth - 1
        if layer % 2 == 0:
            conv0 = jnp.zeros((bp, k_taps - 1, conv_w.shape[2]), F32)
            xp, cst, ckv, kpe = _even_layer(xp, conv0, pos_p, _make_attend_prompt(bp, tp), P, layer, i)
            outs_p['conv'].append(cst); outs_p['ckv'].append(ckv); outs_p['kpe'].append(kpe)
            xs, cst, ckv, kpe = _even_layer(
                xs, state_conv[i], pos_s, _make_attend_sample(bs, ts, cache_ckv, cache_kpe_t, page_table, i),
                P, layer, i)
            outs_s['conv'].append(cst); outs_s['ckv'].append(ckv); outs_s['kpe'].append(kpe)
            if last:
                xp = _final(xp.reshape(bp * tp, d), norm_final[None], tm=_row_tile(bp * tp, 512)).reshape(xp.shape)
                xs = _final(xs.reshape(bs * ts, d), norm_final[None], tm=_row_tile(bs * ts, 512)).reshape(xs.shape)
        else:
            pool0 = jnp.zeros((bp, ctx, d), F32)
            x3p, hp, pst_p, gp, ip = _odd_layer_mixer(xp, pool0, 0, P, layer, i)
            x3s, hs, pst_s, gs, is_ = _odd_layer_mixer(xs, state_pool[i], past, P, layer, i)
            outs_p['pool'].append(pst_p); outs_s['pool'].append(pst_s)
            yp, ysm = _moe_layer([(x3p, hp, gp, ip), (x3s, hs, gs, is_)], P, i,
                                 norm_final[None] if last else None, tm=1024, tt=512)
            xp, xs = yp.reshape(xp.shape), ysm.reshape(xs.shape)

    def stack(lst, shape_if_empty):
        return jnp.stack(lst) if lst else jnp.zeros(shape_if_empty, F32)

    return (xp, xs,
            jnp.stack(outs_p['ckv']), jnp.stack(outs_p['kpe']), jnp.stack(outs_p['conv']),
            stack(outs_p['pool'], (0, bp, ctx, d)),
            jnp.stack(outs_s['ckv']), jnp.stack(outs_s['kpe']), jnp.stack(outs_s['conv']),
            stack(outs_s['pool'], (0, bs, ctx, d)))
```

```python
import functools

import jax
import jax.numpy as jnp
from jax import lax
from jax.experimental import pallas as pl
from jax.experimental.pallas import tpu as pltpu

F32 = jnp.float32
BF16 = jnp.bfloat16

N_HEADS = 8
NOPE_DIM = 64
ROPE_DIM = 32
V_DIM = 64
ROPE_THETA = 10000.0
ATTN_SCALE = (NOPE_DIM + ROPE_DIM) ** -0.5
LOG2E = 1.4426950408889634
POOL_WINDOWS = (2, 4, 8, 16)
TOP_K = 2
EPS = 1e-6

HEAD_SLOT = 128
LANES = 128
SUBLANES = 8
NEG_BIG = -1e30
VMEM_LIMIT = 56 * 1024 * 1024

PROJ_TILE_ROWS = 1024
FFN_TILE_ROWS = 512
CONV_HALO = 32
POOL_HALO = 16


def _cparams(*semantics):
    return pltpu.CompilerParams(dimension_semantics=semantics, vmem_limit_bytes=VMEM_LIMIT)


def _rms(x, g):
    return x * lax.rsqrt(jnp.mean(x * x, axis=-1, keepdims=True) + EPS) * g


def _row_tile(n, want):
    if n <= want:
        return n
    t = want
    while t >= 8:
        if n % t == 0 and t % 8 == 0:
            return t
        t -= 8
    return n


def _in_proj_kernel(x_ref, g_ref, w_ref, qg_ref, kvg_ref, cos_ref, sin_ref,
                    u_ref, qn_ref, ckv_ref, kpe_ref, *, dc, qr, kvr):
    h = _rms(x_ref[...], g_ref[...])
    proj = jnp.dot(h.astype(BF16), w_ref[...], preferred_element_type=F32)
    a = proj[:, :dc]
    gate = proj[:, dc:2 * dc]
    u_ref[...] = a * jax.nn.sigmoid(gate)
    o = 2 * dc
    qn_ref[...] = _rms(proj[:, o:o + qr], qg_ref[...]).astype(BF16)
    o += qr
    ckv_ref[...] = _rms(proj[:, o:o + kvr], kvg_ref[...])
    o += kvr
    kpe_ref[...] = (proj[:, o:o + HEAD_SLOT] * cos_ref[...]
                    + proj[:, o + HEAD_SLOT:o + 2 * HEAD_SLOT] * sin_ref[...])


def _in_proj(x, g, w_ext, qg, kvg, cos_k, sin_k, *, dc, qr, kvr, tm):
    n, d = x.shape
    n_tab = cos_k.shape[0] // tm
    row = lambda i: (i, 0)
    full = lambda i: (0, 0)
    tab = lambda i: (i % n_tab, 0)
    return pl.pallas_call(
        functools.partial(_in_proj_kernel, dc=dc, qr=qr, kvr=kvr),
        grid=(n // tm,),
        in_specs=[
            pl.BlockSpec((tm, d), row),
            pl.BlockSpec((1, d), full),
            pl.BlockSpec(w_ext.shape, full),
            pl.BlockSpec((1, qr), full),
            pl.BlockSpec((1, kvr), full),
            pl.BlockSpec((tm, HEAD_SLOT), tab),
            pl.BlockSpec((tm, HEAD_SLOT), tab),
        ],
        out_specs=[
            pl.BlockSpec((tm, dc), row),
            pl.BlockSpec((tm, qr), row),
            pl.BlockSpec((tm, kvr), row),
            pl.BlockSpec((tm, HEAD_SLOT), row),
        ],
        out_shape=[
            jax.ShapeDtypeStruct((n, dc), F32),
            jax.ShapeDtypeStruct((n, qr), BF16),
            jax.ShapeDtypeStruct((n, kvr), F32),
            jax.ShapeDtypeStruct((n, HEAD_SLOT), F32),
        ],
        compiler_params=_cparams("parallel"),
        name="in_proj",
    )(x, g, w_ext, qg, kvg, cos_k, sin_k)


def _conv_kernel(*refs, tc, k_taps, rows_per_chunk, has_halo, seqs):
    if has_halo:
        prev_ref, halo_ref, u_ref, w_ref, b_ref, lg_ref, lb_ref, c_ref, ext_ref, sh_ref = refs
    else:
        prev_ref, u_ref, w_ref, b_ref, lg_ref, lb_ref, c_ref, ext_ref, sh_ref = refs
    n_rows = CONV_HALO + tc
    first = CONV_HALO - (k_taps - 1)

    def window(start, n):
        q, s = divmod(start, SUBLANES)
        if s == 0:
            return ext_ref[start:start + n, :]
        return sh_ref[s - 1, q * SUBLANES:q * SUBLANES + n, :]

    for sq in range(seqs):
        if has_halo:
            t = pl.program_id(1)

            @pl.when(t == 0)
            def _():
                ext_ref[0:CONV_HALO, :] = prev_ref[sq]

            @pl.when(t > 0)
            def _():
                ext_ref[0:CONV_HALO, :] = halo_ref[sq]
        else:
            ext_ref[0:CONV_HALO, :] = prev_ref[sq]
        ext_ref[CONV_HALO:n_rows, :] = u_ref[sq]

        for s in range(1, SUBLANES):
            sh_ref[s - 1, 0:n_rows - SUBLANES, :] = ext_ref[s:s + n_rows - SUBLANES, :]

        for c0 in range(0, tc, rows_per_chunk):
            rc = min(rows_per_chunk, tc - c0)
            acc = w_ref[0:1, :] * window(first + c0, rc)
            for k in range(1, k_taps):
                acc = acc + w_ref[k:k + 1, :] * window(first + c0 + k, rc)
            acc = acc + b_ref[...]
            mu = jnp.mean(acc, axis=-1, keepdims=True)
            xc = acc - mu
            y = xc * lax.rsqrt(jnp.mean(xc * xc, axis=-1, keepdims=True) + EPS)
            y = y * lg_ref[...] + lb_ref[...]
            c_ref[sq, c0:c0 + rc, :] = (y * jax.nn.sigmoid(y)).astype(BF16)


def _seqs_per_step(bsz, n_t, want=8):
    if n_t > 1:
        return 1
    g = min(want, bsz)
    while bsz % g:
        g -= 1
    return g


def _conv_branch(prev_pad, u, w, b, lg, lb, *, tc):
    bsz, t_len, c = u.shape
    k_taps = w.shape[0]
    n_t = t_len // tc
    has_halo = n_t > 1
    hb = tc // CONV_HALO if has_halo else 1
    seqs = _seqs_per_step(bsz, n_t)
    in_specs = [pl.BlockSpec((seqs, CONV_HALO, c), lambda bi, ti: (bi, 0, 0))]
    args = [prev_pad]
    if has_halo:
        in_specs.append(pl.BlockSpec((seqs, CONV_HALO, c),
                                     lambda bi, ti: (bi, jnp.maximum(ti * hb - 1, 0), 0)))
        args.append(u)
    in_specs += [
        pl.BlockSpec((seqs, tc, c), lambda bi, ti: (bi, ti, 0)),
        pl.BlockSpec((k_taps, c), lambda bi, ti: (0, 0)),
        pl.BlockSpec((1, c), lambda bi, ti: (0, 0)),
        pl.BlockSpec((1, c), lambda bi, ti: (0, 0)),
        pl.BlockSpec((1, c), lambda bi, ti: (0, 0)),
    ]
    args += [u, w, b, lg, lb]
    return pl.pallas_call(
        functools.partial(_conv_kernel, tc=tc, k_taps=k_taps, rows_per_chunk=64, has_halo=has_halo, seqs=seqs),
        grid=(bsz // seqs, n_t),
        in_specs=in_specs,
        out_specs=pl.BlockSpec((seqs, tc, c), lambda bi, ti: (bi, ti, 0)),
        out_shape=jax.ShapeDtypeStruct((bsz, t_len, c), BF16),
        scratch_shapes=[pltpu.VMEM((CONV_HALO + tc, c), F32),
                        pltpu.VMEM((SUBLANES - 1, CONV_HALO + tc, c), F32)],
        compiler_params=_cparams("parallel", "parallel"),
        name="conv_branch",
    )(*args)


def _q_proj_kernel(qn_ref, w_ref, cos_ref, sin_ref, q_ref, *, n_heads):
    proj = jnp.dot(qn_ref[...], w_ref[...], preferred_element_type=F32)
    half = n_heads * HEAD_SLOT
    cos = cos_ref[...]
    sin = sin_ref[...]
    for h in range(n_heads):
        a = proj[:, h * HEAD_SLOT:(h + 1) * HEAD_SLOT]
        b = proj[:, half + h * HEAD_SLOT:half + (h + 1) * HEAD_SLOT]
        q_ref[:, h * HEAD_SLOT:(h + 1) * HEAD_SLOT] = (a * cos + b * sin).astype(BF16)


def _q_proj(qn, w_ext, cos_q, sin_q, *, tm):
    n, qr = qn.shape
    n_tab = cos_q.shape[0] // tm
    width = N_HEADS * HEAD_SLOT
    return pl.pallas_call(
        functools.partial(_q_proj_kernel, n_heads=N_HEADS),
        grid=(n // tm,),
        in_specs=[
            pl.BlockSpec((tm, qr), lambda i: (i, 0)),
            pl.BlockSpec(w_ext.shape, lambda i: (0, 0)),
            pl.BlockSpec((tm, HEAD_SLOT), lambda i: (i % n_tab, 0)),
            pl.BlockSpec((tm, HEAD_SLOT), lambda i: (i % n_tab, 0)),
        ],
        out_specs=pl.BlockSpec((tm, width), lambda i: (i, 0)),
        out_shape=jax.ShapeDtypeStruct((n, width), BF16),
        compiler_params=_cparams("parallel"),
        name="q_proj",
    )(qn, w_ext, cos_q, sin_q)


def _kv_proj_kernel(ckv_ref, kpe_ref, w_ref, k_ref, v_ref, *, n_heads):
    proj = jnp.dot(ckv_ref[...].astype(BF16), w_ref[...], preferred_element_type=F32)
    half = n_heads * HEAD_SLOT
    kpe = kpe_ref[...]
    for h in range(n_heads):
        k_ref[:, h * HEAD_SLOT:(h + 1) * HEAD_SLOT] = (
            proj[:, h * HEAD_SLOT:(h + 1) * HEAD_SLOT] + kpe).astype(BF16)
    v = proj[:, half:]
    lane = lax.broadcasted_iota(jnp.int32, v.shape, 1)
    v_ref[...] = jnp.where(lane % HEAD_SLOT == V_DIM, 1.0, v).astype(BF16)


def _kv_proj(ckv, kpe_slot, w_ext, *, tm):
    n, kvr = ckv.shape
    width = N_HEADS * HEAD_SLOT
    return pl.pallas_call(
        functools.partial(_kv_proj_kernel, n_heads=N_HEADS),
        grid=(n // tm,),
        in_specs=[
            pl.BlockSpec((tm, kvr), lambda i: (i, 0)),
            pl.BlockSpec((tm, HEAD_SLOT), lambda i: (i, 0)),
            pl.BlockSpec(w_ext.shape, lambda i: (0, 0)),
        ],
        out_specs=[pl.BlockSpec((tm, width), lambda i: (i, 0)),
                   pl.BlockSpec((tm, width), lambda i: (i, 0))],
        out_shape=[jax.ShapeDtypeStruct((n, width), BF16),
                   jax.ShapeDtypeStruct((n, width), BF16)],
        compiler_params=_cparams("parallel"),
        name="kv_proj",
    )(ckv, kpe_slot, w_ext)


def _attn_prompt_kernel(q_ref, k_ref, v_ref, o_ref, *, tq, n_heads):
    qi = pl.program_id(1)
    row = lax.broadcasted_iota(jnp.int32, (tq, tq), 0)
    col = lax.broadcasted_iota(jnp.int32, (tq, tq), 1)
    causal = col <= row

    head_lanes = [slice(h * HEAD_SLOT, (h + 1) * HEAD_SLOT) for h in range(n_heads)]

    def step(kt, carry, masked):
        ms, accs = carry
        start = pl.multiple_of(kt * tq, tq)
        new_ms, new_accs = [], []
        for h, lanes in enumerate(head_lanes):
            k = k_ref[pl.ds(start, tq), lanes]
            v = v_ref[pl.ds(start, tq), lanes]
            s = lax.dot_general(q_ref[:, lanes], k, (((1,), (1,)), ((), ())), preferred_element_type=F32)
            if masked:
                s = jnp.where(causal, s, NEG_BIG)
            m_new = jnp.maximum(ms[h], jnp.max(s, axis=-1, keepdims=True))
            p = jnp.exp2(s - m_new)
            new_accs.append(jnp.exp2(ms[h] - m_new) * accs[h]
                            + jnp.dot(p.astype(BF16), v, preferred_element_type=F32))
            new_ms.append(m_new)
        return tuple(new_ms), tuple(new_accs)

    init = (tuple(jnp.full((tq, 1), NEG_BIG, F32) for _ in head_lanes),
            tuple(jnp.zeros((tq, HEAD_SLOT), F32) for _ in head_lanes))
    carry = lax.fori_loop(0, qi, functools.partial(step, masked=False), init)
    _, accs = step(qi, carry, True)
    for lanes, acc in zip(head_lanes, accs):
        o_ref[:, lanes] = (acc / acc[:, V_DIM:V_DIM + 1]).astype(BF16)


def _attn_prompt(q, k, v, *, bsz, t_len, tq):
    width = q.shape[1]
    nq = t_len // tq
    return pl.pallas_call(
        functools.partial(_attn_prompt_kernel, tq=tq, n_heads=N_HEADS),
        grid=(bsz, nq),
        in_specs=[
            pl.BlockSpec((tq, width), lambda b, i: (b * nq + i, 0)),
            pl.BlockSpec((t_len, width), lambda b, i: (b, 0)),
            pl.BlockSpec((t_len, width), lambda b, i: (b, 0)),
        ],
        out_specs=pl.BlockSpec((tq, width), lambda b, i: (b * nq + i, 0)),
        out_shape=jax.ShapeDtypeStruct(q.shape, BF16),
        compiler_params=_cparams("parallel", "arbitrary"),
        name="attn_prompt",
    )(q, k, v)


def _qlat_kernel(q_ref, w_ref, o_ref, *, n_heads):
    for h in range(n_heads):
        o_ref[h] = jnp.dot(q_ref[:, h * HEAD_SLOT:(h + 1) * HEAD_SLOT], w_ref[h],
                           preferred_element_type=F32).astype(BF16)


def _qlat(q_slots, w_q2lat):
    n = q_slots.shape[0]
    n_heads, _, kvr = w_q2lat.shape
    return pl.pallas_call(
        functools.partial(_qlat_kernel, n_heads=n_heads),
        out_shape=jax.ShapeDtypeStruct((n_heads, n, kvr), BF16),
        compiler_params=pltpu.CompilerParams(vmem_limit_bytes=VMEM_LIMIT),
        name="q_latent",
    )(q_slots, w_q2lat)


def _attn_sample_kernel(pt_ref, ql_ref, qp_ref, cn_ref, kn_ref, ckv_hbm, kpe_hbm, o_ref,
                        cbuf_ref, pbuf_ref, kbf_ref, pbf_ref, m_ref, l_ref, acc_ref, sem,
                        *, layer, pages_per_step, pages_per_chunk, page, t_new, t_pad):
    g = pages_per_step
    bsz, n_pages = pt_ref.shape
    steps_per_seq = n_pages // g
    total = bsz * steps_per_seq

    def page_copies(i, slot):
        b = i // steps_per_seq
        first = (i % steps_per_seq) * g
        copies = []
        for j in range(g):
            pid = pt_ref[b, first + j]
            copies.append(pltpu.make_async_copy(
                ckv_hbm.at[layer, pid], cbuf_ref.at[slot, pl.ds(j * page, page)], sem.at[slot]))
            copies.append(pltpu.make_async_copy(
                kpe_hbm.at[layer, pid], pbuf_ref.at[slot, j], sem.at[slot]))
        return copies

    for cp in page_copies(0, 0):
        cp.start()

    def step(i, carry):
        slot = i % 2
        b = i // steps_per_seq
        s_idx = i % steps_per_seq

        @pl.when(i + 1 < total)
        def _():
            for cp in page_copies(i + 1, 1 - slot):
                cp.start()

        pltpu.make_async_copy(cbuf_ref.at[slot], cbuf_ref.at[slot], sem.at[slot]).wait()
        pltpu.make_async_copy(pbuf_ref.at[slot], pbuf_ref.at[slot], sem.at[slot]).wait()
        _attn_sample_step(b, s_idx, slot, steps_per_seq, ql_ref, qp_ref, cn_ref, kn_ref, o_ref, cbuf_ref,
                          pbuf_ref, kbf_ref, pbf_ref, m_ref, l_ref, acc_ref, g=g,
                          pages_per_chunk=pages_per_chunk, page=page, t_new=t_new, t_pad=t_pad)
        return carry

    lax.fori_loop(0, total, step, 0)


def _attn_sample_step(b, s_idx, slot, steps_per_seq, ql_ref, qp_ref, cn_ref, kn_ref, o_ref, cbuf_ref, pbuf_ref,
                      kbf_ref, pbf_ref, m_ref, l_ref, acc_ref, *, g, pages_per_chunk, page, t_new, t_pad):
    ql = ql_ref[b]
    qp = qp_ref[b]
    rows = ql.shape[0]
    contract_last = (((1,), (1,)), ((), ()))

    @pl.when(s_idx == 0)
    def _():
        cn = cn_ref[b].astype(BF16)
        kn = kn_ref[b].astype(BF16)
        s = (lax.dot_general(ql, cn, contract_last, preferred_element_type=F32)
             + lax.dot_general(qp, kn, contract_last, preferred_element_type=F32))
        t_of_row = lax.broadcasted_iota(jnp.int32, (rows, t_pad), 0) % t_pad
        key = lax.broadcasted_iota(jnp.int32, (rows, t_pad), 1)
        ok = (key <= t_of_row) & (key < t_new)
        s = jnp.where(ok, s, NEG_BIG)
        m = jnp.max(s, axis=-1, keepdims=True)
        p = jnp.where(ok, jnp.exp2(s - m), 0.0)
        m_ref[...] = m
        l_ref[...] = jnp.sum(p, axis=-1, keepdims=True)
        acc_ref[...] = jnp.dot(p.astype(BF16), cn, preferred_element_type=F32)

    kbf_ref[...] = cbuf_ref[slot].astype(BF16)
    for j in range(g):
        pbf_ref[:, j * page:(j + 1) * page] = pbuf_ref[slot, j].astype(BF16)

    ck = pages_per_chunk * page
    parts = []
    for c in range(g // pages_per_chunk):
        kb = kbf_ref[c * ck:(c + 1) * ck, :]
        s = (lax.dot_general(ql, kb, contract_last, preferred_element_type=F32)
             + jnp.dot(qp, pbf_ref[:, c * ck:(c + 1) * ck], preferred_element_type=F32))
        mc = jnp.max(s, axis=-1, keepdims=True)
        p = jnp.exp2(s - mc)
        parts.append((mc, jnp.sum(p, axis=-1, keepdims=True),
                      jnp.dot(p.astype(BF16), kb, preferred_element_type=F32)))
    m_old = m_ref[...]
    m_new = m_old
    for mc, _, _ in parts:
        m_new = jnp.maximum(m_new, mc)
    alpha = jnp.exp2(m_old - m_new)
    l = alpha * l_ref[...]
    acc = alpha * acc_ref[...]
    for mc, lc, ac in parts:
        w = jnp.exp2(mc - m_new)
        l = l + w * lc
        acc = acc + w * ac
    m_ref[...] = m_new
    l_ref[...] = l
    acc_ref[...] = acc

    @pl.when(s_idx == steps_per_seq - 1)
    def _():
        o_ref[b] = acc_ref[...] / l_ref[...]


def _attn_sample(page_table, q_lat, q_pe, ckv_new, kpe_new, cache_ckv, cache_kpe_t, layer, *,
                 t_new, t_pad, pages_per_step, pages_per_chunk):
    bsz, rows, kvr = q_lat.shape
    rope = q_pe.shape[2]
    page = cache_ckv.shape[2]
    g = pages_per_step
    assert page_table.shape[1] % g == 0 and g % pages_per_chunk == 0
    vmem = pl.BlockSpec(memory_space=pltpu.VMEM)
    hbm = pl.BlockSpec(memory_space=pl.ANY)
    return pl.pallas_call(
        functools.partial(_attn_sample_kernel, layer=layer, pages_per_step=g, pages_per_chunk=pages_per_chunk,
                          page=page, t_new=t_new, t_pad=t_pad),
        in_specs=[pl.BlockSpec(memory_space=pltpu.SMEM), vmem, vmem, vmem, vmem, hbm, hbm],
        out_specs=vmem,
        out_shape=jax.ShapeDtypeStruct((bsz, rows, kvr), F32),
        scratch_shapes=[
            pltpu.VMEM((2, g * page, kvr), F32),
            pltpu.VMEM((2, g, rope, page), F32),
            pltpu.VMEM((g * page, kvr), BF16),
            pltpu.VMEM((rope, g * page), BF16),
            pltpu.VMEM((rows, 1), F32),
            pltpu.VMEM((rows, 1), F32),
            pltpu.VMEM((rows, kvr), F32),
            pltpu.SemaphoreType.DMA((2,)),
        ],
        compiler_params=pltpu.CompilerParams(vmem_limit_bytes=VMEM_LIMIT),
        name="attn_sample",
    )(page_table, q_lat, q_pe, ckv_new, kpe_new, cache_ckv, cache_kpe_t)


def _olat_kernel(o_ref, w_ref, out_ref, *, n_heads):
    for h in range(n_heads):
        out_ref[:, h * HEAD_SLOT:(h + 1) * HEAD_SLOT] = jnp.dot(
            o_ref[h].astype(BF16), w_ref[h], preferred_element_type=F32).astype(BF16)


def _olat(o_lat_heads, w_uv_pad):
    n_heads, n, _ = o_lat_heads.shape
    return pl.pallas_call(
        functools.partial(_olat_kernel, n_heads=n_heads),
        out_shape=jax.ShapeDtypeStruct((n, n_heads * HEAD_SLOT), BF16),
        compiler_params=pltpu.CompilerParams(vmem_limit_bytes=VMEM_LIMIT),
        name="o_latent",
    )(o_lat_heads, w_uv_pad)


def _out_proj_kernel(x_ref, c_ref, o_ref, wc_ref, wo_ref, g_ref, x1_ref, h_ref):
    mix = (jnp.dot(c_ref[...], wc_ref[...], preferred_element_type=F32)
           + jnp.dot(o_ref[...], wo_ref[...], preferred_element_type=F32))
    x1 = x_ref[...] + mix
    x1_ref[...] = x1
    h_ref[...] = _rms(x1, g_ref[...]).astype(BF16)


def _out_proj(x, c, o_slots, w_c, w_o, g, *, tm):
    n, d = x.shape
    row = lambda i: (i, 0)
    full = lambda i: (0, 0)
    return pl.pallas_call(
        _out_proj_kernel,
        grid=(n // tm,),
        in_specs=[
            pl.BlockSpec((tm, d), row),
            pl.BlockSpec((tm, c.shape[1]), row),
            pl.BlockSpec((tm, o_slots.shape[1]), row),
            pl.BlockSpec(w_c.shape, full),
            pl.BlockSpec(w_o.shape, full),
            pl.BlockSpec((1, d), full),
        ],
        out_specs=[pl.BlockSpec((tm, d), row), pl.BlockSpec((tm, d), row)],
        out_shape=[jax.ShapeDtypeStruct((n, d), F32), jax.ShapeDtypeStruct((n, d), BF16)],
        compiler_params=_cparams("parallel"),
        name="out_proj",
    )(x, c, o_slots, w_c, w_o, g)


def _swiglu_rows(x, wg, wu, wd):
    a = jnp.dot(x, wg, preferred_element_type=F32)
    b = jnp.dot(x, wu, preferred_element_type=F32)
    mid = (a * jax.nn.sigmoid(a) * b).astype(BF16)
    return jnp.dot(mid, wd, preferred_element_type=F32)


def _ffn_kernel(x_ref, res_ref, wg_hbm, wu_hbm, wd_hbm, o_ref, wg_ref, wu_ref, wd_ref, sem, *, row_splits):
    @pl.when(pl.program_id(0) == 0)
    def _():
        copies = [pltpu.make_async_copy(src, dst, sem.at[j]) for j, (src, dst) in enumerate(
            ((wg_hbm, wg_ref), (wu_hbm, wu_ref), (wd_hbm, wd_ref)))]
        for cp in copies:
            cp.start()
        for cp in copies:
            cp.wait()

    hm = o_ref.shape[0] // row_splits
    for r in range(row_splits):
        rows = slice(r * hm, (r + 1) * hm)
        o_ref[rows, :] = res_ref[rows, :] + _swiglu_rows(x_ref[rows, :], wg_ref[...], wu_ref[...], wd_ref[...])


def _ffn(x, res, wg, wu, wd, *, tm):
    n, d = x.shape
    d_ff = wg.shape[1]
    row = lambda i: (i, 0)
    hbm = pl.BlockSpec(memory_space=pl.ANY)
    return pl.pallas_call(
        functools.partial(_ffn_kernel, row_splits=1),
        grid=(n // tm,),
        in_specs=[pl.BlockSpec((tm, d), row), pl.BlockSpec((tm, d), row), hbm, hbm, hbm],
        out_specs=pl.BlockSpec((tm, d), row),
        out_shape=jax.ShapeDtypeStruct((n, d), F32),
        scratch_shapes=[pltpu.VMEM((d, d_ff), BF16), pltpu.VMEM((d, d_ff), BF16), pltpu.VMEM((d_ff, d), BF16),
                        pltpu.SemaphoreType.DMA((3,))],
        compiler_params=_cparams("arbitrary"),
        name="ffn_dense",
    )(x, res, wg, wu, wd)


def _swiglu_kernel(te_ref, nv_ref, tr_ref, x_ref, wg_ref, wu_ref, wd_ref, o_ref, xb_ref, *, row_groups,
                   full_splits):
    m = pl.program_id(0)
    f = pl.program_id(1)
    tm = o_ref.shape[0]
    hm = tm // row_groups

    @pl.when(f == 0)
    def _():
        o_ref[...] = jnp.zeros_like(o_ref)

    valid = m < nv_ref[0]
    groups_needed = (tr_ref[jnp.minimum(m, nv_ref[0] - 1)] + hm - 1) // hm

    @pl.when(valid & (f == 0))
    def _():
        xb_ref[...] = x_ref[...].astype(BF16)

    @pl.when(valid & (groups_needed == row_groups))
    def _():
        wg = wg_ref[...].astype(BF16)
        wu = wu_ref[...].astype(BF16)
        wd = wd_ref[...].astype(BF16)
        fm = tm // full_splits
        for r in range(full_splits):
            rows = slice(r * fm, (r + 1) * fm)
            o_ref[rows, :] += _swiglu_rows(xb_ref[rows, :], wg, wu, wd)

    @pl.when(valid & (groups_needed < row_groups))
    def _():
        wg = wg_ref[...].astype(BF16)
        wu = wu_ref[...].astype(BF16)
        wd = wd_ref[...].astype(BF16)

        def group(r, carry):
            rows = pl.ds(pl.multiple_of(r * hm, hm), hm)
            o_ref[rows, :] += _swiglu_rows(xb_ref[rows, :], wg, wu, wd)
            return carry

        lax.fori_loop(0, groups_needed, group, 0)


def _swiglu(tile_expert, n_valid, tile_rows, x, wg, wu, wd, *, tm, tf):
    p, d = x.shape
    d_ff = wg.shape[2]
    n_f = d_ff // tf
    n_tiles = p // tm

    def mm(m, nv):
        return jnp.minimum(m, nv[0] - 1)

    def ff(m, f, nv):
        return jnp.where(m < nv[0], f, n_f - 1)

    return pl.pallas_call(
        functools.partial(_swiglu_kernel, row_groups=2 if tm % 32 == 0 else 1, full_splits=1),
        grid_spec=pltpu.PrefetchScalarGridSpec(
            num_scalar_prefetch=3,
            grid=(n_tiles, n_f),
            in_specs=[
                pl.BlockSpec((tm, d), lambda m, f, te, nv, tr: (mm(m, nv), 0)),
                pl.BlockSpec((None, d, tf), lambda m, f, te, nv, tr: (te[mm(m, nv)], 0, ff(m, f, nv))),
                pl.BlockSpec((None, d, tf), lambda m, f, te, nv, tr: (te[mm(m, nv)], 0, ff(m, f, nv))),
                pl.BlockSpec((None, tf, d), lambda m, f, te, nv, tr: (te[mm(m, nv)], ff(m, f, nv), 0)),
            ],
            out_specs=pl.BlockSpec((tm, d), lambda m, f, te, nv, tr: (m, 0)),
            scratch_shapes=[pltpu.VMEM((tm, d), BF16)],
        ),
        out_shape=jax.ShapeDtypeStruct((p, d), F32),
        compiler_params=_cparams("arbitrary", "arbitrary"),
        name="swiglu_experts",
    )(tile_expert, n_valid, tile_rows, x, wg, wu, wd)


def _row_copy(src, src_row, dst, dst_row, sem):
    return pltpu.make_async_copy(src.at[pl.ds(src_row, 1)], dst.at[pl.ds(dst_row, 1)], sem)


def _dispatch_kernel(pos_ref, h_ref, xs_in_ref, xs_ref, sem, *, tt):
    del xs_in_ref

    def body(r, carry):
        for k in range(TOP_K):
            _row_copy(h_ref, r, xs_ref, pos_ref[0, TOP_K * r + k], sem).start(priority=k % 2)
        return carry

    lax.fori_loop(0, tt, body, 0, unroll=8)
    for k in range(TOP_K):
        pltpu.make_async_copy(h_ref, xs_ref.at[pl.ds(0, tt)], sem).wait()


def _dispatch(pos_tiles, h, xs):
    n, d = h.shape
    n_tiles = pos_tiles.shape[0]
    tt = n // n_tiles
    return pl.pallas_call(
        functools.partial(_dispatch_kernel, tt=tt),
        grid=(n_tiles,),
        in_specs=[
            pl.BlockSpec((None, 1, TOP_K * tt), lambda i: (i, 0, 0), memory_space=pltpu.SMEM),
            pl.BlockSpec((tt, d), lambda i: (i, 0)),
            pl.BlockSpec(memory_space=pl.ANY),
        ],
        out_specs=pl.BlockSpec(memory_space=pl.ANY),
        out_shape=jax.ShapeDtypeStruct(xs.shape, xs.dtype),
        scratch_shapes=[pltpu.SemaphoreType.DMA],
        input_output_aliases={2: 0},
        compiler_params=_cparams("arbitrary"),
        name="moe_dispatch",
    )(pos_tiles, h, xs)


def _pool_kernel(*refs, tm, pos0, windows, has_halo, n_experts, seqs):
    if has_halo:
        (prev_ref, halo_ref, x_ref, g1_ref, wp_ref, sc_ref, g2_ref, wr_ref,
         x3_ref, h_ref, hn_ref, gate_ref, idx_ref, ext_ref, lv_ref) = refs
    else:
        (prev_ref, x_ref, g1_ref, wp_ref, sc_ref, g2_ref, wr_ref,
         x3_ref, h_ref, hn_ref, gate_ref, idx_ref, ext_ref, lv_ref) = refs
    for sq in range(seqs):
        _pool_sequence(sq, prev_ref, halo_ref if has_halo else None, x_ref, g1_ref, wp_ref, sc_ref, g2_ref,
                       wr_ref, x3_ref, h_ref, hn_ref, gate_ref, idx_ref, ext_ref, lv_ref,
                       tm=tm, pos0=pos0, windows=windows, n_experts=n_experts)


def _pool_sequence(sq, prev_ref, halo_ref, x_ref, g1_ref, wp_ref, sc_ref, g2_ref, wr_ref,
                   x3_ref, h_ref, hn_ref, gate_ref, idx_ref, ext_ref, lv_ref, *, tm, pos0, windows, n_experts):
    t = pl.program_id(1)
    x = x_ref[sq]
    hn = _rms(x, g1_ref[...])
    hn_ref[sq] = hn
    top = SUBLANES
    base = top + POOL_HALO
    end = base + tm
    d = x.shape[1]
    ext_ref[0:top, :] = jnp.zeros((top, d), F32)
    if halo_ref is not None:
        @pl.when(t == 0)
        def _():
            ext_ref[top:base, :] = prev_ref[sq]

        @pl.when(t > 0)
        def _():
            ext_ref[top:base, :] = _rms(halo_ref[sq], g1_ref[...])
    else:
        ext_ref[top:base, :] = prev_ref[sq]
    ext_ref[base:end, :] = hn

    gc = d // len(windows)

    def level_rows(k, lo_row, hi_row, lanes):
        if k == 0:
            return ext_ref[lo_row:hi_row, lanes]
        return lv_ref[k - 1, lo_row:hi_row, lanes]

    for k in range(1, len(windows)):
        lanes = slice((k - 1) * gc, d)
        shift = 1 << (k - 1)
        lv_ref[k - 1, 0:top, lanes] = jnp.zeros((top, d - (k - 1) * gc), F32)
        lv_ref[k - 1, top:end, lanes] = (level_rows(k - 1, top, end, lanes)
                                         + level_rows(k - 1, top - shift, end - shift, lanes))

    pos = pos0 + t * tm + lax.broadcasted_iota(jnp.int32, (tm, 1), 0)
    for gi, w in enumerate(windows):
        lanes = slice(gi * gc, (gi + 1) * gc)
        if gi + 1 < len(windows):
            s = level_rows(gi + 1, base, end, lanes)
        else:
            s = level_rows(gi, base, end, lanes) + level_rows(gi, base - w // 2, end - w // 2, lanes)
        cnt = jnp.minimum(pos + 1, w).astype(F32)
        diff = s / cnt - hn[:, lanes]
        y = jnp.dot(diff.astype(BF16), wp_ref[gi], preferred_element_type=F32)
        x3_ref[sq, :, lanes] = x[:, lanes] + y * sc_ref[:, lanes]

    x3 = x3_ref[sq]
    h2 = _rms(x3, g2_ref[...])
    h_ref[sq] = h2
    h_hi = h2.astype(BF16)
    h_lo = (h2 - h_hi.astype(F32)).astype(BF16)
    wr = wr_ref[...]
    w_hi = wr.astype(BF16)
    w_lo = (wr - w_hi.astype(F32)).astype(BF16)
    logits = (jnp.dot(h_hi, w_hi, preferred_element_type=F32)
              + (jnp.dot(h_lo, w_hi, preferred_element_type=F32)
                 + jnp.dot(h_hi, w_lo, preferred_element_type=F32)))
    lane = lax.broadcasted_iota(jnp.int32, logits.shape, 1)
    logits = jnp.where(lane < n_experts, logits, -jnp.inf)
    m1 = jnp.max(logits, axis=-1, keepdims=True)
    i1 = jnp.min(jnp.where(logits == m1, lane, LANES), axis=-1, keepdims=True)
    rest = jnp.where(lane == i1, -jnp.inf, logits)
    m2 = jnp.max(rest, axis=-1, keepdims=True)
    i2 = jnp.min(jnp.where(rest == m2, lane, LANES), axis=-1, keepdims=True)
    e = jnp.exp(m2 - m1)
    g1 = 1.0 / (1.0 + e)
    g2 = e / (1.0 + e)
    gate_ref[sq] = jnp.where(lane == 0, g1, jnp.where(lane == 1, g2, 0.0))
    idx_ref[sq] = jnp.where(lane == 0, i1, jnp.where(lane == 1, i2, 0))


def _pool_layer(prev_pad, x, g1, w_pool, scale, g2, w_router_pad, *, tm, pos0, n_experts):
    bsz, t_len, d = x.shape
    assert all(w == 2 ** (g + 1) for g, w in enumerate(POOL_WINDOWS)) and POOL_HALO >= max(POOL_WINDOWS) - 1
    n_t = t_len // tm
    has_halo = n_t > 1
    hb = tm // POOL_HALO if has_halo else 1
    n_rows = SUBLANES + POOL_HALO + tm
    seqs = _seqs_per_step(bsz, n_t)
    cur = lambda b, t: (b, t, 0)
    full2 = lambda b, t: (0, 0)
    in_specs = [pl.BlockSpec((seqs, POOL_HALO, d), lambda b, t: (b, 0, 0))]
    args = [prev_pad]
    if has_halo:
        in_specs.append(pl.BlockSpec((seqs, POOL_HALO, d), lambda b, t: (b, jnp.maximum(t * hb - 1, 0), 0)))
        args.append(x)
    in_specs += [
        pl.BlockSpec((seqs, tm, d), cur),
        pl.BlockSpec((1, d), full2),
        pl.BlockSpec(w_pool.shape, lambda b, t: (0, 0, 0)),
        pl.BlockSpec((1, d), full2),
        pl.BlockSpec((1, d), full2),
        pl.BlockSpec(w_router_pad.shape, full2),
    ]
    args += [x, g1, w_pool, scale, g2, w_router_pad]
    return pl.pallas_call(
        functools.partial(_pool_kernel, tm=tm, pos0=pos0, windows=POOL_WINDOWS, has_halo=has_halo,
                          n_experts=n_experts, seqs=seqs),
        grid=(bsz // seqs, n_t),
        in_specs=in_specs,
        out_specs=[
            pl.BlockSpec((seqs, tm, d), cur),
            pl.BlockSpec((seqs, tm, d), cur),
            pl.BlockSpec((seqs, tm, d), cur),
            pl.BlockSpec((seqs, tm, LANES), cur),
            pl.BlockSpec((seqs, tm, LANES), cur),
        ],
        out_shape=[
            jax.ShapeDtypeStruct((bsz, t_len, d), F32),
            jax.ShapeDtypeStruct((bsz, t_len, d), F32),
            jax.ShapeDtypeStruct((bsz, t_len, d), F32),
            jax.ShapeDtypeStruct((bsz, t_len, LANES), F32),
            jax.ShapeDtypeStruct((bsz, t_len, LANES), jnp.int32),
        ],
        scratch_shapes=[pltpu.VMEM((n_rows, d), F32), pltpu.VMEM((len(POOL_WINDOWS) - 1, n_rows, d), F32)],
        compiler_params=_cparams("parallel", "parallel"),
        name="pool_layer",
    )(*args)


def _combine_kernel(pos_cur_ref, pos_nxt_ref, x_ref, gate_ref, g_ref, ys_ref, y_ref, buf_ref, sem,
                    *, tt, apply_norm):
    i = pl.program_id(0)
    n = pl.num_programs(0)
    slot = i % 2

    def fetch(pos_ref, s):
        def body(r, carry):
            for k in range(TOP_K):
                _row_copy(ys_ref, pos_ref[0, TOP_K * r + k], buf_ref.at[s, k], r,
                          sem.at[s]).start(priority=k % 2)
            return carry
        lax.fori_loop(0, tt, body, 0, unroll=8)

    @pl.when(i == 0)
    def _():
        fetch(pos_cur_ref, 0)

    @pl.when(i + 1 < n)
    def _():
        fetch(pos_nxt_ref, 1 - slot)

    for k in range(TOP_K):
        pltpu.make_async_copy(ys_ref.at[pl.ds(0, tt)], buf_ref.at[slot, k], sem.at[slot]).wait()
    moe = gate_ref[:, 0:1] * buf_ref[slot, 0]
    for k in range(1, TOP_K):
        moe = moe + gate_ref[:, k:k + 1] * buf_ref[slot, k]
    y = x_ref[...] + moe
    y_ref[...] = _rms(y, g_ref[...]) if apply_norm else y


def _combine(pos_tiles, x, gates, g, ys, *, apply_norm):
    n, d = x.shape
    n_tiles = pos_tiles.shape[0]
    tt = n // n_tiles
    pos_spec = lambda fn: pl.BlockSpec((None, 1, TOP_K * tt), fn, memory_space=pltpu.SMEM)
    return pl.pallas_call(
        functools.partial(_combine_kernel, tt=tt, apply_norm=apply_norm),
        grid=(n_tiles,),
        in_specs=[
            pos_spec(lambda i: (i, 0, 0)),
            pos_spec(lambda i: (jnp.minimum(i + 1, n_tiles - 1), 0, 0)),
            pl.BlockSpec((tt, d), lambda i: (i, 0)),
            pl.BlockSpec((tt, LANES), lambda i: (i, 0)),
            pl.BlockSpec((1, d), lambda i: (0, 0)),
            pl.BlockSpec(memory_space=pl.ANY),
        ],
        out_specs=pl.BlockSpec((tt, d), lambda i: (i, 0)),
        out_shape=jax.ShapeDtypeStruct((n, d), F32),
        scratch_shapes=[pltpu.VMEM((2, TOP_K, tt, d), F32), pltpu.SemaphoreType.DMA((2,))],
        compiler_params=_cparams("arbitrary"),
        name="moe_combine",
    )(pos_tiles, pos_tiles, x, gates, g, ys)


def _final_kernel(x_ref, g_ref, y_ref):
    y_ref[...] = _rms(x_ref[...], g_ref[...])


def _final(x, g, *, tm):
    n, d = x.shape
    row = lambda i: (i, 0)
    return pl.pallas_call(
        _final_kernel,
        grid=(n // tm,),
        in_specs=[pl.BlockSpec((tm, d), row), pl.BlockSpec((1, d), lambda i: (0, 0))],
        out_specs=pl.BlockSpec((tm, d), row),
        out_shape=jax.ShapeDtypeStruct((n, d), F32),
        compiler_params=_cparams("parallel"),
        name="final_norm",
    )(x, g)


def _swap_halves(w):
    half = w.shape[-1] // 2
    return jnp.concatenate([w[..., half:], w[..., :half]], axis=-1)


def _slot(parts, total=HEAD_SLOT):
    used = sum(p.shape[-1] for p in parts)
    pad = jnp.zeros(parts[0].shape[:-1] + (total - used,), parts[0].dtype)
    return jnp.concatenate(list(parts) + [pad], axis=-1)


def _prep_even_weights(w_in, w_uq, w_ukv, w_out, *, dc, qr, kvr):
    d = w_in.shape[0]
    body = w_in[:, :2 * dc + qr + kvr]
    w_kpe = w_in[:, 2 * dc + qr + kvr:]
    z_nope = jnp.zeros((d, NOPE_DIM), w_in.dtype)
    w_in_ext = jnp.concatenate(
        [body, _slot([z_nope, w_kpe]), _slot([z_nope, _swap_halves(w_kpe)])], axis=-1).astype(BF16)

    uq = w_uq.reshape(qr, N_HEADS, NOPE_DIM + ROPE_DIM)
    uq_nope, uq_pe = uq[..., :NOPE_DIM], uq[..., NOPE_DIM:]
    qa = _slot([uq_nope, uq_pe]).reshape(qr, N_HEADS * HEAD_SLOT)
    qb = _slot([jnp.zeros_like(uq_nope), _swap_halves(uq_pe)]).reshape(qr, N_HEADS * HEAD_SLOT)
    w_uq_ext = jnp.concatenate([qa, qb], axis=-1).astype(BF16)

    ukv = w_ukv.reshape(kvr, N_HEADS, NOPE_DIM + V_DIM)
    uk, uv = ukv[..., :NOPE_DIM], ukv[..., NOPE_DIM:]
    w_kv_ext = jnp.concatenate([_slot([uk]).reshape(kvr, N_HEADS * HEAD_SLOT),
                                _slot([uv]).reshape(kvr, N_HEADS * HEAD_SLOT)], axis=-1).astype(BF16)
    w_q2lat = jnp.concatenate(
        [jnp.transpose(uk, (1, 2, 0)),
         jnp.zeros((N_HEADS, HEAD_SLOT - NOPE_DIM, kvr), w_ukv.dtype)], axis=1).astype(BF16)
    w_uv_pad = _slot([jnp.transpose(uv, (1, 0, 2))]).astype(BF16)

    w_c = w_out[:dc].astype(BF16)
    wo = w_out[dc:].reshape(N_HEADS, V_DIM, d)
    w_o = jnp.concatenate([wo, jnp.zeros((N_HEADS, HEAD_SLOT - V_DIM, d), w_out.dtype)],
                          axis=1).reshape(N_HEADS * HEAD_SLOT, d).astype(BF16)
    return w_in_ext, w_uq_ext, w_kv_ext, w_q2lat, w_uv_pad, w_c, w_o


def _rope_tables(pos):
    half = ROPE_DIM // 2
    inv_freq = jnp.power(ROPE_THETA, -jnp.arange(half, dtype=F32) / half)
    ang = pos.astype(F32)[:, None] * inv_freq[None, :]
    cos, sin = jnp.cos(ang), jnp.sin(ang)
    n = pos.shape[0]
    z_nope = jnp.zeros((n, NOPE_DIM), F32)
    cos_k = _slot([z_nope, cos, cos])
    sin_k = _slot([z_nope, -sin, sin])
    cos_q = _slot([jnp.ones((n, NOPE_DIM), F32), cos, cos]) * (ATTN_SCALE * LOG2E)
    sin_q = sin_k * (ATTN_SCALE * LOG2E)
    return cos_k, sin_k, cos_q, sin_q


def _even_layer(x, conv_prev, pos, attend, P, layer, i):
    bsz, t_len, d = x.shape
    n = bsz * t_len
    dc = P['conv_w'].shape[2]
    qr = P['q_norm'].shape[1]
    kvr = P['kv_norm'].shape[1]
    k_taps = P['conv_w'].shape[1]
    tm = next((t for t in (PROJ_TILE_ROWS, FFN_TILE_ROWS) if t_len % t == 0), n)
    w_in_ext, w_uq_ext, w_kv_ext, w_q2lat, w_uv_pad, w_c, w_o = _prep_even_weights(
        P['w_in'][i], P['w_uq'][i], P['w_ukv'][i], P['w_out'][i], dc=dc, qr=qr, kvr=kvr)
    cos_k, sin_k, cos_q, sin_q = _rope_tables(pos)
    if t_len < tm:
        reps = tm // t_len
        cos_k, sin_k, cos_q, sin_q = (jnp.tile(a, (reps, 1)) for a in (cos_k, sin_k, cos_q, sin_q))

    xf = x.reshape(n, d)
    u, qn, ckv, kpe_slot = _in_proj(
        xf, P['norm_mix'][layer][None], w_in_ext, P['q_norm'][i][None], P['kv_norm'][i][None],
        cos_k, sin_k, dc=dc, qr=qr, kvr=kvr, tm=tm)

    u3 = u.reshape(bsz, t_len, dc)
    prev_pad = jnp.pad(conv_prev.astype(F32), ((0, 0), (CONV_HALO - (k_taps - 1), 0), (0, 0)))
    c = _conv_branch(prev_pad, u3, P['conv_w'][i], P['conv_b'][i][None], P['conv_ln_g'][i][None],
                     P['conv_ln_b'][i][None], tc=min(t_len, 512))
    conv_state = jnp.concatenate([conv_prev.astype(F32), u3], axis=1)[:, -(k_taps - 1):]

    q_slots = _q_proj(qn, w_uq_ext, cos_q, sin_q, tm=tm)
    o_slots = attend(q_slots, ckv, kpe_slot, w_kv_ext, w_q2lat, w_uv_pad, tm)

    x1, h_ffn = _out_proj(xf, c.reshape(n, dc), o_slots, w_c, w_o, P['norm_ffn'][layer][None], tm=tm)

    x2 = _ffn(h_ffn, x1, P['ffn_w_gate'][i].astype(BF16), P['ffn_w_up'][i].astype(BF16),
              P['ffn_w_down'][i].astype(BF16), tm=_row_tile(n, FFN_TILE_ROWS))
    kpe = kpe_slot[:, NOPE_DIM:NOPE_DIM + ROPE_DIM]
    return (x2.reshape(bsz, t_len, d), conv_state, ckv.reshape(bsz, t_len, kvr),
            kpe.reshape(bsz, t_len, ROPE_DIM))


def _make_attend_prompt(bsz, t_len):
    def attend(q_slots, ckv, kpe_slot, w_kv_ext, w_q2lat, w_uv_pad, tm):
        k_slots, v_slots = _kv_proj(ckv, kpe_slot, w_kv_ext, tm=tm)
        return _attn_prompt(q_slots, k_slots, v_slots, bsz=bsz, t_len=t_len, tq=min(t_len, 512))
    return attend


def _make_attend_sample(bsz, t_len, cache_ckv, cache_kpe_t, page_table, i):
    t_pad = 8

    def attend(q_slots, ckv, kpe_slot, w_kv_ext, w_q2lat, w_uv_pad, tm):
        n = bsz * t_len
        kvr = ckv.shape[1]
        q_lat = _qlat(q_slots, w_q2lat)
        q_lat = q_lat.reshape(N_HEADS, bsz, t_len, kvr)
        q_lat = jnp.pad(q_lat, ((0, 0), (0, 0), (0, t_pad - t_len), (0, 0)))
        q_lat = jnp.transpose(q_lat, (1, 0, 2, 3)).reshape(bsz, N_HEADS * t_pad, kvr)
        q_pe = q_slots.reshape(bsz, t_len, N_HEADS, HEAD_SLOT)[..., NOPE_DIM:NOPE_DIM + ROPE_DIM]
        q_pe = jnp.pad(jnp.transpose(q_pe, (0, 2, 1, 3)), ((0, 0), (0, 0), (0, t_pad - t_len), (0, 0)))
        q_pe = q_pe.reshape(bsz, N_HEADS * t_pad, ROPE_DIM)
        pad_t = ((0, 0), (0, t_pad - t_len), (0, 0))
        ckv_new = jnp.pad(ckv.reshape(bsz, t_len, kvr), pad_t)
        kpe_new = jnp.pad(kpe_slot[:, NOPE_DIM:NOPE_DIM + ROPE_DIM].reshape(bsz, t_len, ROPE_DIM), pad_t)
        o_lat = _attn_sample(page_table, q_lat, q_pe, ckv_new, kpe_new, cache_ckv, cache_kpe_t, i,
                             t_new=t_len, t_pad=t_pad, pages_per_step=64,
                             pages_per_chunk=64)
        o_lat = o_lat.reshape(bsz, N_HEADS, t_pad, kvr)[:, :, :t_len]
        o_lat = jnp.transpose(o_lat, (1, 0, 2, 3)).reshape(N_HEADS, n, kvr)
        return _olat(o_lat, w_uv_pad)
    return attend


def _odd_layer_mixer(x, pool_prev, pos0, P, layer, i):
    bsz, t_len, d = x.shape
    n_experts = P['router_w'].shape[2]
    ctx = max(POOL_WINDOWS) - 1
    prev_pad = jnp.pad(pool_prev.astype(F32), ((0, 0), (POOL_HALO - ctx, 0), (0, 0)))
    w_router_pad = jnp.pad(P['router_w'][i], ((0, 0), (0, LANES - n_experts)))
    x3, h_moe, hn, gates, ids = _pool_layer(
        prev_pad, x, P['norm_mix'][layer][None], P['pool_w'][i].astype(BF16), P['pool_scale'][i][None],
        P['norm_ffn'][layer][None], w_router_pad, tm=min(t_len, 512), pos0=pos0, n_experts=n_experts)
    pool_state = jnp.concatenate([pool_prev.astype(F32), hn], axis=1)[:, -ctx:]
    n = bsz * t_len
    return (x3.reshape(n, d), h_moe.reshape(n, d), pool_state, gates.reshape(n, LANES),
            ids.reshape(n, LANES)[:, :TOP_K])


def _moe_plan(ids, n_experts, tm):
    n = ids.shape[0]
    flat = ids.reshape(-1)
    experts = jnp.arange(n_experts, dtype=jnp.int32)
    onehot = (flat[:, None] == experts[None, :]).astype(jnp.int32)
    csum = jnp.cumsum(onehot, axis=0)
    counts = csum[-1]
    rank = jnp.sum(csum * onehot, axis=1) - 1
    padded = ((counts + tm - 1) // tm) * tm
    ends = jnp.cumsum(padded)
    starts = ends - padded
    pos = jnp.sum(starts[None, :] * onehot, axis=1) + rank
    n_tiles = (n * TOP_K + tm - 1) // tm + n_experts
    n_valid = (ends[-1] // tm).astype(jnp.int32)
    tile_start = jnp.arange(n_tiles, dtype=jnp.int32) * tm
    tile_expert = jnp.sum((ends[None, :] <= tile_start[:, None]).astype(jnp.int32), axis=1)
    tile_expert = jnp.minimum(tile_expert, n_experts - 1)
    group_end = starts + counts
    tile_onehot = (tile_expert[:, None] == experts[None, :]).astype(jnp.int32)
    tile_rows = jnp.clip(jnp.sum(group_end[None, :] * tile_onehot, axis=1) - tile_start, 0, tm)
    return pos.reshape(n, TOP_K), tile_expert, n_valid.reshape(1), tile_rows.astype(jnp.int32), n_tiles


def _moe_layer(groups, P, i, g_final, *, tm, tt):
    d = groups[0][0].shape[1]
    n_experts = P['router_w'].shape[2]
    ids = jnp.concatenate([g[3] for g in groups], axis=0)
    pos, tile_expert, n_valid, tile_rows, n_tiles = _moe_plan(ids, n_experts, tm)
    xs = jnp.zeros((n_tiles * tm, d), F32)
    pos_tiles = []
    lo = 0
    for x3, h, _, _ in groups:
        n = x3.shape[0]
        t = min(tt, n)
        pt = pos[lo:lo + n].reshape(n // t, 1, TOP_K * t)
        xs = _dispatch(pt, h, xs)
        pos_tiles.append(pt)
        lo += n
    ys = _swiglu(tile_expert, n_valid, tile_rows, xs, P['moe_w_gate'][i], P['moe_w_up'][i],
                 P['moe_w_down'][i], tm=tm, tf=512)
    gain = g_final if g_final is not None else jnp.ones((1, d), F32)
    return [_combine(pt, x3, gates, gain, ys, apply_norm=g_final is not None)
            for pt, (x3, _, gates, _) in zip(pos_tiles, groups)]


def kernel(x_prompt, x_sample, cache_ckv, cache_kpe, page_table, state_conv, state_pool, norm_mix, norm_ffn,
           norm_final, w_in, conv_w, conv_b, conv_ln_g, conv_ln_b, q_norm, w_uq, kv_norm, w_ukv, w_out,
           ffn_w_gate, ffn_w_up, ffn_w_down, pool_w, pool_scale, router_w, moe_w_gate, moe_w_up, moe_w_down):
    P = {'norm_mix': norm_mix, 'norm_ffn': norm_ffn, 'norm_final': norm_final, 'w_in': w_in,
         'conv_w': conv_w, 'conv_b': conv_b, 'conv_ln_g': conv_ln_g, 'conv_ln_b': conv_ln_b,
         'q_norm': q_norm, 'w_uq': w_uq, 'kv_norm': kv_norm, 'w_ukv': w_ukv, 'w_out': w_out,
         'ffn_w_gate': ffn_w_gate, 'ffn_w_up': ffn_w_up, 'ffn_w_down': ffn_w_down,
         'pool_w': pool_w, 'pool_scale': pool_scale, 'router_w': router_w,
         'moe_w_gate': moe_w_gate, 'moe_w_up': moe_w_up, 'moe_w_down': moe_w_down}
    depth = norm_mix.shape[0]
    d = x_prompt.shape[2]
    bp, tp, _ = x_prompt.shape
    bs, ts, _ = x_sample.shape
    past = page_table.shape[1] * cache_ckv.shape[2]
    k_taps = conv_w.shape[1]
    ctx = max(POOL_WINDOWS) - 1
    pos_p = jnp.arange(tp, dtype=jnp.int32)
    pos_s = past + jnp.arange(ts, dtype=jnp.int32)
    cache_kpe_t = jnp.swapaxes(cache_kpe, 2, 3)

    xp, xs = x_prompt, x_sample
    outs_p = {'ckv': [], 'kpe': [], 'conv': [], 'pool': []}
    outs_s = {'ckv': [], 'kpe': [], 'conv': [], 'pool': []}
    for layer in range(depth):
        i = layer // 2
        last = layer == depth - 1
        if layer % 2 == 0:
            conv0 = jnp.zeros((bp, k_taps - 1, conv_w.shape[2]), F32)
            xp, cst, ckv, kpe = _even_layer(xp, conv0, pos_p, _make_attend_prompt(bp, tp), P, layer, i)
            outs_p['conv'].append(cst); outs_p['ckv'].append(ckv); outs_p['kpe'].append(kpe)
            xs, cst, ckv, kpe = _even_layer(
                xs, state_conv[i], pos_s, _make_attend_sample(bs, ts, cache_ckv, cache_kpe_t, page_table, i),
                P, layer, i)
            outs_s['conv'].append(cst); outs_s['ckv'].append(ckv); outs_s['kpe'].append(kpe)
            if last:
                xp = _final(xp.reshape(bp * tp, d), norm_final[None], tm=_row_tile(bp * tp, 512)).reshape(xp.shape)
                xs = _final(xs.reshape(bs * ts, d), norm_final[None], tm=_row_tile(bs * ts, 512)).reshape(xs.shape)
        else:
            pool0 = jnp.zeros((bp, ctx, d), F32)
            x3p, hp, pst_p, gp, ip = _odd_layer_mixer(xp, pool0, 0, P, layer, i)
            x3s, hs, pst_s, gs, is_ = _odd_layer_mixer(xs, state_pool[i], past, P, layer, i)
            outs_p['pool'].append(pst_p); outs_s['pool'].append(pst_s)
            yp, ysm = _moe_layer([(x3p, hp, gp, ip), (x3s, hs, gs, is_)], P, i,
                                 norm_final[None] if last else None, tm=1024, tt=512)
            xp, xs = yp.reshape(xp.shape), ysm.reshape(xs.shape)

    def stack(lst, shape_if_empty):
        return jnp.stack(lst) if lst else jnp.zeros(shape_if_empty, F32)

    return (xp, xs,
            jnp.stack(outs_p['ckv']), jnp.stack(outs_p['kpe']), jnp.stack(outs_p['conv']),
            stack(outs_p['pool'], (0, bp, ctx, d)),
            jnp.stack(outs_s['ckv']), jnp.stack(outs_s['kpe']), jnp.stack(outs_s['conv']),
            stack(outs_s['pool'], (0, bs, ctx, d)))
```

```python
import functools

import jax
import jax.numpy as jnp
from jax import lax
from jax.experimental import pallas as pl
from jax.experimental.pallas import tpu as pltpu

F32 = jnp.float32
BF16 = jnp.bfloat16

N_HEADS = 8
NOPE_DIM = 64
ROPE_DIM = 32
V_DIM = 64
ROPE_THETA = 10000.0
ATTN_SCALE = (NOPE_DIM + ROPE_DIM) ** -0.5
LOG2E = 1.4426950408889634
POOL_WINDOWS = (2, 4, 8, 16)
TOP_K = 2
EPS = 1e-6

HEAD_SLOT = 128
LANES = 128
SUBLANES = 8
NEG_BIG = -1e30
VMEM_LIMIT = 56 * 1024 * 1024

PROJ_TILE_ROWS = 1024
FFN_TILE_ROWS = 512
CONV_HALO = 32
POOL_HALO = 16


def _cparams(*semantics):
    return pltpu.CompilerParams(dimension_semantics=semantics, vmem_limit_bytes=VMEM_LIMIT)


def _rms(x, g):
    return x * lax.rsqrt(jnp.mean(x * x, axis=-1, keepdims=True) + EPS) * g


def _row_tile(n, want):
    if n <= want:
        return n
    t = want
    while t >= 8:
        if n % t == 0 and t % 8 == 0:
            return t
        t -= 8
    return n


def _in_proj_kernel(x_ref, g_ref, w_ref, qg_ref, kvg_ref, cos_ref, sin_ref,
                    u_ref, qn_ref, ckv_ref, kpe_ref, *, dc, qr, kvr):
    h = _rms(x_ref[...], g_ref[...])
    proj = jnp.dot(h.astype(BF16), w_ref[...], preferred_element_type=F32)
    a = proj[:, :dc]
    gate = proj[:, dc:2 * dc]
    u_ref[...] = a * jax.nn.sigmoid(gate)
    o = 2 * dc
    qn_ref[...] = _rms(proj[:, o:o + qr], qg_ref[...]).astype(BF16)
    o += qr
    ckv_ref[...] = _rms(proj[:, o:o + kvr], kvg_ref[...])
    o += kvr
    kpe_ref[...] = (proj[:, o:o + HEAD_SLOT] * cos_ref[...]
                    + proj[:, o + HEAD_SLOT:o + 2 * HEAD_SLOT] * sin_ref[...])


def _in_proj(x, g, w_ext, qg, kvg, cos_k, sin_k, *, dc, qr, kvr, tm):
    n, d = x.shape
    n_tab = cos_k.shape[0] // tm
    row = lambda i: (i, 0)
    full = lambda i: (0, 0)
    tab = lambda i: (i % n_tab, 0)
    return pl.pallas_call(
        functools.partial(_in_proj_kernel, dc=dc, qr=qr, kvr=kvr),
        grid=(n // tm,),
        in_specs=[
            pl.BlockSpec((tm, d), row),
            pl.BlockSpec((1, d), full),
            pl.BlockSpec(w_ext.shape, full),
            pl.BlockSpec((1, qr), full),
            pl.BlockSpec((1, kvr), full),
            pl.BlockSpec((tm, HEAD_SLOT), tab),
            pl.BlockSpec((tm, HEAD_SLOT), tab),
        ],
        out_specs=[
            pl.BlockSpec((tm, dc), row),
            pl.BlockSpec((tm, qr), row),
            pl.BlockSpec((tm, kvr), row),
            pl.BlockSpec((tm, HEAD_SLOT), row),
        ],
        out_shape=[
            jax.ShapeDtypeStruct((n, dc), F32),
            jax.ShapeDtypeStruct((n, qr), BF16),
            jax.ShapeDtypeStruct((n, kvr), F32),
            jax.ShapeDtypeStruct((n, HEAD_SLOT), F32),
        ],
        compiler_params=_cparams("parallel"),
        name="in_proj",
    )(x, g, w_ext, qg, kvg, cos_k, sin_k)


def _conv_kernel(*refs, tc, k_taps, rows_per_chunk, has_halo, seqs):
    if has_halo:
        prev_ref, halo_ref, u_ref, w_ref, b_ref, lg_ref, lb_ref, c_ref, ext_ref, sh_ref = refs
    else:
        prev_ref, u_ref, w_ref, b_ref, lg_ref, lb_ref, c_ref, ext_ref, sh_ref = refs
    n_rows = CONV_HALO + tc
    first = CONV_HALO - (k_taps - 1)

    def window(start, n):
        q, s = divmod(start, SUBLANES)
        if s == 0:
            return ext_ref[start:start + n, :]
        return sh_ref[s - 1, q * SUBLANES:q * SUBLANES + n, :]

    for sq in range(seqs):
        if has_halo:
            t = pl.program_id(1)

            @pl.when(t == 0)
            def _():
                ext_ref[0:CONV_HALO, :] = prev_ref[sq]

            @pl.when(t > 0)
            def _():
                ext_ref[0:CONV_HALO, :] = halo_ref[sq]
        else:
            ext_ref[0:CONV_HALO, :] = prev_ref[sq]
        ext_ref[CONV_HALO:n_rows, :] = u_ref[sq]

        for s in range(1, SUBLANES):
            sh_ref[s - 1, 0:n_rows - SUBLANES, :] = ext_ref[s:s + n_rows - SUBLANES, :]

        for c0 in range(0, tc, rows_per_chunk):
            rc = min(rows_per_chunk, tc - c0)
            acc = w_ref[0:1, :] * window(first + c0, rc)
            for k in range(1, k_taps):
                acc = acc + w_ref[k:k + 1, :] * window(first + c0 + k, rc)
            acc = acc + b_ref[...]
            mu = jnp.mean(acc, axis=-1, keepdims=True)
            xc = acc - mu
            y = xc * lax.rsqrt(jnp.mean(xc * xc, axis=-1, keepdims=True) + EPS)
            y = y * lg_ref[...] + lb_ref[...]
            c_ref[sq, c0:c0 + rc, :] = (y * jax.nn.sigmoid(y)).astype(BF16)


def _seqs_per_step(bsz, n_t, want=8):
    if n_t > 1:
        return 1
    g = min(want, bsz)
    while bsz % g:
        g -= 1
    return g


def _conv_branch(prev_pad, u, w, b, lg, lb, *, tc):
    bsz, t_len, c = u.shape
    k_taps = w.shape[0]
    n_t = t_len // tc
    has_halo = n_t > 1
    hb = tc // CONV_HALO if has_halo else 1
    seqs = _seqs_per_step(bsz, n_t)
    in_specs = [pl.BlockSpec((seqs, CONV_HALO, c), lambda bi, ti: (bi, 0, 0))]
    args = [prev_pad]
    if has_halo:
        in_specs.append(pl.BlockSpec((seqs, CONV_HALO, c),
                                     lambda bi, ti: (bi, jnp.maximum(ti * hb - 1, 0), 0)))
        args.append(u)
    in_specs += [
        pl.BlockSpec((seqs, tc, c), lambda bi, ti: (bi, ti, 0)),
        pl.BlockSpec((k_taps, c), lambda bi, ti: (0, 0)),
        pl.BlockSpec((1, c), lambda bi, ti: (0, 0)),
        pl.BlockSpec((1, c), lambda bi, ti: (0, 0)),
        pl.BlockSpec((1, c), lambda bi, ti: (0, 0)),
    ]
    args += [u, w, b, lg, lb]
    return pl.pallas_call(
        functools.partial(_conv_kernel, tc=tc, k_taps=k_taps, rows_per_chunk=64, has_halo=has_halo, seqs=seqs),
        grid=(bsz // seqs, n_t),
        in_specs=in_specs,
        out_specs=pl.BlockSpec((seqs, tc, c), lambda bi, ti: (bi, ti, 0)),
        out_shape=jax.ShapeDtypeStruct((bsz, t_len, c), BF16),
        scratch_shapes=[pltpu.VMEM((CONV_HALO + tc, c), F32),
                        pltpu.VMEM((SUBLANES - 1, CONV_HALO + tc, c), F32)],
        compiler_params=_cparams("parallel", "parallel"),
        name="conv_branch",
    )(*args)


def _q_proj_kernel(qn_ref, w_ref, cos_ref, sin_ref, q_ref, *, n_heads):
    proj = jnp.dot(qn_ref[...], w_ref[...], preferred_element_type=F32)
    half = n_heads * HEAD_SLOT
    cos = cos_ref[...]
    sin = sin_ref[...]
    for h in range(n_heads):
        a = proj[:, h * HEAD_SLOT:(h + 1) * HEAD_SLOT]
        b = proj[:, half + h * HEAD_SLOT:half + (h + 1) * HEAD_SLOT]
        q_ref[:, h * HEAD_SLOT:(h + 1) * HEAD_SLOT] = (a * cos + b * sin).astype(BF16)


def _q_proj(qn, w_ext, cos_q, sin_q, *, tm):
    n, qr = qn.shape
    n_tab = cos_q.shape[0] // tm
    width = N_HEADS * HEAD_SLOT
    return pl.pallas_call(
        functools.partial(_q_proj_kernel, n_heads=N_HEADS),
        grid=(n // tm,),
        in_specs=[
            pl.BlockSpec((tm, qr), lambda i: (i, 0)),
            pl.BlockSpec(w_ext.shape, lambda i: (0, 0)),
            pl.BlockSpec((tm, HEAD_SLOT), lambda i: (i % n_tab, 0)),
            pl.BlockSpec((tm, HEAD_SLOT), lambda i: (i % n_tab, 0)),
        ],
        out_specs=pl.BlockSpec((tm, width), lambda i: (i, 0)),
        out_shape=jax.ShapeDtypeStruct((n, width), BF16),
        compiler_params=_cparams("parallel"),
        name="q_proj",
    )(qn, w_ext, cos_q, sin_q)


def _kv_proj_kernel(ckv_ref, kpe_ref, w_ref, k_ref, v_ref, *, n_heads):
    proj = jnp.dot(ckv_ref[...].astype(BF16), w_ref[...], preferred_element_type=F32)
    half = n_heads * HEAD_SLOT
    kpe = kpe_ref[...]
    for h in range(n_heads):
        k_ref[:, h * HEAD_SLOT:(h + 1) * HEAD_SLOT] = (
            proj[:, h * HEAD_SLOT:(h + 1) * HEAD_SLOT] + kpe).astype(BF16)
    v = proj[:, half:]
    lane = lax.broadcasted_iota(jnp.int32, v.shape, 1)
    v_ref[...] = jnp.where(lane % HEAD_SLOT == V_DIM, 1.0, v).astype(BF16)


def _kv_proj(ckv, kpe_slot, w_ext, *, tm):
    n, kvr = ckv.shape
    width = N_HEADS * HEAD_SLOT
    return pl.pallas_call(
        functools.partial(_kv_proj_kernel, n_heads=N_HEADS),
        grid=(n // tm,),
        in_specs=[
            pl.BlockSpec((tm, kvr), lambda i: (i, 0)),
            pl.BlockSpec((tm, HEAD_SLOT), lambda i: (i, 0)),
            pl.BlockSpec(w_ext.shape, lambda i: (0, 0)),
        ],
        out_specs=[pl.BlockSpec((tm, width), lambda i: (i, 0)),
                   pl.BlockSpec((tm, width), lambda i: (i, 0))],
        out_shape=[jax.ShapeDtypeStruct((n, width), BF16),
                   jax.ShapeDtypeStruct((n, width), BF16)],
        compiler_params=_cparams("parallel"),
        name="kv_proj",
    )(ckv, kpe_slot, w_ext)


def _attn_prompt_kernel(q_ref, k_ref, v_ref, o_ref, *, tq, n_heads):
    qi = pl.program_id(1)
    row = lax.broadcasted_iota(jnp.int32, (tq, tq), 0)
    col = lax.broadcasted_iota(jnp.int32, (tq, tq), 1)
    causal = col <= row

    head_lanes = [slice(h * HEAD_SLOT, (h + 1) * HEAD_SLOT) for h in range(n_heads)]

    def step(kt, carry, masked):
        ms, accs = carry
        start = pl.multiple_of(kt * tq, tq)
        new_ms, new_accs = [], []
        for h, lanes in enumerate(head_lanes):
            k = k_ref[pl.ds(start, tq), lanes]
            v = v_ref[pl.ds(start, tq), lanes]
            s = lax.dot_general(q_ref[:, lanes], k, (((1,), (1,)), ((), ())), preferred_element_type=F32)
            if masked:
                s = jnp.where(causal, s, NEG_BIG)
            m_new = jnp.maximum(ms[h], jnp.max(s, axis=-1, keepdims=True))
            p = jnp.exp2(s - m_new)
            new_accs.append(jnp.exp2(ms[h] - m_new) * accs[h]
                            + jnp.dot(p.astype(BF16), v, preferred_element_type=F32))
            new_ms.append(m_new)
        return tuple(new_ms), tuple(new_accs)

    init = (tuple(jnp.full((tq, 1), NEG_BIG, F32) for _ in head_lanes),
            tuple(jnp.zeros((tq, HEAD_SLOT), F32) for _ in head_lanes))
    carry = lax.fori_loop(0, qi, functools.partial(step, masked=False), init)
    _, accs = step(qi, carry, True)
    for lanes, acc in zip(head_lanes, accs):
        o_ref[:, lanes] = (acc / acc[:, V_DIM:V_DIM + 1]).astype(BF16)


def _attn_prompt(q, k, v, *, bsz, t_len, tq):
    width = q.shape[1]
    nq = t_len // tq
    return pl.pallas_call(
        functools.partial(_attn_prompt_kernel, tq=tq, n_heads=N_HEADS),
        grid=(bsz, nq),
        in_specs=[
            pl.BlockSpec((tq, width), lambda b, i: (b * nq + i, 0)),
            pl.BlockSpec((t_len, width), lambda b, i: (b, 0)),
            pl.BlockSpec((t_len, width), lambda b, i: (b, 0)),
        ],
        out_specs=pl.BlockSpec((tq, width), lambda b, i: (b * nq + i, 0)),
        out_shape=jax.ShapeDtypeStruct(q.shape, BF16),
        compiler_params=_cparams("parallel", "arbitrary"),
        name="attn_prompt",
    )(q, k, v)


def _qlat_kernel(q_ref, w_ref, o_ref, *, n_heads):
    for h in range(n_heads):
        o_ref[h] = jnp.dot(q_ref[:, h * HEAD_SLOT:(h + 1) * HEAD_SLOT], w_ref[h],
                           preferred_element_type=F32).astype(BF16)


def _qlat(q_slots, w_q2lat):
    n = q_slots.shape[0]
    n_heads, _, kvr = w_q2lat.shape
    return pl.pallas_call(
        functools.partial(_qlat_kernel, n_heads=n_heads),
        out_shape=jax.ShapeDtypeStruct((n_heads, n, kvr), BF16),
        compiler_params=pltpu.CompilerParams(vmem_limit_bytes=VMEM_LIMIT),
        name="q_latent",
    )(q_slots, w_q2lat)


def _attn_sample_kernel(pt_ref, ql_ref, qp_ref, cn_ref, kn_ref, ckv_hbm, kpe_hbm, o_ref,
                        cbuf_ref, pbuf_ref, kbf_ref, pbf_ref, m_ref, l_ref, acc_ref, sem,
                        *, layer, pages_per_step, pages_per_chunk, page, t_new, t_pad):
    g = pages_per_step
    bsz, n_pages = pt_ref.shape
    steps_per_seq = n_pages // g
    total = bsz * steps_per_seq

    def page_copies(i, slot):
        b = i // steps_per_seq
        first = (i % steps_per_seq) * g
        copies = []
        for j in range(g):
            pid = pt_ref[b, first + j]
            copies.append(pltpu.make_async_copy(
                ckv_hbm.at[layer, pid], cbuf_ref.at[slot, pl.ds(j * page, page)], sem.at[slot]))
            copies.append(pltpu.make_async_copy(
                kpe_hbm.at[layer, pid], pbuf_ref.at[slot, j], sem.at[slot]))
        return copies

    for cp in page_copies(0, 0):
        cp.start()

    def step(i, carry):
        slot = i % 2
        b = i // steps_per_seq
        s_idx = i % steps_per_seq

        @pl.when(i + 1 < total)
        def _():
            for cp in page_copies(i + 1, 1 - slot):
                cp.start()

        pltpu.make_async_copy(cbuf_ref.at[slot], cbuf_ref.at[slot], sem.at[slot]).wait()
        pltpu.make_async_copy(pbuf_ref.at[slot], pbuf_ref.at[slot], sem.at[slot]).wait()
        _attn_sample_step(b, s_idx, slot, steps_per_seq, ql_ref, qp_ref, cn_ref, kn_ref, o_ref, cbuf_ref,
                          pbuf_ref, kbf_ref, pbf_ref, m_ref, l_ref, acc_ref, g=g,
                          pages_per_chunk=pages_per_chunk, page=page, t_new=t_new, t_pad=t_pad)
        return carry

    lax.fori_loop(0, total, step, 0)


def _attn_sample_step(b, s_idx, slot, steps_per_seq, ql_ref, qp_ref, cn_ref, kn_ref, o_ref, cbuf_ref, pbuf_ref,
                      kbf_ref, pbf_ref, m_ref, l_ref, acc_ref, *, g, pages_per_chunk, page, t_new, t_pad):
    ql = ql_ref[b]
    qp = qp_ref[b]
    rows = ql.shape[0]
    contract_last = (((1,), (1,)), ((), ()))

    @pl.when(s_idx == 0)
    def _():
        cn = cn_ref[b].astype(BF16)
        kn = kn_ref[b].astype(BF16)
        s = (lax.dot_general(ql, cn, contract_last, preferred_element_type=F32)
             + lax.dot_general(qp, kn, contract_last, preferred_element_type=F32))
        t_of_row = lax.broadcasted_iota(jnp.int32, (rows, t_pad), 0) % t_pad
        key = lax.broadcasted_iota(jnp.int32, (rows, t_pad), 1)
        ok = (key <= t_of_row) & (key < t_new)
        s = jnp.where(ok, s, NEG_BIG)
        m = jnp.max(s, axis=-1, keepdims=True)
        p = jnp.where(ok, jnp.exp2(s - m), 0.0)
        m_ref[...] = m
        l_ref[...] = jnp.sum(p, axis=-1, keepdims=True)
        acc_ref[...] = jnp.dot(p.astype(BF16), cn, preferred_element_type=F32)

    kbf_ref[...] = cbuf_ref[slot].astype(BF16)
    for j in range(g):
        pbf_ref[:, j * page:(j + 1) * page] = pbuf_ref[slot, j].astype(BF16)

    ck = pages_per_chunk * page
    parts = []
    for c in range(g // pages_per_chunk):
        kb = kbf_ref[c * ck:(c + 1) * ck, :]
        s = (lax.dot_general(ql, kb, contract_last, preferred_element_type=F32)
             + jnp.dot(qp, pbf_ref[:, c * ck:(c + 1) * ck], preferred_element_type=F32))
        mc = jnp.max(s, axis=-1, keepdims=True)
        p = jnp.exp2(s - mc)
        parts.append((mc, jnp.sum(p, axis=-1, keepdims=True),
                      jnp.dot(p.astype(BF16), kb, preferred_element_type=F32)))
    m_old = m_ref[...]
    m_new = m_old
    for mc, _, _ in parts:
        m_new = jnp.maximum(m_new, mc)
    alpha = jnp.exp2(m_old - m_new)
    l = alpha * l_ref[...]
    acc = alpha * acc_ref[...]
    for mc, lc, ac in parts:
        w = jnp.exp2(mc - m_new)
        l = l + w * lc
        acc = acc + w * ac
    m_ref[...] = m_new
    l_ref[...] = l
    acc_ref[...] = acc

    @pl.when(s_idx == steps_per_seq - 1)
    def _():
        o_ref[b] = acc_ref[...] / l_ref[...]


def _attn_sample(page_table, q_lat, q_pe, ckv_new, kpe_new, cache_ckv, cache_kpe_t, layer, *,
                 t_new, t_pad, pages_per_step, pages_per_chunk):
    bsz, rows, kvr = q_lat.shape
    rope = q_pe.shape[2]
    page = cache_ckv.shape[2]
    g = pages_per_step
    assert page_table.shape[1] % g == 0 and g % pages_per_chunk == 0
    vmem = pl.BlockSpec(memory_space=pltpu.VMEM)
    hbm = pl.BlockSpec(memory_space=pl.ANY)
    return pl.pallas_call(
        functools.partial(_attn_sample_kernel, layer=layer, pages_per_step=g, pages_per_chunk=pages_per_chunk,
                          page=page, t_new=t_new, t_pad=t_pad),
        in_specs=[pl.BlockSpec(memory_space=pltpu.SMEM), vmem, vmem, vmem, vmem, hbm, hbm],
        out_specs=vmem,
        out_shape=jax.ShapeDtypeStruct((bsz, rows, kvr), F32),
        scratch_shapes=[
            pltpu.VMEM((2, g * page, kvr), F32),
            pltpu.VMEM((2, g, rope, page), F32),
            pltpu.VMEM((g * page, kvr), BF16),
            pltpu.VMEM((rope, g * page), BF16),
            pltpu.VMEM((rows, 1), F32),
            pltpu.VMEM((rows, 1), F32),
            pltpu.VMEM((rows, kvr), F32),
            pltpu.SemaphoreType.DMA((2,)),
        ],
        compiler_params=pltpu.CompilerParams(vmem_limit_bytes=VMEM_LIMIT),
        name="attn_sample",
    )(page_table, q_lat, q_pe, ckv_new, kpe_new, cache_ckv, cache_kpe_t)


def _olat_kernel(o_ref, w_ref, out_ref, *, n_heads):
    for h in range(n_heads):
        out_ref[:, h * HEAD_SLOT:(h + 1) * HEAD_SLOT] = jnp.dot(
            o_ref[h].astype(BF16), w_ref[h], preferred_element_type=F32).astype(BF16)


def _olat(o_lat_heads, w_uv_pad):
    n_heads, n, _ = o_lat_heads.shape
    return pl.pallas_call(
        functools.partial(_olat_kernel, n_heads=n_heads),
        out_shape=jax.ShapeDtypeStruct((n, n_heads * HEAD_SLOT), BF16),
        compiler_params=pltpu.CompilerParams(vmem_limit_bytes=VMEM_LIMIT),
        name="o_latent",
    )(o_lat_heads, w_uv_pad)


def _out_proj_kernel(x_ref, c_ref, o_ref, wc_ref, wo_ref, g_ref, x1_ref, h_ref):
    mix = (jnp.dot(c_ref[...], wc_ref[...], preferred_element_type=F32)
           + jnp.dot(o_ref[...], wo_ref[...], preferred_element_type=F32))
    x1 = x_ref[...] + mix
    x1_ref[...] = x1
    h_ref[...] = _rms(x1, g_ref[...]).astype(BF16)


def _out_proj(x, c, o_slots, w_c, w_o, g, *, tm):
    n, d = x.shape
    row = lambda i: (i, 0)
    full = lambda i: (0, 0)
    return pl.pallas_call(
        _out_proj_kernel,
        grid=(n // tm,),
        in_specs=[
            pl.BlockSpec((tm, d), row),
            pl.BlockSpec((tm, c.shape[1]), row),
            pl.BlockSpec((tm, o_slots.shape[1]), row),
            pl.BlockSpec(w_c.shape, full),
            pl.BlockSpec(w_o.shape, full),
            pl.BlockSpec((1, d), full),
        ],
        out_specs=[pl.BlockSpec((tm, d), row), pl.BlockSpec((tm, d), row)],
        out_shape=[jax.ShapeDtypeStruct((n, d), F32), jax.ShapeDtypeStruct((n, d), BF16)],
        compiler_params=_cparams("parallel"),
        name="out_proj",
    )(x, c, o_slots, w_c, w_o, g)


def _swiglu_rows(x, wg, wu, wd):
    a = jnp.dot(x, wg, preferred_element_type=F32)
    b = jnp.dot(x, wu, preferred_element_type=F32)
    mid = (a * jax.nn.sigmoid(a) * b).astype(BF16)
    return jnp.dot(mid, wd, preferred_element_type=F32)


def _ffn_kernel(x_ref, res_ref, wg_hbm, wu_hbm, wd_hbm, o_ref, wg_ref, wu_ref, wd_ref, sem, *, row_splits):
    @pl.when(pl.program_id(0) == 0)
    def _():
        copies = [pltpu.make_async_copy(src, dst, sem.at[j]) for j, (src, dst) in enumerate(
            ((wg_hbm, wg_ref), (wu_hbm, wu_ref), (wd_hbm, wd_ref)))]
        for cp in copies:
            cp.start()
        for cp in copies:
            cp.wait()

    hm = o_ref.shape[0] // row_splits
    for r in range(row_splits):
        rows = slice(r * hm, (r + 1) * hm)
        o_ref[rows, :] = res_ref[rows, :] + _swiglu_rows(x_ref[rows, :], wg_ref[...], wu_ref[...], wd_ref[...])


def _ffn(x, res, wg, wu, wd, *, tm):
    n, d = x.shape
    d_ff = wg.shape[1]
    row = lambda i: (i, 0)
    hbm = pl.BlockSpec(memory_space=pl.ANY)
    return pl.pallas_call(
        functools.partial(_ffn_kernel, row_splits=1),
        grid=(n // tm,),
        in_specs=[pl.BlockSpec((tm, d), row), pl.BlockSpec((tm, d), row), hbm, hbm, hbm],
        out_specs=pl.BlockSpec((tm, d), row),
        out_shape=jax.ShapeDtypeStruct((n, d), F32),
        scratch_shapes=[pltpu.VMEM((d, d_ff), BF16), pltpu.VMEM((d, d_ff), BF16), pltpu.VMEM((d_ff, d), BF16),
                        pltpu.SemaphoreType.DMA((3,))],
        compiler_params=_cparams("arbitrary"),
        name="ffn_dense",
    )(x, res, wg, wu, wd)


def _swiglu_kernel(te_ref, nv_ref, tr_ref, x_ref, wg_ref, wu_ref, wd_ref, o_ref, xb_ref, *, row_groups,
                   full_splits):
    m = pl.program_id(0)
    f = pl.program_id(1)
    tm = o_ref.shape[0]
    hm = tm // row_groups

    @pl.when(f == 0)
    def _():
        o_ref[...] = jnp.zeros_like(o_ref)

    valid = m < nv_ref[0]
    groups_needed = (tr_ref[jnp.minimum(m, nv_ref[0] - 1)] + hm - 1) // hm

    @pl.when(valid & (f == 0))
    def _():
        xb_ref[...] = x_ref[...].astype(BF16)

    @pl.when(valid & (groups_needed == row_groups))
    def _():
        wg = wg_ref[...].astype(BF16)
        wu = wu_ref[...].astype(BF16)
        wd = wd_ref[...].astype(BF16)
        fm = tm // full_splits
        for r in range(full_splits):
            rows = slice(r * fm, (r + 1) * fm)
            o_ref[rows, :] += _swiglu_rows(xb_ref[rows, :], wg, wu, wd)

    @pl.when(valid & (groups_needed < row_groups))
    def _():
        wg = wg_ref[...].astype(BF16)
        wu = wu_ref[...].astype(BF16)
        wd = wd_ref[...].astype(BF16)

        def group(r, carry):
            rows = pl.ds(pl.multiple_of(r * hm, hm), hm)
            o_ref[rows, :] += _swiglu_rows(xb_ref[rows, :], wg, wu, wd)
            return carry

        lax.fori_loop(0, groups_needed, group, 0)


def _swiglu(tile_expert, n_valid, tile_rows, x, wg, wu, wd, *, tm, tf):
    p, d = x.shape
    d_ff = wg.shape[2]
    n_f = d_ff // tf
    n_tiles = p // tm

    def mm(m, nv):
        return jnp.minimum(m, nv[0] - 1)

    def ff(m, f, nv):
        return jnp.where(m < nv[0], f, n_f - 1)

    return pl.pallas_call(
        functools.partial(_swiglu_kernel, row_groups=2 if tm % 32 == 0 else 1, full_splits=1),
        grid_spec=pltpu.PrefetchScalarGridSpec(
            num_scalar_prefetch=3,
            grid=(n_tiles, n_f),
            in_specs=[
                pl.BlockSpec((tm, d), lambda m, f, te, nv, tr: (mm(m, nv), 0)),
                pl.BlockSpec((None, d, tf), lambda m, f, te, nv, tr: (te[mm(m, nv)], 0, ff(m, f, nv))),
                pl.BlockSpec((None, d, tf), lambda m, f, te, nv, tr: (te[mm(m, nv)], 0, ff(m, f, nv))),
                pl.BlockSpec((None, tf, d), lambda m, f, te, nv, tr: (te[mm(m, nv)], ff(m, f, nv), 0)),
            ],
            out_specs=pl.BlockSpec((tm, d), lambda m, f, te, nv, tr: (m, 0)),
            scratch_shapes=[pltpu.VMEM((tm, d), BF16)],
        ),
        out_shape=jax.ShapeDtypeStruct((p, d), F32),
        compiler_params=_cparams("arbitrary", "arbitrary"),
        name="swiglu_experts",
    )(tile_expert, n_valid, tile_rows, x, wg, wu, wd)


def _row_copy(src, src_row, dst, dst_row, sem):
    return pltpu.make_async_copy(src.at[pl.ds(src_row, 1)], dst.at[pl.ds(dst_row, 1)], sem)


def _dispatch_kernel(pos_ref, h_ref, xs_in_ref, xs_ref, sem, *, tt):
    del xs_in_ref

    def body(r, carry):
        for k in range(TOP_K):
            _row_copy(h_ref, r, xs_ref, pos_ref[0, TOP_K * r + k], sem).start(priority=k % 2)
        return carry

    lax.fori_loop(0, tt, body, 0, unroll=8)
    for k in range(TOP_K):
        pltpu.make_async_copy(h_ref, xs_ref.at[pl.ds(0, tt)], sem).wait()


def _dispatch(pos_tiles, h, xs):
    n, d = h.shape
    n_tiles = pos_tiles.shape[0]
    tt = n // n_tiles
    return pl.pallas_call(
        functools.partial(_dispatch_kernel, tt=tt),
        grid=(n_tiles,),
        in_specs=[
            pl.BlockSpec((None, 1, TOP_K * tt), lambda i: (i, 0, 0), memory_space=pltpu.SMEM),
            pl.BlockSpec((tt, d), lambda i: (i, 0)),
            pl.BlockSpec(memory_space=pl.ANY),
        ],
        out_specs=pl.BlockSpec(memory_space=pl.ANY),
        out_shape=jax.ShapeDtypeStruct(xs.shape, xs.dtype),
        scratch_shapes=[pltpu.SemaphoreType.DMA],
        input_output_aliases={2: 0},
        compiler_params=_cparams("arbitrary"),
        name="moe_dispatch",
    )(pos_tiles, h, xs)


def _pool_kernel(*refs, tm, pos0, windows, has_halo, n_experts, seqs):
    if has_halo:
        (prev_ref, halo_ref, x_ref, g1_ref, wp_ref, sc_ref, g2_ref, wr_ref,
         x3_ref, h_ref, hn_ref, gate_ref, idx_ref, ext_ref, lv_ref) = refs
    else:
        (prev_ref, x_ref, g1_ref, wp_ref, sc_ref, g2_ref, wr_ref,
         x3_ref, h_ref, hn_ref, gate_ref, idx_ref, ext_ref, lv_ref) = refs
    for sq in range(seqs):
        _pool_sequence(sq, prev_ref, halo_ref if has_halo else None, x_ref, g1_ref, wp_ref, sc_ref, g2_ref,
                       wr_ref, x3_ref, h_ref, hn_ref, gate_ref, idx_ref, ext_ref, lv_ref,
                       tm=tm, pos0=pos0, windows=windows, n_experts=n_experts)


def _pool_sequence(sq, prev_ref, halo_ref, x_ref, g1_ref, wp_ref, sc_ref, g2_ref, wr_ref,
                   x3_ref, h_ref, hn_ref, gate_ref, idx_ref, ext_ref, lv_ref, *, tm, pos0, windows, n_experts):
    t = pl.program_id(1)
    x = x_ref[sq]
    hn = _rms(x, g1_ref[...])
    tail = hn_ref.shape[1]
    hn_ref[sq] = hn[tm - tail:, :]
    top = SUBLANES
    base = top + POOL_HALO
    end = base + tm
    d = x.shape[1]
    ext_ref[0:top, :] = jnp.zeros((top, d), F32)
    if halo_ref is not None:
        @pl.when(t == 0)
        def _():
            ext_ref[top:base, :] = prev_ref[sq]

        @pl.when(t > 0)
        def _():
            ext_ref[top:base, :] = _rms(halo_ref[sq], g1_ref[...])
    else:
        ext_ref[top:base, :] = prev_ref[sq]
    ext_ref[base:end, :] = hn

    gc = d // len(windows)

    def level_rows(k, lo_row, hi_row, lanes):
        if k == 0:
            return ext_ref[lo_row:hi_row, lanes]
        return lv_ref[k - 1, lo_row:hi_row, lanes]

    for k in range(1, len(windows)):
        lanes = slice((k - 1) * gc, d)
        shift = 1 << (k - 1)
        lv_ref[k - 1, 0:top, lanes] = jnp.zeros((top, d - (k - 1) * gc), F32)
        lv_ref[k - 1, top:end, lanes] = (level_rows(k - 1, top, end, lanes)
                                         + level_rows(k - 1, top - shift, end - shift, lanes))

    pos = pos0 + t * tm + lax.broadcasted_iota(jnp.int32, (tm, 1), 0)
    for gi, w in enumerate(windows):
        lanes = slice(gi * gc, (gi + 1) * gc)
        if gi + 1 < len(windows):
            s = level_rows(gi + 1, base, end, lanes)
        else:
            s = level_rows(gi, base, end, lanes) + level_rows(gi, base - w // 2, end - w // 2, lanes)
        cnt = jnp.minimum(pos + 1, w).astype(F32)
        diff = s / cnt - hn[:, lanes]
        y = jnp.dot(diff.astype(BF16), wp_ref[gi], preferred_element_type=F32)
        x3_ref[sq, :, lanes] = x[:, lanes] + y * sc_ref[:, lanes]

    x3 = x3_ref[sq]
    h2 = _rms(x3, g2_ref[...])
    h_ref[sq] = h2
    h_hi = h2.astype(BF16)
    h_lo = (h2 - h_hi.astype(F32)).astype(BF16)
    wr = wr_ref[...]
    w_hi = wr.astype(BF16)
    w_lo = (wr - w_hi.astype(F32)).astype(BF16)
    logits = (jnp.dot(h_hi, w_hi, preferred_element_type=F32)
              + (jnp.dot(h_lo, w_hi, preferred_element_type=F32)
                 + jnp.dot(h_hi, w_lo, preferred_element_type=F32)))
    lane = lax.broadcasted_iota(jnp.int32, logits.shape, 1)
    logits = jnp.where(lane < n_experts, logits, -jnp.inf)
    m1 = jnp.max(logits, axis=-1, keepdims=True)
    i1 = jnp.min(jnp.where(logits == m1, lane, LANES), axis=-1, keepdims=True)
    rest = jnp.where(lane == i1, -jnp.inf, logits)
    m2 = jnp.max(rest, axis=-1, keepdims=True)
    i2 = jnp.min(jnp.where(rest == m2, lane, LANES), axis=-1, keepdims=True)
    e = jnp.exp(m2 - m1)
    g1 = 1.0 / (1.0 + e)
    g2 = e / (1.0 + e)
    gate_ref[sq] = jnp.where(lane == 0, g1, jnp.where(lane == 1, g2, 0.0))
    idx_ref[sq] = jnp.where(lane == 0, i1, jnp.where(lane == 1, i2, 0))


def _pool_layer(prev_pad, x, g1, w_pool, scale, g2, w_router_pad, *, tm, pos0, n_experts):
    bsz, t_len, d = x.shape
    assert all(w == 2 ** (g + 1) for g, w in enumerate(POOL_WINDOWS)) and POOL_HALO >= max(POOL_WINDOWS) - 1
    n_t = t_len // tm
    has_halo = n_t > 1
    hb = tm // POOL_HALO if has_halo else 1
    n_rows = SUBLANES + POOL_HALO + tm
    tail = min(tm, POOL_HALO)
    seqs = _seqs_per_step(bsz, n_t)
    cur = lambda b, t: (b, t, 0)
    full2 = lambda b, t: (0, 0)
    in_specs = [pl.BlockSpec((seqs, POOL_HALO, d), lambda b, t: (b, 0, 0))]
    args = [prev_pad]
    if has_halo:
        in_specs.append(pl.BlockSpec((seqs, POOL_HALO, d), lambda b, t: (b, jnp.maximum(t * hb - 1, 0), 0)))
        args.append(x)
    in_specs += [
        pl.BlockSpec((seqs, tm, d), cur),
        pl.BlockSpec((1, d), full2),
        pl.BlockSpec(w_pool.shape, lambda b, t: (0, 0, 0)),
        pl.BlockSpec((1, d), full2),
        pl.BlockSpec((1, d), full2),
        pl.BlockSpec(w_router_pad.shape, full2),
    ]
    args += [x, g1, w_pool, scale, g2, w_router_pad]
    return pl.pallas_call(
        functools.partial(_pool_kernel, tm=tm, pos0=pos0, windows=POOL_WINDOWS, has_halo=has_halo,
                          n_experts=n_experts, seqs=seqs),
        grid=(bsz // seqs, n_t),
        in_specs=in_specs,
        out_specs=[
            pl.BlockSpec((seqs, tm, d), cur),
            pl.BlockSpec((seqs, tm, d), cur),
            pl.BlockSpec((seqs, tail, d), lambda b, t: (b, 0, 0)),
            pl.BlockSpec((seqs, tm, LANES), cur),
            pl.BlockSpec((seqs, tm, LANES), cur),
        ],
        out_shape=[
            jax.ShapeDtypeStruct((bsz, t_len, d), F32),
            jax.ShapeDtypeStruct((bsz, t_len, d), F32),
            jax.ShapeDtypeStruct((bsz, tail, d), F32),
            jax.ShapeDtypeStruct((bsz, t_len, LANES), F32),
            jax.ShapeDtypeStruct((bsz, t_len, LANES), jnp.int32),
        ],
        scratch_shapes=[pltpu.VMEM((n_rows, d), F32), pltpu.VMEM((len(POOL_WINDOWS) - 1, n_rows, d), F32)],
        compiler_params=_cparams("parallel", "arbitrary"),
        name="pool_layer",
    )(*args)


def _combine_kernel(pos_cur_ref, pos_nxt_ref, x_ref, gate_ref, g_ref, ys_ref, y_ref, buf_ref, sem,
                    *, tt, apply_norm):
    i = pl.program_id(0)
    n = pl.num_programs(0)
    slot = i % 2

    def fetch(pos_ref, s):
        def body(r, carry):
            for k in range(TOP_K):
                _row_copy(ys_ref, pos_ref[0, TOP_K * r + k], buf_ref.at[s, k], r,
                          sem.at[s]).start(priority=k % 2)
            return carry
        lax.fori_loop(0, tt, body, 0, unroll=8)

    @pl.when(i == 0)
    def _():
        fetch(pos_cur_ref, 0)

    @pl.when(i + 1 < n)
    def _():
        fetch(pos_nxt_ref, 1 - slot)

    for k in range(TOP_K):
        pltpu.make_async_copy(ys_ref.at[pl.ds(0, tt)], buf_ref.at[slot, k], sem.at[slot]).wait()
    moe = gate_ref[:, 0:1] * buf_ref[slot, 0]
    for k in range(1, TOP_K):
        moe = moe + gate_ref[:, k:k + 1] * buf_ref[slot, k]
    y = x_ref[...] + moe
    y_ref[...] = _rms(y, g_ref[...]) if apply_norm else y


def _combine(pos_tiles, x, gates, g, ys, *, apply_norm):
    n, d = x.shape
    n_tiles = pos_tiles.shape[0]
    tt = n // n_tiles
    pos_spec = lambda fn: pl.BlockSpec((None, 1, TOP_K * tt), fn, memory_space=pltpu.SMEM)
    return pl.pallas_call(
        functools.partial(_combine_kernel, tt=tt, apply_norm=apply_norm),
        grid=(n_tiles,),
        in_specs=[
            pos_spec(lambda i: (i, 0, 0)),
            pos_spec(lambda i: (jnp.minimum(i + 1, n_tiles - 1), 0, 0)),
            pl.BlockSpec((tt, d), lambda i: (i, 0)),
            pl.BlockSpec((tt, LANES), lambda i: (i, 0)),
            pl.BlockSpec((1, d), lambda i: (0, 0)),
            pl.BlockSpec(memory_space=pl.ANY),
        ],
        out_specs=pl.BlockSpec((tt, d), lambda i: (i, 0)),
        out_shape=jax.ShapeDtypeStruct((n, d), F32),
        scratch_shapes=[pltpu.VMEM((2, TOP_K, tt, d), F32), pltpu.SemaphoreType.DMA((2,))],
        compiler_params=_cparams("arbitrary"),
        name="moe_combine",
    )(pos_tiles, pos_tiles, x, gates, g, ys)


def _final_kernel(x_ref, g_ref, y_ref):
    y_ref[...] = _rms(x_ref[...], g_ref[...])


def _final(x, g, *, tm):
    n, d = x.shape
    row = lambda i: (i, 0)
    return pl.pallas_call(
        _final_kernel,
        grid=(n // tm,),
        in_specs=[pl.BlockSpec((tm, d), row), pl.BlockSpec((1, d), lambda i: (0, 0))],
        out_specs=pl.BlockSpec((tm, d), row),
        out_shape=jax.ShapeDtypeStruct((n, d), F32),
        compiler_params=_cparams("parallel"),
        name="final_norm",
    )(x, g)


def _swap_halves(w):
    half = w.shape[-1] // 2
    return jnp.concatenate([w[..., half:], w[..., :half]], axis=-1)


def _slot(parts, total=HEAD_SLOT):
    used = sum(p.shape[-1] for p in parts)
    pad = jnp.zeros(parts[0].shape[:-1] + (total - used,), parts[0].dtype)
    return jnp.concatenate(list(parts) + [pad], axis=-1)


def _prep_even_weights(w_in, w_uq, w_ukv, w_out, *, dc, qr, kvr):
    d = w_in.shape[0]
    body = w_in[:, :2 * dc + qr + kvr]
    w_kpe = w_in[:, 2 * dc + qr + kvr:]
    z_nope = jnp.zeros((d, NOPE_DIM), w_in.dtype)
    w_in_ext = jnp.concatenate(
        [body, _slot([z_nope, w_kpe]), _slot([z_nope, _swap_halves(w_kpe)])], axis=-1).astype(BF16)

    uq = w_uq.reshape(qr, N_HEADS, NOPE_DIM + ROPE_DIM)
    uq_nope, uq_pe = uq[..., :NOPE_DIM], uq[..., NOPE_DIM:]
    qa = _slot([uq_nope, uq_pe]).reshape(qr, N_HEADS * HEAD_SLOT)
    qb = _slot([jnp.zeros_like(uq_nope), _swap_halves(uq_pe)]).reshape(qr, N_HEADS * HEAD_SLOT)
    w_uq_ext = jnp.concatenate([qa, qb], axis=-1).astype(BF16)

    ukv = w_ukv.reshape(kvr, N_HEADS, NOPE_DIM + V_DIM)
    uk, uv = ukv[..., :NOPE_DIM], ukv[..., NOPE_DIM:]
    w_kv_ext = jnp.concatenate([_slot([uk]).reshape(kvr, N_HEADS * HEAD_SLOT),
                                _slot([uv]).reshape(kvr, N_HEADS * HEAD_SLOT)], axis=-1).astype(BF16)
    w_q2lat = jnp.concatenate(
        [jnp.transpose(uk, (1, 2, 0)),
         jnp.zeros((N_HEADS, HEAD_SLOT - NOPE_DIM, kvr), w_ukv.dtype)], axis=1).astype(BF16)
    w_uv_pad = _slot([jnp.transpose(uv, (1, 0, 2))]).astype(BF16)

    w_c = w_out[:dc].astype(BF16)
    wo = w_out[dc:].reshape(N_HEADS, V_DIM, d)
    w_o = jnp.concatenate([wo, jnp.zeros((N_HEADS, HEAD_SLOT - V_DIM, d), w_out.dtype)],
                          axis=1).reshape(N_HEADS * HEAD_SLOT, d).astype(BF16)
    return w_in_ext, w_uq_ext, w_kv_ext, w_q2lat, w_uv_pad, w_c, w_o


def _rope_tables(pos):
    half = ROPE_DIM // 2
    inv_freq = jnp.power(ROPE_THETA, -jnp.arange(half, dtype=F32) / half)
    ang = pos.astype(F32)[:, None] * inv_freq[None, :]
    cos, sin = jnp.cos(ang), jnp.sin(ang)
    n = pos.shape[0]
    z_nope = jnp.zeros((n, NOPE_DIM), F32)
    cos_k = _slot([z_nope, cos, cos])
    sin_k = _slot([z_nope, -sin, sin])
    cos_q = _slot([jnp.ones((n, NOPE_DIM), F32), cos, cos]) * (ATTN_SCALE * LOG2E)
    sin_q = sin_k * (ATTN_SCALE * LOG2E)
    return cos_k, sin_k, cos_q, sin_q


def _even_layer(x, conv_prev, pos, attend, P, layer, i):
    bsz, t_len, d = x.shape
    n = bsz * t_len
    dc = P['conv_w'].shape[2]
    qr = P['q_norm'].shape[1]
    kvr = P['kv_norm'].shape[1]
    k_taps = P['conv_w'].shape[1]
    tm = next((t for t in (PROJ_TILE_ROWS, FFN_TILE_ROWS) if t_len % t == 0), n)
    w_in_ext, w_uq_ext, w_kv_ext, w_q2lat, w_uv_pad, w_c, w_o = _prep_even_weights(
        P['w_in'][i], P['w_uq'][i], P['w_ukv'][i], P['w_out'][i], dc=dc, qr=qr, kvr=kvr)
    cos_k, sin_k, cos_q, sin_q = _rope_tables(pos)
    if t_len < tm:
        reps = tm // t_len
        cos_k, sin_k, cos_q, sin_q = (jnp.tile(a, (reps, 1)) for a in (cos_k, sin_k, cos_q, sin_q))

    xf = x.reshape(n, d)
    u, qn, ckv, kpe_slot = _in_proj(
        xf, P['norm_mix'][layer][None], w_in_ext, P['q_norm'][i][None], P['kv_norm'][i][None],
        cos_k, sin_k, dc=dc, qr=qr, kvr=kvr, tm=tm)

    u3 = u.reshape(bsz, t_len, dc)
    prev_pad = jnp.pad(conv_prev.astype(F32), ((0, 0), (CONV_HALO - (k_taps - 1), 0), (0, 0)))
    c = _conv_branch(prev_pad, u3, P['conv_w'][i], P['conv_b'][i][None], P['conv_ln_g'][i][None],
                     P['conv_ln_b'][i][None], tc=min(t_len, 512))
    conv_state = jnp.concatenate([conv_prev.astype(F32), u3], axis=1)[:, -(k_taps - 1):]

    q_slots = _q_proj(qn, w_uq_ext, cos_q, sin_q, tm=tm)
    o_slots = attend(q_slots, ckv, kpe_slot, w_kv_ext, w_q2lat, w_uv_pad, tm)

    x1, h_ffn = _out_proj(xf, c.reshape(n, dc), o_slots, w_c, w_o, P['norm_ffn'][layer][None], tm=tm)

    x2 = _ffn(h_ffn, x1, P['ffn_w_gate'][i].astype(BF16), P['ffn_w_up'][i].astype(BF16),
              P['ffn_w_down'][i].astype(BF16), tm=_row_tile(n, FFN_TILE_ROWS))
    kpe = kpe_slot[:, NOPE_DIM:NOPE_DIM + ROPE_DIM]
    return (x2.reshape(bsz, t_len, d), conv_state, ckv.reshape(bsz, t_len, kvr),
            kpe.reshape(bsz, t_len, ROPE_DIM))


def _make_attend_prompt(bsz, t_len):
    def attend(q_slots, ckv, kpe_slot, w_kv_ext, w_q2lat, w_uv_pad, tm):
        k_slots, v_slots = _kv_proj(ckv, kpe_slot, w_kv_ext, tm=tm)
        return _attn_prompt(q_slots, k_slots, v_slots, bsz=bsz, t_len=t_len, tq=min(t_len, 512))
    return attend


def _make_attend_sample(bsz, t_len, cache_ckv, cache_kpe_t, page_table, i):
    t_pad = 8

    def attend(q_slots, ckv, kpe_slot, w_kv_ext, w_q2lat, w_uv_pad, tm):
        n = bsz * t_len
        kvr = ckv.shape[1]
        q_lat = _qlat(q_slots, w_q2lat)
        q_lat = q_lat.reshape(N_HEADS, bsz, t_len, kvr)
        q_lat = jnp.pad(q_lat, ((0, 0), (0, 0), (0, t_pad - t_len), (0, 0)))
        q_lat = jnp.transpose(q_lat, (1, 0, 2, 3)).reshape(bsz, N_HEADS * t_pad, kvr)
        q_pe = q_slots.reshape(bsz, t_len, N_HEADS, HEAD_SLOT)[..., NOPE_DIM:NOPE_DIM + ROPE_DIM]
        q_pe = jnp.pad(jnp.transpose(q_pe, (0, 2, 1, 3)), ((0, 0), (0, 0), (0, t_pad - t_len), (0, 0)))
        q_pe = q_pe.reshape(bsz, N_HEADS * t_pad, ROPE_DIM)
        pad_t = ((0, 0), (0, t_pad - t_len), (0, 0))
        ckv_new = jnp.pad(ckv.reshape(bsz, t_len, kvr), pad_t)
        kpe_new = jnp.pad(kpe_slot[:, NOPE_DIM:NOPE_DIM + ROPE_DIM].reshape(bsz, t_len, ROPE_DIM), pad_t)
        o_lat = _attn_sample(page_table, q_lat, q_pe, ckv_new, kpe_new, cache_ckv, cache_kpe_t, i,
                             t_new=t_len, t_pad=t_pad, pages_per_step=64,
                             pages_per_chunk=64)
        o_lat = o_lat.reshape(bsz, N_HEADS, t_pad, kvr)[:, :, :t_len]
        o_lat = jnp.transpose(o_lat, (1, 0, 2, 3)).reshape(N_HEADS, n, kvr)
        return _olat(o_lat, w_uv_pad)
    return attend


def _odd_layer_mixer(x, pool_prev, pos0, P, layer, i):
    bsz, t_len, d = x.shape
    n_experts = P['router_w'].shape[2]
    ctx = max(POOL_WINDOWS) - 1
    prev_pad = jnp.pad(pool_prev.astype(F32), ((0, 0), (POOL_HALO - ctx, 0), (0, 0)))
    w_router_pad = jnp.pad(P['router_w'][i], ((0, 0), (0, LANES - n_experts)))
    x3, h_moe, hn, gates, ids = _pool_layer(
        prev_pad, x, P['norm_mix'][layer][None], P['pool_w'][i].astype(BF16), P['pool_scale'][i][None],
        P['norm_ffn'][layer][None], w_router_pad, tm=min(t_len, 512), pos0=pos0, n_experts=n_experts)
    pool_state = jnp.concatenate([pool_prev.astype(F32), hn], axis=1)[:, -ctx:]
    n = bsz * t_len
    return (x3.reshape(n, d), h_moe.reshape(n, d), pool_state, gates.reshape(n, LANES),
            ids.reshape(n, LANES)[:, :TOP_K])


def _moe_plan(ids, n_experts, tm):
    n = ids.shape[0]
    flat = ids.reshape(-1)
    experts = jnp.arange(n_experts, dtype=jnp.int32)
    onehot = (flat[:, None] == experts[None, :]).astype(jnp.int32)
    csum = jnp.cumsum(onehot, axis=0)
    counts = csum[-1]
    rank = jnp.sum(csum * onehot, axis=1) - 1
    padded = ((counts + tm - 1) // tm) * tm
    ends = jnp.cumsum(padded)
    starts = ends - padded
    pos = jnp.sum(starts[None, :] * onehot, axis=1) + rank
    n_tiles = (n * TOP_K + tm - 1) // tm + n_experts
    n_valid = (ends[-1] // tm).astype(jnp.int32)
    tile_start = jnp.arange(n_tiles, dtype=jnp.int32) * tm
    tile_expert = jnp.sum((ends[None, :] <= tile_start[:, None]).astype(jnp.int32), axis=1)
    tile_expert = jnp.minimum(tile_expert, n_experts - 1)
    group_end = starts + counts
    tile_onehot = (tile_expert[:, None] == experts[None, :]).astype(jnp.int32)
    tile_rows = jnp.clip(jnp.sum(group_end[None, :] * tile_onehot, axis=1) - tile_start, 0, tm)
    return pos.reshape(n, TOP_K), tile_expert, n_valid.reshape(1), tile_rows.astype(jnp.int32), n_tiles


def _moe_layer(groups, P, i, g_final, *, tm, tt):
    d = groups[0][0].shape[1]
    n_experts = P['router_w'].shape[2]
    ids = jnp.concatenate([g[3] for g in groups], axis=0)
    pos, tile_expert, n_valid, tile_rows, n_tiles = _moe_plan(ids, n_experts, tm)
    xs = jnp.zeros((n_tiles * tm, d), F32)
    pos_tiles = []
    lo = 0
    for x3, h, _, _ in groups:
        n = x3.shape[0]
        t = min(tt, n)
        pt = pos[lo:lo + n].reshape(n // t, 1, TOP_K * t)
        xs = _dispatch(pt, h, xs)
        pos_tiles.append(pt)
        lo += n
    ys = _swiglu(tile_expert, n_valid, tile_rows, xs, P['moe_w_gate'][i], P['moe_w_up'][i],
                 P['moe_w_down'][i], tm=tm, tf=512)
    gain = g_final if g_final is not None else jnp.ones((1, d), F32)
    return [_combine(pt, x3, gates, gain, ys, apply_norm=g_final is not None)
            for pt, (x3, _, gates, _) in zip(pos_tiles, groups)]


def kernel(x_prompt, x_sample, cache_ckv, cache_kpe, page_table, state_conv, state_pool, norm_mix, norm_ffn,
           norm_final, w_in, conv_w, conv_b, conv_ln_g, conv_ln_b, q_norm, w_uq, kv_norm, w_ukv, w_out,
           ffn_w_gate, ffn_w_up, ffn_w_down, pool_w, pool_scale, router_w, moe_w_gate, moe_w_up, moe_w_down):
    P = {'norm_mix': norm_mix, 'norm_ffn': norm_ffn, 'norm_final': norm_final, 'w_in': w_in,
         'conv_w': conv_w, 'conv_b': conv_b, 'conv_ln_g': conv_ln_g, 'conv_ln_b': conv_ln_b,
         'q_norm': q_norm, 'w_uq': w_uq, 'kv_norm': kv_norm, 'w_ukv': w_ukv, 'w_out': w_out,
         'ffn_w_gate': ffn_w_gate, 'ffn_w_up': ffn_w_up, 'ffn_w_down': ffn_w_down,
         'pool_w': pool_w, 'pool_scale': pool_scale, 'router_w': router_w,
         'moe_w_gate': moe_w_gate, 'moe_w_up': moe_w_up, 'moe_w_down': moe_w_down}
    depth = norm_mix.shape[0]
    d = x_prompt.shape[2]
    bp, tp, _ = x_prompt.shape
    bs, ts, _ = x_sample.shape
    past = page_table.shape[1] * cache_ckv.shape[2]
    k_taps = conv_w.shape[1]
    ctx = max(POOL_WINDOWS) - 1
    pos_p = jnp.arange(tp, dtype=jnp.int32)
    pos_s = past + jnp.arange(ts, dtype=jnp.int32)
    cache_kpe_t = jnp.swapaxes(cache_kpe, 2, 3)

    xp, xs = x_prompt, x_sample
    outs_p = {'ckv': [], 'kpe': [], 'conv': [], 'pool': []}
    outs_s = {'ckv': [], 'kpe': [], 'conv': [], 'pool': []}
    for layer in range(depth):
        i = layer // 2
        last = layer == depth - 1
        if layer % 2 == 0:
            conv0 = jnp.zeros((bp, k_taps - 1, conv_w.shape[2]), F32)
            xp, cst, ckv, kpe = _even_layer(xp, conv0, pos_p, _make_attend_prompt(bp, tp), P, layer, i)
            outs_p['conv'].append(cst); outs_p['ckv'].append(ckv); outs_p['kpe'].append(kpe)
            xs, cst, ckv, kpe = _even_layer(
                xs, state_conv[i], pos_s, _make_attend_sample(bs, ts, cache_ckv, cache_kpe_t, page_table, i),
                P, layer, i)
            outs_s['conv'].append(cst); outs_s['ckv'].append(ckv); outs_s['kpe'].append(kpe)
            if last:
                xp = _final(xp.reshape(bp * tp, d), norm_final[None], tm=_row_tile(bp * tp, 512)).reshape(xp.shape)
                xs = _final(xs.reshape(bs * ts, d), norm_final[None], tm=_row_tile(bs * ts, 512)).reshape(xs.shape)
        else:
            pool0 = jnp.zeros((bp, ctx, d), F32)
            x3p, hp, pst_p, gp, ip = _odd_layer_mixer(xp, pool0, 0, P, layer, i)
            x3s, hs, pst_s, gs, is_ = _odd_layer_mixer(xs, state_pool[i], past, P, layer, i)
            outs_p['pool'].append(pst_p); outs_s['pool'].append(pst_s)
            yp, ysm = _moe_layer([(x3p, hp, gp, ip), (x3s, hs, gs, is_)], P, i,
                                 norm_final[None] if last else None, tm=1024, tt=512)
            xp, xs = yp.reshape(xp.shape), ysm.reshape(xs.shape)

    def stack(lst, shape_if_empty):
        return jnp.stack(lst) if lst else jnp.zeros(shape_if_empty, F32)

    return (xp, xs,
            jnp.stack(outs_p['ckv']), jnp.stack(outs_p['kpe']), jnp.stack(outs_p['conv']),
            stack(outs_p['pool'], (0, bp, ctx, d)),
            jnp.stack(outs_s['ckv']), jnp.stack(outs_s['kpe']), jnp.stack(outs_s['conv']),
            stack(outs_s['pool'], (0, bs, ctx, d)))
```
